```python
import jax, jax.numpy as jnp
from jax import lax
import numpy as np

D_MODEL = 1024
BATCH = 4
SEQ = 4096
DEPTH = 4
DEC_BATCH = 16
DEC_SEQ = 2048
PAST_LEN = 128

ATT_HEADS = 16
ATT_KV_HEADS = 4
ATT_GROUP = ATT_HEADS // ATT_KV_HEADS
ATT_HEAD_DIM = 64
ATT_WIDTH = ATT_HEADS * ATT_HEAD_DIM
ATT_KV_WIDTH = ATT_KV_HEADS * ATT_HEAD_DIM
WINDOW = 128
ATT_BLOCK = 128
ROPE_THETA = 500000.0
ROPE_DIM = ATT_HEAD_DIM // 4
M_HEADS = 8
M_HEAD_DIM = 128
M_WIDTH = M_HEADS * M_HEAD_DIM
M_CHUNK = 128
CONV_K = 3
N_BRANCH = 2
NORM_EPS = 1e-6
NEG = -1e30

SPLITS = (ATT_WIDTH, ATT_KV_WIDTH, ATT_KV_WIDTH, ATT_WIDTH,
          M_WIDTH, M_WIDTH, M_WIDTH, M_WIDTH, M_WIDTH,
          M_HEADS, M_HEADS, M_HEADS, M_HEADS,
          N_BRANCH * D_MODEL)
IN_DIM = sum(SPLITS)

kernel_name = "hybrid_swa_mlstm_bidir_encoder"


def rmsnorm(x, g):
    xf = x.astype(jnp.float32)
    y = xf * lax.rsqrt(jnp.mean(xf * xf, axis=-1, keepdims=True) + NORM_EPS) * g.astype(jnp.float32)
    return y.astype(x.dtype)


def partial_rope(x):
    S = x.shape[1]
    half = ROPE_DIM // 2
    inv = jnp.power(jnp.float32(ROPE_THETA), -jnp.arange(half, dtype=jnp.float32) * 2.0 / ROPE_DIM)
    ang = jnp.arange(S, dtype=jnp.float32)[:, None] * inv[None, :]
    cos = jnp.cos(ang)[None, :, None, :]
    sin = jnp.sin(ang)[None, :, None, :]
    xf = x.astype(jnp.float32)
    x1 = xf[..., :half]
    x2 = xf[..., half:ROPE_DIM]
    out = jnp.concatenate([x1 * cos - x2 * sin, x2 * cos + x1 * sin, xf[..., ROPE_DIM:]], axis=-1)
    return out.astype(x.dtype)


def windowed_gqa_sink(q, k, v, sink):
    B, S = q.shape[0], q.shape[1]
    nb = S // ATT_BLOCK
    qb = q.reshape(B, nb, ATT_BLOCK, ATT_KV_HEADS, ATT_GROUP, ATT_HEAD_DIM)

    def band(t):
        tp = jnp.pad(t, ((0, 0), (ATT_BLOCK, ATT_BLOCK), (0, 0), (0, 0)))
        tp = tp.reshape(B, nb + 2, ATT_BLOCK, ATT_KV_HEADS, ATT_HEAD_DIM)
        return jnp.concatenate([tp[:, :-2], tp[:, 1:-1], tp[:, 2:]], axis=2)

    kb, vb = band(k), band(v)
    s = jnp.einsum("bnqhgd,bnkhd->bnhgqk", qb, kb).astype(jnp.float32) * (ATT_HEAD_DIM ** -0.5)
    blk = jnp.arange(nb)[:, None, None]
    qpos = blk * ATT_BLOCK + jnp.arange(ATT_BLOCK)[None, :, None]
    kpos = (blk - 1) * ATT_BLOCK + jnp.arange(3 * ATT_BLOCK)[None, None, :]
    valid = (jnp.abs(kpos - qpos) <= WINDOW) & (kpos >= 0) & (kpos < S)
    s = jnp.where(valid[None, :, None, None], s, NEG)
    sink_l = jnp.broadcast_to(sink.astype(jnp.float32).reshape(1, 1, ATT_KV_HEADS, ATT_GROUP, 1, 1), s.shape[:-1] + (1,))
    p = jax.nn.softmax(jnp.concatenate([s, sink_l], axis=-1), axis=-1)[..., :-1]
    o = jnp.einsum("bnhgqk,bnkhd->bnqhgd", p.astype(v.dtype), vb)
    return o.reshape(B, S, ATT_WIDTH)


def centred_conv(x, w):
    S = x.shape[1]
    pad = CONV_K // 2
    xp = jnp.pad(x, ((0, 0), (pad, pad), (0, 0)))
    out = xp[:, 0:S] * w[0]
    for j in range(1, CONV_K):
        out = out + xp[:, j:j + S] * w[j]
    return out


def mlstm_chunkwise(q, k, v, log_i, log_f):
    B, H, S, dk = q.shape
    dv = v.shape[-1]
    L = M_CHUNK
    nc = S // L

    def chunks(t):
        return jnp.moveaxis(t.reshape((B, H, nc, L) + t.shape[3:]), 2, 0)

    qc, kc, vc, lic, lfc = chunks(q), chunks(k), chunks(v), chunks(log_i), chunks(log_f)
    b = jnp.cumsum(lfc, axis=-1)
    lower = jnp.tril(jnp.ones((L, L), dtype=bool))
    log_d = jnp.where(lower, b[..., :, None] - b[..., None, :] + lic[..., None, :], NEG)
    g = b[..., -1:] - b + lic
    qk = jnp.einsum("nbhld,nbhsd->nbhls", qc, kc)

    def step(carry, inp):
        C, n, m = carry
        q_, k_, v_, ld, qk_, b_, g_ = inp
        m_inter = b_ + m[..., None]
        m_t = jnp.maximum(m_inter, ld.max(axis=-1))
        w_inter = jnp.exp(m_inter - m_t)
        p = jnp.exp(ld - m_t[..., None]) * qk_
        num = w_inter[..., None] * jnp.einsum("bhld,bhde->bhle", q_, C) + jnp.einsum("bhls,bhse->bhle", p, v_)
        den = w_inter * jnp.einsum("bhld,bhd->bhl", q_, n) + p.sum(axis=-1)
        h = num / jnp.maximum(jnp.abs(den), jnp.exp(-m_t))[..., None]
        b_last = b_[..., -1]
        m_new = jnp.maximum(b_last + m, g_.max(axis=-1))
        w_c = jnp.exp(b_last + m - m_new)
        w_k = jnp.exp(g_ - m_new[..., None])
        C = w_c[..., None, None] * C + jnp.einsum("bhl,bhld,bhle->bhde", w_k, k_, v_)
        n = w_c[..., None] * n + jnp.einsum("bhl,bhld->bhd", w_k, k_)
        return (C, n, m_new), h

    init = (jnp.zeros((B, H, dk, dv), jnp.float32), jnp.zeros((B, H, dk), jnp.float32),
            jnp.full((B, H), NEG, jnp.float32))
    _, h = lax.scan(step, init, (qc, kc, vc, log_d, qk, b, g))
    return jnp.moveaxis(h, 0, 2).reshape(B, H, S, dv)


def mlstm_bidir(q, k, v, i_f, f_f, i_b, f_b):
    def to_bhs(t):
        return jnp.moveaxis(t.astype(jnp.float32), 1, 2)

    def flip(t):
        return jnp.flip(t, axis=2)

    q, k, v = to_bhs(q), to_bhs(k) * (M_HEAD_DIM ** -0.5), to_bhs(v)
    h_fwd = mlstm_chunkwise(q, k, v, to_bhs(i_f), jax.nn.log_sigmoid(to_bhs(f_f)))
    h_bwd = flip(mlstm_chunkwise(flip(q), flip(k), flip(v), flip(to_bhs(i_b)),
                                 flip(jax.nn.log_sigmoid(to_bhs(f_b)))))
    return jnp.moveaxis(h_fwd + h_bwd, 2, 1)


def encoder_layer(x, norm_g, w_in, b_in, q_norm_g, k_norm_g, sink, conv_w, m_norm_g, w_att_out, w_m_out, w_out):
    dt = x.dtype
    B, S, _ = x.shape
    xn = rmsnorm(x, norm_g)
    proj = xn @ w_in + b_in
    idx = [int(i) for i in np.cumsum(SPLITS)[:-1]]
    (aq, ak, av, az, mq, mk, mv, mo, mz, i_f, f_f, i_b, f_b, gates) = jnp.split(proj, idx, axis=-1)

    aq = partial_rope(rmsnorm(aq.reshape(B, S, ATT_HEADS, ATT_HEAD_DIM), q_norm_g))
    ak = partial_rope(rmsnorm(ak.reshape(B, S, ATT_KV_HEADS, ATT_HEAD_DIM), k_norm_g))
    av = av.reshape(B, S, ATT_KV_HEADS, ATT_HEAD_DIM)
    att = windowed_gqa_sink(aq, ak, av, sink)
    branch_a = (att * jax.nn.silu(az)) @ w_att_out

    qk_m = jax.nn.silu(centred_conv(jnp.concatenate([mq, mk], axis=-1), conv_w))
    mq, mk = qk_m[..., :M_WIDTH], qk_m[..., M_WIDTH:]
    hs = (M_HEADS, M_HEAD_DIM)
    h = mlstm_bidir(mq.reshape(B, S, *hs), mk.reshape(B, S, *hs), mv.reshape(B, S, *hs), i_f, f_f, i_b, f_b)
    h = jax.nn.sigmoid(mo.astype(jnp.float32)).reshape(B, S, *hs) * h
    h = h * lax.rsqrt(jnp.mean(h * h, axis=-1, keepdims=True) + NORM_EPS) * m_norm_g.astype(jnp.float32).reshape(hs)
    h = h.reshape(B, S, M_WIDTH).astype(dt)
    branch_m = (h * jax.nn.silu(mz)) @ w_m_out

    gates = jax.nn.sigmoid(gates)
    merged = gates[..., :D_MODEL] * branch_a + gates[..., D_MODEL:] * branch_m
    return (x + merged @ w_out).astype(dt)


def trunk(x, norm_g, w_in, b_in, q_norm_g, k_norm_g, sink, conv_w, m_norm_g, w_att_out, w_m_out, w_out):
    for l in range(DEPTH):
        x = encoder_layer(x, norm_g[l], w_in[l], b_in[l], q_norm_g[l], k_norm_g[l], sink[l], conv_w[l],
                          m_norm_g[l], w_att_out[l], w_m_out[l], w_out[l])
    return x


def setup_inputs(seed: int = 0) -> dict:
    key = jax.random.key(seed)
    ks = jax.random.split(key, 16)
    f32 = jnp.float32
    nrm = jax.random.normal
    x_prompt = nrm(ks[0], (BATCH, SEQ, D_MODEL), f32)
    x_sample = nrm(ks[1], (DEC_BATCH, DEC_SEQ, D_MODEL), f32)
    norm_g = 1.0 + 0.02 * nrm(ks[2], (DEPTH, D_MODEL), f32)
    w_in = nrm(ks[3], (DEPTH, D_MODEL, IN_DIM), f32) * (D_MODEL ** -0.5)
    b_in = 0.01 * nrm(ks[4], (DEPTH, IN_DIM), f32)
    f_bias = jnp.linspace(3.0, 6.0, M_HEADS, dtype=f32) + 0.1 * nrm(ks[5], (DEPTH, 2, M_HEADS), f32)
    off_ff = sum(SPLITS[:10])
    off_fb = sum(SPLITS[:12])
    b_in = b_in.at[:, off_ff:off_ff + M_HEADS].set(f_bias[:, 0]).at[:, off_fb:off_fb + M_HEADS].set(f_bias[:, 1])
    q_norm_g = 1.0 + 0.02 * nrm(ks[6], (DEPTH, ATT_HEAD_DIM), f32)
    k_norm_g = 1.0 + 0.02 * nrm(ks[7], (DEPTH, ATT_HEAD_DIM), f32)
    sink = 0.5 * nrm(ks[8], (DEPTH, ATT_HEADS), f32)
    conv_w = nrm(ks[9], (DEPTH, CONV_K, 2 * M_WIDTH), f32) * (CONV_K ** -0.5)
    m_norm_g = 1.0 + 0.02 * nrm(ks[10], (DEPTH, M_WIDTH), f32)
    w_att_out = nrm(ks[11], (DEPTH, ATT_WIDTH, D_MODEL), f32) * (ATT_WIDTH ** -0.5)
    w_m_out = nrm(ks[12], (DEPTH, M_WIDTH, D_MODEL), f32) * (M_WIDTH ** -0.5)
    w_out = nrm(ks[13], (DEPTH, D_MODEL, D_MODEL), f32) * (D_MODEL ** -0.5)
    return {"x_prompt": x_prompt, "x_sample": x_sample, "norm_g": norm_g, "w_in": w_in, "b_in": b_in,
            "q_norm_g": q_norm_g, "k_norm_g": k_norm_g, "sink": sink, "conv_w": conv_w,
            "m_norm_g": m_norm_g, "w_att_out": w_att_out, "w_m_out": w_m_out, "w_out": w_out}


def reference(x_prompt, x_sample, norm_g, w_in, b_in, q_norm_g, k_norm_g, sink, conv_w, m_norm_g, w_att_out, w_m_out, w_out):
    y_prompt = trunk(x_prompt, norm_g, w_in, b_in, q_norm_g, k_norm_g, sink, conv_w, m_norm_g, w_att_out, w_m_out, w_out)
    y_sample = trunk(x_sample, norm_g, w_in, b_in, q_norm_g, k_norm_g, sink, conv_w, m_norm_g, w_att_out, w_m_out, w_out)
    return (y_prompt, y_sample)
```

```python
import functools

import jax
import jax.numpy as jnp
from jax import lax
from jax.experimental import pallas as pl
from jax.experimental.pallas import tpu as pltpu

F32 = jnp.float32
BF16 = jnp.bfloat16

D_MODEL = 1024
DEPTH = 4
ATT_HEADS = 16
ATT_KV_HEADS = 4
ATT_GROUP = ATT_HEADS // ATT_KV_HEADS
ATT_HEAD_DIM = 64
ATT_WIDTH = ATT_HEADS * ATT_HEAD_DIM
ATT_KV_WIDTH = ATT_KV_HEADS * ATT_HEAD_DIM
WINDOW = 128
ATT_BLOCK = 128
ROPE_THETA = 500000.0
ROPE_DIM = ATT_HEAD_DIM // 4
ROPE_HALF = ROPE_DIM // 2
M_HEADS = 8
M_HEAD_DIM = 128
M_WIDTH = M_HEADS * M_HEAD_DIM
M_CHUNK = 128
CONV_K = 3
NORM_EPS = 1e-6
NEG = -1e30

LANES = 128
VMEM_LIMIT_BYTES = 56 * 1024 * 1024

_REF_SPLITS = (ATT_WIDTH, ATT_KV_WIDTH, ATT_KV_WIDTH, ATT_WIDTH,
               M_WIDTH, M_WIDTH, M_WIDTH, M_WIDTH, M_WIDTH,
               M_HEADS, M_HEADS, M_HEADS, M_HEADS, 2 * D_MODEL)
_REF_OFF = [0]
for _w in _REF_SPLITS:
    _REF_OFF.append(_REF_OFF[-1] + _w)
(_R_AQ, _R_AK, _R_AV, _R_AZ, _R_MQ, _R_MK, _R_MV, _R_MO, _R_MZ,
 _R_IF, _R_FF, _R_IB, _R_FB, _R_GATES) = _REF_OFF[:-1]

P_GATES = 0
P_AQ = 2 * D_MODEL
P_AZ = P_AQ + ATT_WIDTH
P_AK = P_AZ + ATT_WIDTH
P_AV = P_AK + ATT_KV_WIDTH
P_MQ = P_AV + ATT_KV_WIDTH
P_MK = P_MQ + M_WIDTH
P_MV = P_MK + M_WIDTH
P_MO = P_MV + M_WIDTH
P_MZ = P_MO + M_WIDTH
P_WIDTH = P_MZ + M_WIDTH
N_IFG = 4 * M_HEADS

ROW_LI_F, ROW_B_F, ROW_LI_B, ROW_B_B = 0, M_HEADS, 2 * M_HEADS, 3 * M_HEADS
LANE_B_F, LANE_B_B = M_HEADS, 3 * M_HEADS

PROJ_DTYPE = BF16
IN_TM = 512
IN_TN = 512
OUT_TM = 512


def _sigmoid(x):
    return 1.0 / (1.0 + jnp.exp(-x))


def _silu(x):
    return x * _sigmoid(x)


def _log_sigmoid(x):
    return jnp.minimum(x, 0.0) - jnp.log1p(jnp.exp(-jnp.abs(x)))


def _split_dot_left(tri, x):
    hi = x.astype(BF16)
    lo = (x - hi.astype(F32)).astype(BF16)
    return (jnp.dot(tri, hi, preferred_element_type=F32)
            + jnp.dot(tri, lo, preferred_element_type=F32))


def _split_dot_right(x, tri):
    hi = x.astype(BF16)
    lo = (x - hi.astype(F32)).astype(BF16)
    return (jnp.dot(hi, tri, preferred_element_type=F32)
            + jnp.dot(lo, tri, preferred_element_type=F32))


def _in_proj_kernel(x_ref, g_ref, w_ref, b_ref, wif_ref, wift_ref, bif_ref, bifc_ref,
                    proj_ref, grow_ref, bcol_ref):
    x = x_ref[...]
    xn = x * lax.rsqrt(jnp.mean(x * x, axis=-1, keepdims=True) + NORM_EPS) * g_ref[...]
    xn = xn.astype(BF16)
    for j in range(P_WIDTH // IN_TN):
        cols = slice(j * IN_TN, (j + 1) * IN_TN)
        acc = jnp.dot(xn, w_ref[:, cols], preferred_element_type=F32) + b_ref[:, cols]
        proj_ref[:, cols] = acc.astype(proj_ref.dtype)

    gc = jnp.dot(xn, wif_ref[...], preferred_element_type=F32) + bif_ref[...]
    gr = lax.dot_general(wift_ref[...], xn, (((1,), (1,)), ((), ())),
                         preferred_element_type=F32) + bifc_ref[...]
    ls_c = _log_sigmoid(gc)
    ls_r = _log_sigmoid(gr)

    r = lax.broadcasted_iota(jnp.int32, (M_CHUNK, M_CHUNK), 0)
    c = lax.broadcasted_iota(jnp.int32, (M_CHUNK, M_CHUNK), 1)
    tril = (c <= r).astype(BF16)
    triu = (c >= r).astype(BF16)
    lane = lax.broadcasted_iota(jnp.int32, (M_CHUNK, LANES), 1)
    fwd_lanes = lane < 2 * M_HEADS

    grow_ref[ROW_LI_F:ROW_LI_F + M_HEADS, :] = gr[0:M_HEADS, :]
    grow_ref[ROW_LI_B:ROW_LI_B + M_HEADS, :] = gr[2 * M_HEADS:3 * M_HEADS, :]
    for ci in range(x_ref.shape[0] // M_CHUNK):
        rows = slice(ci * M_CHUNK, (ci + 1) * M_CHUNK)
        lsc = ls_c[rows, :]
        b_f = _split_dot_left(tril, lsc)
        b_b = _split_dot_left(triu, lsc)
        bcol_ref[rows, :] = jnp.where(fwd_lanes, b_f, b_b)
        grow_ref[ROW_B_F:ROW_B_F + M_HEADS, rows] = _split_dot_right(ls_r[M_HEADS:2 * M_HEADS, rows], triu)
        grow_ref[ROW_B_B:ROW_B_B + M_HEADS, rows] = _split_dot_right(ls_r[3 * M_HEADS:4 * M_HEADS, rows], tril)


def _in_proj(x, g, w, b, wif, wift, bif, bifc):
    t = x.shape[0]
    const = lambda i: (0, 0)
    return pl.pallas_call(
        _in_proj_kernel,
        grid=(t // IN_TM,),
        in_specs=[
            pl.BlockSpec((IN_TM, D_MODEL), lambda i: (i, 0)),
            pl.BlockSpec((1, D_MODEL), const),
            pl.BlockSpec((D_MODEL, P_WIDTH), const, pipeline_mode=pl.Buffered(1)),
            pl.BlockSpec((1, P_WIDTH), const),
            pl.BlockSpec((D_MODEL, LANES), const),
            pl.BlockSpec((N_IFG, D_MODEL), const),
            pl.BlockSpec((1, LANES), const),
            pl.BlockSpec((N_IFG, 1), const),
        ],
        out_specs=[
            pl.BlockSpec((IN_TM, P_WIDTH), lambda i: (i, 0)),
            pl.BlockSpec((N_IFG, IN_TM), lambda i: (0, i)),
            pl.BlockSpec((IN_TM, LANES), lambda i: (i, 0)),
        ],
        out_shape=[
            jax.ShapeDtypeStruct((t, P_WIDTH), PROJ_DTYPE),
            jax.ShapeDtypeStruct((N_IFG, t), F32),
            jax.ShapeDtypeStruct((t, LANES), F32),
        ],
        compiler_params=pltpu.CompilerParams(
            dimension_semantics=("arbitrary",), vmem_limit_bytes=VMEM_LIMIT_BYTES),
        name="in_proj",
    )(x, g, w, b, wif, wift, bif, bifc)


def _norm_rope_tile(x, gain, tab):
    lane = lax.broadcasted_iota(jnp.int32, x.shape, 1)
    left = lane < ATT_HEAD_DIM
    sq = x * x
    ss_l = jnp.sum(jnp.where(left, sq, 0.0), axis=-1, keepdims=True)
    ss_r = jnp.sum(jnp.where(left, 0.0, sq), axis=-1, keepdims=True)
    inv = jnp.where(left, lax.rsqrt(ss_l / ATT_HEAD_DIM + NORM_EPS),
                    lax.rsqrt(ss_r / ATT_HEAD_DIM + NORM_EPS))
    y = x * inv * gain
    return (y * tab[0]
            + pltpu.roll(y, LANES - ROPE_HALF, 1) * tab[1]
            + pltpu.roll(y, ROPE_HALF, 1) * tab[2])


def _attention_kernel(q_ref, z_ref, kp_ref, kc_ref, kn_ref, vp_ref, vc_ref, vn_ref,
                      tp_ref, tc_ref, tn_ref, gq_ref, gk_ref, sink_ref, out_ref,
                      qs, ks, vs, att, *, n_blocks):
    blk = pl.program_id(1)
    bq = ATT_BLOCK
    scale = ATT_HEAD_DIM ** -0.5

    tabs = (tp_ref[...], tc_ref[...], tn_ref[...])
    for t in range(ATT_WIDTH // LANES):
        cols = slice(t * LANES, (t + 1) * LANES)
        qt = _norm_rope_tile(q_ref[:, cols].astype(F32), gq_ref[...], tabs[1])
        qs[:, cols] = (qt * scale).astype(BF16)
    for i, (k_ref, v_ref) in enumerate(((kp_ref, vp_ref), (kc_ref, vc_ref), (kn_ref, vn_ref))):
        rows = slice(i * bq, (i + 1) * bq)
        for t in range(ATT_KV_WIDTH // LANES):
            cols = slice(t * LANES, (t + 1) * LANES)
            ks[rows, cols] = _norm_rope_tile(k_ref[:, cols].astype(F32), gk_ref[...], tabs[i]).astype(BF16)
        vs[rows, :] = v_ref[...].astype(BF16)

    r = lax.broadcasted_iota(jnp.int32, (bq, 3 * bq), 0)
    kk = lax.broadcasted_iota(jnp.int32, (bq, 3 * bq), 1)
    rel = kk - bq - r
    in_band = jnp.where(rel >= -WINDOW, jnp.where(rel <= WINDOW, 1, 0), 0)
    has_prev = jnp.where(kk >= bq, 1, jnp.where(blk > 0, 1, 0))
    has_next = jnp.where(kk < 2 * bq, 1, jnp.where(blk < n_blocks - 1, 1, 0))
    bias = jnp.where(in_band * has_prev * has_next > 0, 0.0, NEG).astype(F32)

    for c in range(ATT_HEADS):
        h = c // ATT_GROUP
        qh = qs[:, c * ATT_HEAD_DIM:(c + 1) * ATT_HEAD_DIM]
        kh = ks[:, h * ATT_HEAD_DIM:(h + 1) * ATT_HEAD_DIM]
        vh = vs[:, h * ATT_HEAD_DIM:(h + 1) * ATT_HEAD_DIM]
        s = lax.dot_general(qh, kh, (((1,), (1,)), ((), ())), preferred_element_type=F32) + bias
        sink = sink_ref[:, c:c + 1]
        m = jnp.maximum(jnp.max(s, axis=-1, keepdims=True), sink)
        p = jnp.exp(s - m)
        denom = jnp.sum(p, axis=-1, keepdims=True) + jnp.exp(sink - m)
        o = jnp.dot(p.astype(BF16), vh, preferred_element_type=F32)
        att[:, c * ATT_HEAD_DIM:(c + 1) * ATT_HEAD_DIM] = o / denom

    out_ref[...] = (att[...] * _silu(z_ref[...].astype(F32))).astype(out_ref.dtype)


def _attention(proj, rope_tab, gq, gk, sink, batch, seq):
    t = proj.shape[0]
    nb = seq // ATT_BLOCK
    bq = ATT_BLOCK
    kcol, vcol = P_AK // ATT_KV_WIDTH, P_AV // ATT_KV_WIDTH

    def cur(col):
        return lambda b, i: (b * nb + i, col)

    def prev(col):
        return lambda b, i: (b * nb + jnp.maximum(i - 1, 0), col)

    def nxt(col):
        return lambda b, i: (b * nb + jnp.minimum(i + 1, nb - 1), col)

    tab_block = (3, bq, LANES)
    const = lambda b, i: (0, 0)
    return pl.pallas_call(
        functools.partial(_attention_kernel, n_blocks=nb),
        grid=(batch, nb),
        in_specs=[
            pl.BlockSpec((bq, ATT_WIDTH), cur(P_AQ // ATT_WIDTH)),
            pl.BlockSpec((bq, ATT_WIDTH), cur(P_AZ // ATT_WIDTH)),
            pl.BlockSpec((bq, ATT_KV_WIDTH), prev(kcol)),
            pl.BlockSpec((bq, ATT_KV_WIDTH), cur(kcol)),
            pl.BlockSpec((bq, ATT_KV_WIDTH), nxt(kcol)),
            pl.BlockSpec((bq, ATT_KV_WIDTH), prev(vcol)),
            pl.BlockSpec((bq, ATT_KV_WIDTH), cur(vcol)),
            pl.BlockSpec((bq, ATT_KV_WIDTH), nxt(vcol)),
            pl.BlockSpec(tab_block, lambda b, i: (0, jnp.maximum(i - 1, 0), 0)),
            pl.BlockSpec(tab_block, lambda b, i: (0, i, 0)),
            pl.BlockSpec(tab_block, lambda b, i: (0, jnp.minimum(i + 1, nb - 1), 0)),
            pl.BlockSpec((1, LANES), const),
            pl.BlockSpec((1, LANES), const),
            pl.BlockSpec((1, ATT_HEADS), const),
        ],
        out_specs=pl.BlockSpec((bq, ATT_WIDTH), lambda b, i: (b * nb + i, 0)),
        out_shape=jax.ShapeDtypeStruct((t, ATT_WIDTH), BF16),
        scratch_shapes=[
            pltpu.VMEM((bq, ATT_WIDTH), BF16),
            pltpu.VMEM((3 * bq, ATT_KV_WIDTH), BF16),
            pltpu.VMEM((3 * bq, ATT_KV_WIDTH), BF16),
            pltpu.VMEM((bq, ATT_WIDTH), F32),
        ],
        compiler_params=pltpu.CompilerParams(
            dimension_semantics=("arbitrary", "arbitrary"), vmem_limit_bytes=VMEM_LIMIT_BYTES),
        name="attention",
    )(proj, proj, proj, proj, proj, proj, proj, proj, rope_tab, rope_tab, rope_tab, gq, gk, sink)


def _mlstm_kernel(q_ref, k_ref, v_ref, o_ref, z_ref, grow_ref, bcol_ref, cwq_ref, cwk_ref, ng_ref,
                  out_ref, qc_s, kc_s, kct_s, cc_s, nf_s, nb_s, mf_s, mb_s, *, seq):
    hd = pl.program_id(1)
    L = M_CHUNK
    nc = seq // L
    kscale = M_HEAD_DIM ** -0.5
    sub = 16

    row_id = lax.broadcasted_iota(jnp.int32, (L, LANES), 0)

    def conv_body(ci, carry):
        t0 = pl.multiple_of(ci * L, L)
        tp = pl.multiple_of(jnp.maximum(t0 - sub, 0), sub)
        tn = pl.multiple_of(jnp.minimum(t0 + L, seq - sub), sub)
        for src, w_ref, is_k in ((q_ref, cwq_ref, False), (k_ref, cwk_ref, True)):
            x = src[pl.ds(t0, L), :].astype(F32)
            before = src[pl.ds(tp, sub), :].astype(F32)[sub - 1:sub, :]
            after = src[pl.ds(tn, sub), :].astype(F32)[0:1, :]
            before = jnp.where(ci > 0, before, 0.0)
            after = jnp.where(ci < nc - 1, after, 0.0)
            xm = jnp.where(row_id == 0, before, pltpu.roll(x, 1, 0))
            xp = jnp.where(row_id == L - 1, after, pltpu.roll(x, L - 1, 0))
            y = xm * w_ref[0:1, :] + x * w_ref[1:2, :] + xp * w_ref[2:3, :]
            y = _silu(y)
            if is_k:
                y = y * kscale
                kc_s[pl.ds(t0, L), :] = y.astype(BF16)
                kct_s[:, pl.ds(t0, L)] = y.T.astype(BF16)
            else:
                qc_s[pl.ds(t0, L), :] = y.astype(BF16)
        return carry

    lax.fori_loop(0, nc, conv_body, 0)

    head_row = lax.broadcasted_iota(jnp.int32, (M_HEADS, L), 0) == hd

    def gate_row(base, t0):
        rows = grow_ref[base:base + M_HEADS, pl.ds(t0, L)]
        return jnp.sum(jnp.where(head_row, rows, 0.0), axis=0, keepdims=True)

    def scan_dir(t0, ci, state, li_base, b_base, last_lane, n_s, m_s, col0):
        c_st, n_st, m_st = state
        cc_s[ci, :, col0:col0 + M_HEAD_DIM] = c_st.astype(BF16)
        n_s[ci] = jnp.broadcast_to(n_st, (8, LANES))
        m_s[ci] = jnp.broadcast_to(m_st, (8, LANES))
        li = gate_row(li_base, t0)
        b = gate_row(b_base, t0)
        b_last = b[:, last_lane:last_lane + 1]
        g = b_last - b + li
        m_new = jnp.maximum(b_last + m_st, jnp.max(g, axis=-1, keepdims=True))
        w_c = jnp.exp(b_last + m_st - m_new)
        w_k = jnp.exp(g - m_new)
        ktw = (kct_s[:, pl.ds(t0, L)].astype(F32) * w_k).astype(BF16)
        v = v_ref[pl.ds(t0, L), :].astype(BF16)
        c_new = w_c * c_st + jnp.dot(ktw, v, preferred_element_type=F32)
        wk8 = jnp.broadcast_to(w_k, (8, L)).astype(BF16)
        n_upd = jnp.dot(wk8, kc_s[pl.ds(t0, L), :], preferred_element_type=F32)[0:1, :]
        n_new = w_c * n_st + n_upd
        return c_new, n_new, m_new

    def scan_body(j, carry):
        st_f, st_b = carry
        cf = j
        cb = nc - 1 - j
        st_f = scan_dir(pl.multiple_of(cf * L, L), cf, st_f, ROW_LI_F, ROW_B_F, L - 1, nf_s, mf_s, 0)
        st_b = scan_dir(pl.multiple_of(cb * L, L), cb, st_b, ROW_LI_B, ROW_B_B, 0, nb_s, mb_s, M_HEAD_DIM)
        return st_f, st_b

    init = (jnp.zeros((M_HEAD_DIM, M_HEAD_DIM), F32), jnp.zeros((1, M_HEAD_DIM), F32),
            jnp.full((1, 1), NEG, F32))
    lax.fori_loop(0, nc, scan_body, (init, init))

    rr = lax.broadcasted_iota(jnp.int32, (L, L), 0)
    cc = lax.broadcasted_iota(jnp.int32, (L, L), 1)
    lane = lax.broadcasted_iota(jnp.int32, (L, LANES), 1)

    def direction(qf, qk, bc, t0, ci, li_base, b_base, b_lane, keep, n_s, m_s):
        li = gate_row(li_base, t0)
        b_r = gate_row(b_base, t0)
        b_c = jnp.sum(jnp.where(lane == b_lane + hd, bc, 0.0), axis=-1, keepdims=True)
        m_prev = m_s[ci][0:1, 0:1]
        n_prev = n_s[ci][0:1, :]
        d = jnp.where(keep, b_c + (li - b_r), NEG)
        m_inter = b_c + m_prev
        m_t = jnp.maximum(m_inter, jnp.max(d, axis=-1, keepdims=True))
        w_inter = jnp.exp(m_inter - m_t)
        a = jnp.exp(d - m_t) * qk
        den = w_inter * jnp.sum(qf * n_prev, axis=-1, keepdims=True) + jnp.sum(a, axis=-1, keepdims=True)
        inv = 1.0 / jnp.maximum(jnp.abs(den), jnp.exp(-m_t))
        return a * inv, w_inter * inv

    def out_body(ci, carry):
        t0 = pl.multiple_of(ci * L, L)
        q = qc_s[pl.ds(t0, L), :]
        qf = q.astype(F32)
        qk = jnp.dot(q, kct_s[:, pl.ds(t0, L)], preferred_element_type=F32)
        qc = jnp.dot(q, cc_s[ci], preferred_element_type=F32)
        bc = bcol_ref[pl.ds(t0, L), :]
        p_f, w_f = direction(qf, qk, bc, t0, ci, ROW_LI_F, ROW_B_F, LANE_B_F, cc <= rr, nf_s, mf_s)
        p_b, w_b = direction(qf, qk, bc, t0, ci, ROW_LI_B, ROW_B_B, LANE_B_B, cc >= rr, nb_s, mb_s)
        v = v_ref[pl.ds(t0, L), :].astype(BF16)
        h = (jnp.dot((p_f + p_b).astype(BF16), v, preferred_element_type=F32)
             + w_f * qc[:, :M_HEAD_DIM] + w_b * qc[:, M_HEAD_DIM:])
        h = _sigmoid(o_ref[pl.ds(t0, L), :].astype(F32)) * h
        h = h * lax.rsqrt(jnp.mean(h * h, axis=-1, keepdims=True) + NORM_EPS) * ng_ref[...]
        out_ref[pl.ds(t0, L), :] = (h * _silu(z_ref[pl.ds(t0, L), :].astype(F32))).astype(out_ref.dtype)
        return carry

    lax.fori_loop(0, nc, out_body, 0)


def _mlstm(proj, grow, bcol, conv_w, ng, batch, seq):
    t = proj.shape[0]
    nc = seq // M_CHUNK

    def head_block(col0):
        return pl.BlockSpec((seq, M_HEAD_DIM), lambda b, h: (b, col0 // M_HEAD_DIM + h))

    return pl.pallas_call(
        functools.partial(_mlstm_kernel, seq=seq),
        grid=(batch, M_HEADS),
        in_specs=[
            head_block(P_MQ), head_block(P_MK), head_block(P_MV), head_block(P_MO), head_block(P_MZ),
            pl.BlockSpec((N_IFG, seq), lambda b, h: (0, b)),
            pl.BlockSpec((seq, LANES), lambda b, h: (b, 0)),
            pl.BlockSpec((CONV_K, M_HEAD_DIM), lambda b, h: (0, h)),
            pl.BlockSpec((CONV_K, M_HEAD_DIM), lambda b, h: (0, M_HEADS + h)),
            pl.BlockSpec((1, M_HEAD_DIM), lambda b, h: (0, h)),
        ],
        out_specs=pl.BlockSpec((seq, M_HEAD_DIM), lambda b, h: (b, h)),
        out_shape=jax.ShapeDtypeStruct((t, M_WIDTH), BF16),
        scratch_shapes=[
            pltpu.VMEM((seq, M_HEAD_DIM), BF16),
            pltpu.VMEM((seq, M_HEAD_DIM), BF16),
            pltpu.VMEM((M_HEAD_DIM, seq), BF16),
            pltpu.VMEM((nc, M_HEAD_DIM, 2 * M_HEAD_DIM), BF16),
            pltpu.VMEM((nc, 8, M_HEAD_DIM), F32),
            pltpu.VMEM((nc, 8, M_HEAD_DIM), F32),
            pltpu.VMEM((nc, 8, LANES), F32),
            pltpu.VMEM((nc, 8, LANES), F32),
        ],
        compiler_params=pltpu.CompilerParams(
            dimension_semantics=("arbitrary", "arbitrary"), vmem_limit_bytes=VMEM_LIMIT_BYTES),
        name="mlstm",
    )(proj, proj, proj, proj, proj, grow, bcol, conv_w, conv_w, ng)


def _out_proj_kernel(x_ref, a_ref, m_ref, gates_ref, wa_ref, wm_ref, wo_ref, out_ref):
    branch_a = jnp.dot(a_ref[...], wa_ref[...], preferred_element_type=F32)
    branch_m = jnp.dot(m_ref[...], wm_ref[...], preferred_element_type=F32)
    gates = _sigmoid(gates_ref[...].astype(F32))
    merged = gates[:, :D_MODEL] * branch_a + gates[:, D_MODEL:] * branch_m
    out_ref[...] = x_ref[...] + jnp.dot(merged.astype(BF16), wo_ref[...], preferred_element_type=F32)


def _out_proj(x, a, m, proj, wa, wm, wo):
    t = x.shape[0]
    tile = lambda i: (i, 0)
    const = lambda i: (0, 0)
    return pl.pallas_call(
        _out_proj_kernel,
        grid=(t // OUT_TM,),
        in_specs=[
            pl.BlockSpec((OUT_TM, D_MODEL), tile),
            pl.BlockSpec((OUT_TM, ATT_WIDTH), tile),
            pl.BlockSpec((OUT_TM, M_WIDTH), tile),
            pl.BlockSpec((OUT_TM, 2 * D_MODEL), lambda i: (i, P_GATES // (2 * D_MODEL))),
            pl.BlockSpec((ATT_WIDTH, D_MODEL), const),
            pl.BlockSpec((M_WIDTH, D_MODEL), const),
            pl.BlockSpec((D_MODEL, D_MODEL), const),
        ],
        out_specs=pl.BlockSpec((OUT_TM, D_MODEL), tile),
        out_shape=jax.ShapeDtypeStruct((t, D_MODEL), F32),
        compiler_params=pltpu.CompilerParams(
            dimension_semantics=("arbitrary",), vmem_limit_bytes=VMEM_LIMIT_BYTES),
        name="out_proj",
    )(x, a, m, proj, wa, wm, wo)


def _rope_tables(seq):
    inv = jnp.power(jnp.float32(ROPE_THETA), -jnp.arange(ROPE_HALF, dtype=F32) * 2.0 / ROPE_DIM)
    ang = jnp.arange(seq, dtype=F32)[:, None] * inv[None, :]
    cos, sin = jnp.cos(ang), jnp.sin(ang)
    pad = ATT_HEAD_DIM - ROPE_DIM
    ones = jnp.ones((seq, pad), F32)
    zeros = jnp.zeros((seq, pad), F32)
    zh = jnp.zeros((seq, ROPE_HALF), F32)
    c = jnp.concatenate([cos, cos, ones], axis=-1)
    s1 = jnp.concatenate([-sin, zh, zeros], axis=-1)
    s2 = jnp.concatenate([zh, sin, zeros], axis=-1)
    tab = jnp.stack([c, s1, s2])
    return jnp.concatenate([tab, tab], axis=-1)


def _prep_weights(w_in, b_in):
    order = [(_R_GATES, 2 * D_MODEL), (_R_AQ, ATT_WIDTH), (_R_AZ, ATT_WIDTH), (_R_AK, ATT_KV_WIDTH),
             (_R_AV, ATT_KV_WIDTH), (_R_MQ, M_WIDTH), (_R_MK, M_WIDTH), (_R_MV, M_WIDTH),
             (_R_MO, M_WIDTH), (_R_MZ, M_WIDTH)]
    w_main = jnp.concatenate([w_in[:, :, o:o + n] for o, n in order], axis=-1).astype(BF16)
    b_main = jnp.concatenate([b_in[:, o:o + n] for o, n in order], axis=-1)[:, None, :]
    w_if = w_in[:, :, _R_IF:_R_IF + N_IFG]
    b_if = b_in[:, _R_IF:_R_IF + N_IFG]
    w_if_pad = jnp.pad(w_if, ((0, 0), (0, 0), (0, LANES - N_IFG))).astype(BF16)
    b_if_pad = jnp.pad(b_if, ((0, 0), (0, LANES - N_IFG)))[:, None, :]
    w_if_t = jnp.swapaxes(w_if, 1, 2).astype(BF16)
    b_if_c = b_if[:, :, None]
    return w_main, b_main, w_if_pad, w_if_t, b_if_pad, b_if_c


def _trunk(x, rope_tab, layers):
    batch, seq, _ = x.shape
    xf = x.reshape(batch * seq, D_MODEL)
    for lw in layers:
        proj, grow, bcol = _in_proj(xf, lw["norm_g"], lw["w_main"], lw["b_main"], lw["w_if"], lw["w_if_t"],
                                    lw["b_if"], lw["b_if_c"])
        a = _attention(proj, rope_tab, lw["gq"], lw["gk"], lw["sink"], batch, seq)
        m = _mlstm(proj, grow, bcol, lw["conv_w"], lw["m_norm_g"], batch, seq)
        xf = _out_proj(xf, a, m, proj, lw["w_att_out"], lw["w_m_out"], lw["w_out"])
    return xf.reshape(batch, seq, D_MODEL)


def kernel(x_prompt, x_sample, norm_g, w_in, b_in, q_norm_g, k_norm_g, sink, conv_w, m_norm_g,
           w_att_out, w_m_out, w_out):
    w_main, b_main, w_if, w_if_t, b_if, b_if_c = _prep_weights(w_in, b_in)
    wa, wm, wo = w_att_out.astype(BF16), w_m_out.astype(BF16), w_out.astype(BF16)
    layers = []
    for l in range(DEPTH):
        layers.append(dict(
            norm_g=norm_g[l][None, :], w_main=w_main[l], b_main=b_main[l], w_if=w_if[l], w_if_t=w_if_t[l],
            b_if=b_if[l], b_if_c=b_if_c[l],
            gq=jnp.tile(q_norm_g[l], 2)[None, :], gk=jnp.tile(k_norm_g[l], 2)[None, :],
            sink=sink[l][None, :], conv_w=conv_w[l], m_norm_g=m_norm_g[l][None, :],
            w_att_out=wa[l], w_m_out=wm[l], w_out=wo[l]))
    outs = []
    for x in (x_prompt, x_sample):
        outs.append(_trunk(x, _rope_tables(x.shape[1]), layers))
    return tuple(outs)
```

```python
import functools

import jax
import jax.numpy as jnp
from jax import lax
from jax.experimental import pallas as pl
from jax.experimental.pallas import tpu as pltpu

F32 = jnp.float32
BF16 = jnp.bfloat16

D_MODEL = 1024
DEPTH = 4
ATT_HEADS = 16
ATT_KV_HEADS = 4
ATT_GROUP = ATT_HEADS // ATT_KV_HEADS
ATT_HEAD_DIM = 64
ATT_WIDTH = ATT_HEADS * ATT_HEAD_DIM
ATT_KV_WIDTH = ATT_KV_HEADS * ATT_HEAD_DIM
WINDOW = 128
ATT_BLOCK = 128
ROPE_THETA = 500000.0
ROPE_DIM = ATT_HEAD_DIM // 4
ROPE_HALF = ROPE_DIM // 2
M_HEADS = 8
M_HEAD_DIM = 128
M_WIDTH = M_HEADS * M_HEAD_DIM
M_CHUNK = 128
CONV_K = 3
NORM_EPS = 1e-6
NEG = -1e30

LANES = 128
VMEM_LIMIT_BYTES = 56 * 1024 * 1024

_REF_SPLITS = (ATT_WIDTH, ATT_KV_WIDTH, ATT_KV_WIDTH, ATT_WIDTH,
               M_WIDTH, M_WIDTH, M_WIDTH, M_WIDTH, M_WIDTH,
               M_HEADS, M_HEADS, M_HEADS, M_HEADS, 2 * D_MODEL)
_REF_OFF = [0]
for _w in _REF_SPLITS:
    _REF_OFF.append(_REF_OFF[-1] + _w)
(_R_AQ, _R_AK, _R_AV, _R_AZ, _R_MQ, _R_MK, _R_MV, _R_MO, _R_MZ,
 _R_IF, _R_FF, _R_IB, _R_FB, _R_GATES) = _REF_OFF[:-1]

P_GATES = 0
P_AQ = 2 * D_MODEL
P_AZ = P_AQ + ATT_WIDTH
P_AK = P_AZ + ATT_WIDTH
P_AV = P_AK + ATT_KV_WIDTH
P_MQ = P_AV + ATT_KV_WIDTH
P_MK = P_MQ + M_WIDTH
P_MV = P_MK + M_WIDTH
P_MO = P_MV + M_WIDTH
P_MZ = P_MO + M_WIDTH
P_WIDTH = P_MZ + M_WIDTH
N_IFG = 4 * M_HEADS

ROW_LI_F, ROW_B_F, ROW_LI_B, ROW_B_B = 0, M_HEADS, 2 * M_HEADS, 3 * M_HEADS
LANE_B_F, LANE_B_B = M_HEADS, 3 * M_HEADS

PROJ_DTYPE = BF16
IN_TM = 512
IN_TN = 512
OUT_TM = 512
ATT_TQ = 512
ATT_NSB = ATT_TQ // ATT_BLOCK


def _sigmoid(x):
    return 1.0 / (1.0 + jnp.exp(-x))


def _silu(x):
    return x * _sigmoid(x)


def _log_sigmoid(x):
    return jnp.minimum(x, 0.0) - jnp.log1p(jnp.exp(-jnp.abs(x)))


def _split_dot_left(tri, x):
    hi = x.astype(BF16)
    lo = (x - hi.astype(F32)).astype(BF16)
    return (jnp.dot(tri, hi, preferred_element_type=F32)
            + jnp.dot(tri, lo, preferred_element_type=F32))


def _split_dot_right(x, tri):
    hi = x.astype(BF16)
    lo = (x - hi.astype(F32)).astype(BF16)
    return (jnp.dot(hi, tri, preferred_element_type=F32)
            + jnp.dot(lo, tri, preferred_element_type=F32))


def _in_proj_kernel(x_ref, g_ref, w_ref, b_ref, wif_ref, wift_ref, bif_ref, bifc_ref,
                    proj_ref, grow_ref, bcol_ref):
    x = x_ref[...]
    xn = x * lax.rsqrt(jnp.mean(x * x, axis=-1, keepdims=True) + NORM_EPS) * g_ref[...]
    xn = xn.astype(BF16)
    for j in range(P_WIDTH // IN_TN):
        cols = slice(j * IN_TN, (j + 1) * IN_TN)
        acc = jnp.dot(xn, w_ref[:, cols], preferred_element_type=F32) + b_ref[:, cols]
        proj_ref[:, cols] = acc.astype(proj_ref.dtype)

    gc = jnp.dot(xn, wif_ref[...], preferred_element_type=F32) + bif_ref[...]
    gr = lax.dot_general(wift_ref[...], xn, (((1,), (1,)), ((), ())),
                         preferred_element_type=F32) + bifc_ref[...]
    ls_c = _log_sigmoid(gc)
    ls_r = _log_sigmoid(gr)

    r = lax.broadcasted_iota(jnp.int32, (M_CHUNK, M_CHUNK), 0)
    c = lax.broadcasted_iota(jnp.int32, (M_CHUNK, M_CHUNK), 1)
    tril = (c <= r).astype(BF16)
    triu = (c >= r).astype(BF16)
    lane = lax.broadcasted_iota(jnp.int32, (M_CHUNK, LANES), 1)
    fwd_lanes = lane < 2 * M_HEADS

    grow_ref[ROW_LI_F:ROW_LI_F + M_HEADS, :] = gr[0:M_HEADS, :]
    grow_ref[ROW_LI_B:ROW_LI_B + M_HEADS, :] = gr[2 * M_HEADS:3 * M_HEADS, :]
    for ci in range(x_ref.shape[0] // M_CHUNK):
        rows = slice(ci * M_CHUNK, (ci + 1) * M_CHUNK)
        lsc = ls_c[rows, :]
        b_f = _split_dot_left(tril, lsc)
        b_b = _split_dot_left(triu, lsc)
        bcol_ref[rows, :] = jnp.where(fwd_lanes, b_f, b_b)
        grow_ref[ROW_B_F:ROW_B_F + M_HEADS, rows] = _split_dot_right(ls_r[M_HEADS:2 * M_HEADS, rows], triu)
        grow_ref[ROW_B_B:ROW_B_B + M_HEADS, rows] = _split_dot_right(ls_r[3 * M_HEADS:4 * M_HEADS, rows], tril)


def _in_proj(x, g, w, b, wif, wift, bif, bifc):
    t = x.shape[0]
    const = lambda i: (0, 0)
    return pl.pallas_call(
        _in_proj_kernel,
        grid=(t // IN_TM,),
        in_specs=[
            pl.BlockSpec((IN_TM, D_MODEL), lambda i: (i, 0)),
            pl.BlockSpec((1, D_MODEL), const),
            pl.BlockSpec((D_MODEL, P_WIDTH), const, pipeline_mode=pl.Buffered(1)),
            pl.BlockSpec((1, P_WIDTH), const),
            pl.BlockSpec((D_MODEL, LANES), const),
            pl.BlockSpec((N_IFG, D_MODEL), const),
            pl.BlockSpec((1, LANES), const),
            pl.BlockSpec((N_IFG, 1), const),
        ],
        out_specs=[
            pl.BlockSpec((IN_TM, P_WIDTH), lambda i: (i, 0)),
            pl.BlockSpec((N_IFG, IN_TM), lambda i: (0, i)),
            pl.BlockSpec((IN_TM, LANES), lambda i: (i, 0)),
        ],
        out_shape=[
            jax.ShapeDtypeStruct((t, P_WIDTH), PROJ_DTYPE),
            jax.ShapeDtypeStruct((N_IFG, t), F32),
            jax.ShapeDtypeStruct((t, LANES), F32),
        ],
        compiler_params=pltpu.CompilerParams(
            dimension_semantics=("arbitrary",), vmem_limit_bytes=VMEM_LIMIT_BYTES),
        name="in_proj",
    )(x, g, w, b, wif, wift, bif, bifc)


def _norm_rope_tile(x, gain, tab):
    lane = lax.broadcasted_iota(jnp.int32, x.shape, 1)
    left = lane < ATT_HEAD_DIM
    sq = x * x
    ss_l = jnp.sum(jnp.where(left, sq, 0.0), axis=-1, keepdims=True)
    ss_r = jnp.sum(jnp.where(left, 0.0, sq), axis=-1, keepdims=True)
    inv = jnp.where(left, lax.rsqrt(ss_l / ATT_HEAD_DIM + NORM_EPS),
                    lax.rsqrt(ss_r / ATT_HEAD_DIM + NORM_EPS))
    y = x * inv * gain
    return (y * tab[0]
            + pltpu.roll(y, LANES - ROPE_HALF, 1) * tab[1]
            + pltpu.roll(y, ROPE_HALF, 1) * tab[2])


_NT = (((1,), (1,)), ((), ()))


def _attention_kernel(q_ref, z_ref, kp_ref, kc_ref, kn_ref, vp_ref, vc_ref, vn_ref,
                      tp_ref, tc_ref, tn_ref, gq_ref, gk_ref, sink_ref, out_ref,
                      qs, ks, vts, rqs, out_t, *, n_blocks):
    ti = pl.program_id(1)
    bq = ATT_BLOCK
    scale = ATT_HEAD_DIM ** -0.5

    qf = q_ref[...].astype(F32)
    sq = qf * qf
    sq_hi = sq.astype(BF16)
    sq_lo = (sq - sq_hi.astype(F32)).astype(BF16)
    head_of_col = lax.broadcasted_iota(jnp.int32, (ATT_HEADS, ATT_WIDTH), 1) >> (ATT_HEAD_DIM.bit_length() - 1)
    pick = (head_of_col == lax.broadcasted_iota(jnp.int32, (ATT_HEADS, ATT_WIDTH), 0)).astype(BF16)
    ssq = (lax.dot_general(pick, sq_hi, _NT, preferred_element_type=F32)
           + lax.dot_general(pick, sq_lo, _NT, preferred_element_type=F32))
    rqs[...] = lax.rsqrt(ssq / ATT_HEAD_DIM + NORM_EPS) * scale
    tab_q = tc_ref[...]
    for p in range(ATT_WIDTH // LANES):
        y = qf[:, p * LANES:(p + 1) * LANES] * gq_ref[...]
        y = (y * tab_q[0] + pltpu.roll(y, LANES - ROPE_HALF, 1) * tab_q[1]
             + pltpu.roll(y, ROPE_HALF, 1) * tab_q[2]).astype(BF16)
        for j in range(ATT_NSB):
            slab = (j * (ATT_WIDTH // LANES) + p) * bq
            qs[slab:slab + bq, :] = y[j * bq:(j + 1) * bq, :]

    r0 = 0
    for k_ref, v_ref, t_ref in ((kp_ref, vp_ref, tp_ref), (kc_ref, vc_ref, tc_ref), (kn_ref, vn_ref, tn_ref)):
        nr = k_ref.shape[0]
        tab = t_ref[...]
        left = lax.broadcasted_iota(jnp.int32, (nr, LANES), 1) < ATT_HEAD_DIM
        for t in range(ATT_KV_WIDTH // LANES):
            kt = _norm_rope_tile(k_ref[:, t * LANES:(t + 1) * LANES].astype(F32), gk_ref[...], tab)
            sw = pltpu.roll(kt, ATT_HEAD_DIM, 1)
            ks[2 * t, 0, r0:r0 + nr, :] = jnp.where(left, kt, 0.0).astype(BF16)
            ks[2 * t, 1, r0:r0 + nr, :] = jnp.where(left, 0.0, sw).astype(BF16)
            ks[2 * t + 1, 0, r0:r0 + nr, :] = jnp.where(left, sw, 0.0).astype(BF16)
            ks[2 * t + 1, 1, r0:r0 + nr, :] = jnp.where(left, 0.0, kt).astype(BF16)
        vts[:, r0:r0 + nr] = v_ref[...].astype(F32).T.astype(BF16)
        r0 += nr

    def group_heads(h):
        return (ATT_GROUP * h, ATT_GROUP * h + 2, ATT_GROUP * h + 1, ATT_GROUP * h + 3)

    sink_rows = [jnp.concatenate([jnp.broadcast_to(sink_ref[:, c:c + 1], (1, bq)) for c in group_heads(h)], axis=1)
                 for h in range(ATT_KV_HEADS)]

    kk = lax.broadcasted_iota(jnp.int32, (3 * bq, bq), 0)
    qq = lax.broadcasted_iota(jnp.int32, (3 * bq, bq), 1)
    rel = kk - bq - qq
    in_band = jnp.where(rel >= -WINDOW, jnp.where(rel <= WINDOW, 1, 0), 0)

    def sub_block(j, carry):
        blk = ti * ATT_NSB + j
        has_prev = jnp.where(kk >= bq, 1, jnp.where(blk > 0, 1, 0))
        has_next = jnp.where(kk < 2 * bq, 1, jnp.where(blk < n_blocks - 1, 1, 0))
        bias = jnp.where(in_band * has_prev * has_next > 0, 0.0, NEG).astype(F32)
        bias4 = jnp.concatenate([bias] * ATT_GROUP, axis=1)
        row0 = pl.multiple_of(j * bq, bq)
        for h in range(ATT_KV_HEADS):
            heads = group_heads(h)
            slab = pl.multiple_of((j * (ATT_WIDTH // LANES) + 2 * h) * bq, bq)
            q_pairs = qs[pl.ds(slab, 2 * bq), :]
            s = jnp.concatenate(
                [lax.dot_general(ks[h, v, pl.ds(row0, 3 * bq), :], q_pairs, _NT, preferred_element_type=F32)
                 for v in range(2)], axis=1)
            rq_row = jnp.concatenate([rqs[c:c + 1, pl.ds(row0, bq)] for c in heads], axis=1)
            s = s * rq_row + bias4
            m = jnp.maximum(jnp.max(s, axis=0, keepdims=True), sink_rows[h])
            p = jnp.exp(s - m)
            denom = jnp.sum(p, axis=0, keepdims=True) + jnp.exp(sink_rows[h] - m)
            o = jnp.dot(vts[h * ATT_HEAD_DIM:(h + 1) * ATT_HEAD_DIM, pl.ds(row0, 3 * bq)], p.astype(BF16),
                        preferred_element_type=F32) * (1.0 / denom)
            for i, c in enumerate(heads):
                out_t[c * ATT_HEAD_DIM:(c + 1) * ATT_HEAD_DIM, :] = o[:, i * bq:(i + 1) * bq]
        att = out_t[...].T
        z = z_ref[pl.ds(row0, bq), :].astype(F32)
        out_ref[pl.ds(row0, bq), :] = (att * _silu(z)).astype(out_ref.dtype)
        return carry

    lax.fori_loop(0, ATT_NSB, sub_block, 0)


def _attention(proj, rope_tab, gq, gk, sink, batch, seq):
    t = proj.shape[0]
    nb = seq // ATT_BLOCK
    nt = seq // ATT_TQ
    bq = ATT_BLOCK
    kcol, vcol = P_AK // ATT_KV_WIDTH, P_AV // ATT_KV_WIDTH

    def cur(col):
        return lambda b, i: (b * nt + i, col)

    def prev(col):
        return lambda b, i: (b * nb + jnp.maximum(i * ATT_NSB - 1, 0), col)

    def nxt(col):
        return lambda b, i: (b * nb + jnp.minimum((i + 1) * ATT_NSB, nb - 1), col)

    const = lambda b, i: (0, 0)
    return pl.pallas_call(
        functools.partial(_attention_kernel, n_blocks=nb),
        grid=(batch, nt),
        in_specs=[
            pl.BlockSpec((ATT_TQ, ATT_WIDTH), cur(P_AQ // ATT_WIDTH)),
            pl.BlockSpec((ATT_TQ, ATT_WIDTH), cur(P_AZ // ATT_WIDTH)),
            pl.BlockSpec((bq, ATT_KV_WIDTH), prev(kcol)),
            pl.BlockSpec((ATT_TQ, ATT_KV_WIDTH), cur(kcol)),
            pl.BlockSpec((bq, ATT_KV_WIDTH), nxt(kcol)),
            pl.BlockSpec((bq, ATT_KV_WIDTH), prev(vcol)),
            pl.BlockSpec((ATT_TQ, ATT_KV_WIDTH), cur(vcol)),
            pl.BlockSpec((bq, ATT_KV_WIDTH), nxt(vcol)),
            pl.BlockSpec((3, bq, LANES), lambda b, i: (0, jnp.maximum(i * ATT_NSB - 1, 0), 0)),
            pl.BlockSpec((3, ATT_TQ, LANES), lambda b, i: (0, i, 0)),
            pl.BlockSpec((3, bq, LANES), lambda b, i: (0, jnp.minimum((i + 1) * ATT_NSB, nb - 1), 0)),
            pl.BlockSpec((1, LANES), const),
            pl.BlockSpec((1, LANES), const),
            pl.BlockSpec((1, ATT_HEADS), const),
        ],
        out_specs=pl.BlockSpec((ATT_TQ, ATT_WIDTH), lambda b, i: (b * nt + i, 0)),
        out_shape=jax.ShapeDtypeStruct((t, ATT_WIDTH), BF16),
        scratch_shapes=[
            pltpu.VMEM((ATT_NSB * (ATT_WIDTH // LANES) * bq, LANES), BF16),
            pltpu.VMEM((ATT_KV_HEADS, 2, ATT_TQ + 2 * bq, LANES), BF16),
            pltpu.VMEM((ATT_KV_WIDTH, ATT_TQ + 2 * bq), BF16),
            pltpu.VMEM((ATT_HEADS, ATT_TQ), F32),
            pltpu.VMEM((ATT_WIDTH, bq), F32),
        ],
        compiler_params=pltpu.CompilerParams(
            dimension_semantics=("arbitrary", "arbitrary"), vmem_limit_bytes=VMEM_LIMIT_BYTES),
        name="attention",
    )(proj, proj, proj, proj, proj, proj, proj, proj, rope_tab, rope_tab, rope_tab, gq, gk, sink)


def _mlstm_kernel(q_ref, k_ref, v_ref, o_ref, z_ref, grow_ref, bcol_ref, cwq_ref, cwk_ref, ng_ref,
                  out_ref, qc_s, kc_s, kct_s, cc_s, nf_s, nb_s, mf_s, mb_s, *, seq):
    hd = pl.program_id(1)
    L = M_CHUNK
    nc = seq // L
    kscale = M_HEAD_DIM ** -0.5
    sub = 16

    row_id = lax.broadcasted_iota(jnp.int32, (L, LANES), 0)

    def conv_body(ci, carry):
        t0 = pl.multiple_of(ci * L, L)
        tp = pl.multiple_of(jnp.maximum(t0 - sub, 0), sub)
        tn = pl.multiple_of(jnp.minimum(t0 + L, seq - sub), sub)
        for src, w_ref, is_k in ((q_ref, cwq_ref, False), (k_ref, cwk_ref, True)):
            x = src[pl.ds(t0, L), :].astype(F32)
            before = src[pl.ds(tp, sub), :].astype(F32)[sub - 1:sub, :]
            after = src[pl.ds(tn, sub), :].astype(F32)[0:1, :]
            before = jnp.where(ci > 0, before, 0.0)
            after = jnp.where(ci < nc - 1, after, 0.0)
            xm = jnp.where(row_id == 0, before, pltpu.roll(x, 1, 0))
            xp = jnp.where(row_id == L - 1, after, pltpu.roll(x, L - 1, 0))
            y = xm * w_ref[0:1, :] + x * w_ref[1:2, :] + xp * w_ref[2:3, :]
            y = _silu(y)
            if is_k:
                y = y * kscale
                kc_s[pl.ds(t0, L), :] = y.astype(BF16)
                kct_s[:, pl.ds(t0, L)] = y.T.astype(BF16)
            else:
                qc_s[pl.ds(t0, L), :] = y.astype(BF16)
        return carry

    lax.fori_loop(0, nc, conv_body, 0)

    head_row = lax.broadcasted_iota(jnp.int32, (M_HEADS, L), 0) == hd

    def gate_row(base, t0):
        rows = grow_ref[base:base + M_HEADS, pl.ds(t0, L)]
        return jnp.sum(jnp.where(head_row, rows, 0.0), axis=0, keepdims=True)

    def scan_dir(t0, ci, state, li_base, b_base, last_lane, n_s, m_s, col0):
        c_st, n_st, m_st = state
        cc_s[ci, :, col0:col0 + M_HEAD_DIM] = c_st.astype(BF16)
        n_s[ci] = jnp.broadcast_to(n_st, (8, LANES))
        m_s[ci] = jnp.broadcast_to(m_st, (8, LANES))
        li = gate_row(li_base, t0)
        b = gate_row(b_base, t0)
        b_last = b[:, last_lane:last_lane + 1]
        g = b_last - b + li
        m_new = jnp.maximum(b_last + m_st, jnp.max(g, axis=-1, keepdims=True))
        w_c = jnp.exp(b_last + m_st - m_new)
        w_k = jnp.exp(g - m_new)
        ktw = (kct_s[:, pl.ds(t0, L)].astype(F32) * w_k).astype(BF16)
        v = v_ref[pl.ds(t0, L), :].astype(BF16)
        c_new = w_c * c_st + jnp.dot(ktw, v, preferred_element_type=F32)
        wk8 = jnp.broadcast_to(w_k, (8, L)).astype(BF16)
        n_upd = jnp.dot(wk8, kc_s[pl.ds(t0, L), :], preferred_element_type=F32)[0:1, :]
        n_new = w_c * n_st + n_upd
        return c_new, n_new, m_new

    def scan_body(j, carry):
        st_f, st_b = carry
        cf = j
        cb = nc - 1 - j
        st_f = scan_dir(pl.multiple_of(cf * L, L), cf, st_f, ROW_LI_F, ROW_B_F, L - 1, nf_s, mf_s, 0)
        st_b = scan_dir(pl.multiple_of(cb * L, L), cb, st_b, ROW_LI_B, ROW_B_B, 0, nb_s, mb_s, M_HEAD_DIM)
        return st_f, st_b

    init = (jnp.zeros((M_HEAD_DIM, M_HEAD_DIM), F32), jnp.zeros((1, M_HEAD_DIM), F32),
            jnp.full((1, 1), NEG, F32))
    lax.fori_loop(0, nc, scan_body, (init, init))

    rr = lax.broadcasted_iota(jnp.int32, (L, L), 0)
    cc = lax.broadcasted_iota(jnp.int32, (L, L), 1)
    lane = lax.broadcasted_iota(jnp.int32, (L, LANES), 1)

    def direction(qf, qk, bc, t0, ci, li_base, b_base, b_lane, keep, n_s, m_s):
        li = gate_row(li_base, t0)
        b_r = gate_row(b_base, t0)
        b_c = jnp.sum(jnp.where(lane == b_lane + hd, bc, 0.0), axis=-1, keepdims=True)
        m_prev = m_s[ci][0:1, 0:1]
        n_prev = n_s[ci][0:1, :]
        d = jnp.where(keep, b_c + (li - b_r), NEG)
        m_inter = b_c + m_prev
        m_t = jnp.maximum(m_inter, jnp.max(d, axis=-1, keepdims=True))
        w_inter = jnp.exp(m_inter - m_t)
        a = jnp.exp(d - m_t) * qk
        den = w_inter * jnp.sum(qf * n_prev, axis=-1, keepdims=True) + jnp.sum(a, axis=-1, keepdims=True)
        inv = 1.0 / jnp.maximum(jnp.abs(den), jnp.exp(-m_t))
        return a * inv, w_inter * inv

    def out_body(ci, carry):
        t0 = pl.multiple_of(ci * L, L)
        q = qc_s[pl.ds(t0, L), :]
        qf = q.astype(F32)
        qk = jnp.dot(q, kct_s[:, pl.ds(t0, L)], preferred_element_type=F32)
        qc = jnp.dot(q, cc_s[ci], preferred_element_type=F32)
        bc = bcol_ref[pl.ds(t0, L), :]
        p_f, w_f = direction(qf, qk, bc, t0, ci, ROW_LI_F, ROW_B_F, LANE_B_F, cc <= rr, nf_s, mf_s)
        p_b, w_b = direction(qf, qk, bc, t0, ci, ROW_LI_B, ROW_B_B, LANE_B_B, cc >= rr, nb_s, mb_s)
        v = v_ref[pl.ds(t0, L), :].astype(BF16)
        h = (jnp.dot((p_f + p_b).astype(BF16), v, preferred_element_type=F32)
             + w_f * qc[:, :M_HEAD_DIM] + w_b * qc[:, M_HEAD_DIM:])
        h = _sigmoid(o_ref[pl.ds(t0, L), :].astype(F32)) * h
        h = h * lax.rsqrt(jnp.mean(h * h, axis=-1, keepdims=True) + NORM_EPS) * ng_ref[...]
        out_ref[pl.ds(t0, L), :] = (h * _silu(z_ref[pl.ds(t0, L), :].astype(F32))).astype(out_ref.dtype)
        return carry

    lax.fori_loop(0, nc, out_body, 0)


def _mlstm(proj, grow, bcol, conv_w, ng, batch, seq):
    t = proj.shape[0]
    nc = seq // M_CHUNK

    def head_block(col0):
        return pl.BlockSpec((seq, M_HEAD_DIM), lambda b, h: (b, col0 // M_HEAD_DIM + h))

    return pl.pallas_call(
        functools.partial(_mlstm_kernel, seq=seq),
        grid=(batch, M_HEADS),
        in_specs=[
            head_block(P_MQ), head_block(P_MK), head_block(P_MV), head_block(P_MO), head_block(P_MZ),
            pl.BlockSpec((N_IFG, seq), lambda b, h: (0, b)),
            pl.BlockSpec((seq, LANES), lambda b, h: (b, 0)),
            pl.BlockSpec((CONV_K, M_HEAD_DIM), lambda b, h: (0, h)),
            pl.BlockSpec((CONV_K, M_HEAD_DIM), lambda b, h: (0, M_HEADS + h)),
            pl.BlockSpec((1, M_HEAD_DIM), lambda b, h: (0, h)),
        ],
        out_specs=pl.BlockSpec((seq, M_HEAD_DIM), lambda b, h: (b, h)),
        out_shape=jax.ShapeDtypeStruct((t, M_WIDTH), BF16),
        scratch_shapes=[
            pltpu.VMEM((seq, M_HEAD_DIM), BF16),
            pltpu.VMEM((seq, M_HEAD_DIM), BF16),
            pltpu.VMEM((M_HEAD_DIM, seq), BF16),
            pltpu.VMEM((nc, M_HEAD_DIM, 2 * M_HEAD_DIM), BF16),
            pltpu.VMEM((nc, 8, M_HEAD_DIM), F32),
            pltpu.VMEM((nc, 8, M_HEAD_DIM), F32),
            pltpu.VMEM((nc, 8, LANES), F32),
            pltpu.VMEM((nc, 8, LANES), F32),
        ],
        compiler_params=pltpu.CompilerParams(
            dimension_semantics=("arbitrary", "arbitrary"), vmem_limit_bytes=VMEM_LIMIT_BYTES),
        name="mlstm",
    )(proj, proj, proj, proj, proj, grow, bcol, conv_w, conv_w, ng)


def _out_proj_kernel(x_ref, a_ref, m_ref, gates_ref, wa_ref, wm_ref, wo_ref, out_ref):
    branch_a = jnp.dot(a_ref[...], wa_ref[...], preferred_element_type=F32)
    branch_m = jnp.dot(m_ref[...], wm_ref[...], preferred_element_type=F32)
    gates = _sigmoid(gates_ref[...].astype(F32))
    merged = gates[:, :D_MODEL] * branch_a + gates[:, D_MODEL:] * branch_m
    out_ref[...] = x_ref[...] + jnp.dot(merged.astype(BF16), wo_ref[...], preferred_element_type=F32)


def _out_proj(x, a, m, proj, wa, wm, wo):
    t = x.shape[0]
    tile = lambda i: (i, 0)
    const = lambda i: (0, 0)
    return pl.pallas_call(
        _out_proj_kernel,
        grid=(t // OUT_TM,),
        in_specs=[
            pl.BlockSpec((OUT_TM, D_MODEL), tile),
            pl.BlockSpec((OUT_TM, ATT_WIDTH), tile),
            pl.BlockSpec((OUT_TM, M_WIDTH), tile),
            pl.BlockSpec((OUT_TM, 2 * D_MODEL), lambda i: (i, P_GATES // (2 * D_MODEL))),
            pl.BlockSpec((ATT_WIDTH, D_MODEL), const),
            pl.BlockSpec((M_WIDTH, D_MODEL), const),
            pl.BlockSpec((D_MODEL, D_MODEL), const),
        ],
        out_specs=pl.BlockSpec((OUT_TM, D_MODEL), tile),
        out_shape=jax.ShapeDtypeStruct((t, D_MODEL), F32),
        compiler_params=pltpu.CompilerParams(
            dimension_semantics=("arbitrary",), vmem_limit_bytes=VMEM_LIMIT_BYTES),
        name="out_proj",
    )(x, a, m, proj, wa, wm, wo)


def _rope_tables(seq):
    inv = jnp.power(jnp.float32(ROPE_THETA), -jnp.arange(ROPE_HALF, dtype=F32) * 2.0 / ROPE_DIM)
    ang = jnp.arange(seq, dtype=F32)[:, None] * inv[None, :]
    cos, sin = jnp.cos(ang), jnp.sin(ang)
    pad = ATT_HEAD_DIM - ROPE_DIM
    ones = jnp.ones((seq, pad), F32)
    zeros = jnp.zeros((seq, pad), F32)
    zh = jnp.zeros((seq, ROPE_HALF), F32)
    c = jnp.concatenate([cos, cos, ones], axis=-1)
    s1 = jnp.concatenate([-sin, zh, zeros], axis=-1)
    s2 = jnp.concatenate([zh, sin, zeros], axis=-1)
    tab = jnp.stack([c, s1, s2])
    return jnp.concatenate([tab, tab], axis=-1)


def _prep_weights(w_in, b_in):
    order = [(_R_GATES, 2 * D_MODEL), (_R_AQ, ATT_WIDTH), (_R_AZ, ATT_WIDTH), (_R_AK, ATT_KV_WIDTH),
             (_R_AV, ATT_KV_WIDTH), (_R_MQ, M_WIDTH), (_R_MK, M_WIDTH), (_R_MV, M_WIDTH),
             (_R_MO, M_WIDTH), (_R_MZ, M_WIDTH)]
    w_main = jnp.concatenate([w_in[:, :, o:o + n] for o, n in order], axis=-1).astype(BF16)
    b_main = jnp.concatenate([b_in[:, o:o + n] for o, n in order], axis=-1)[:, None, :]
    w_if = w_in[:, :, _R_IF:_R_IF + N_IFG]
    b_if = b_in[:, _R_IF:_R_IF + N_IFG]
    w_if_pad = jnp.pad(w_if, ((0, 0), (0, 0), (0, LANES - N_IFG))).astype(BF16)
    b_if_pad = jnp.pad(b_if, ((0, 0), (0, LANES - N_IFG)))[:, None, :]
    w_if_t = jnp.swapaxes(w_if, 1, 2).astype(BF16)
    b_if_c = b_if[:, :, None]
    return w_main, b_main, w_if_pad, w_if_t, b_if_pad, b_if_c


def _trunk(x, rope_tab, layers):
    batch, seq, _ = x.shape
    xf = x.reshape(batch * seq, D_MODEL)
    for lw in layers:
        proj, grow, bcol = _in_proj(xf, lw["norm_g"], lw["w_main"], lw["b_main"], lw["w_if"], lw["w_if_t"],
                                    lw["b_if"], lw["b_if_c"])
        a = _attention(proj, rope_tab, lw["gq"], lw["gk"], lw["sink"], batch, seq)
        m = _mlstm(proj, grow, bcol, lw["conv_w"], lw["m_norm_g"], batch, seq)
        xf = _out_proj(xf, a, m, proj, lw["w_att_out"], lw["w_m_out"], lw["w_out"])
    return xf.reshape(batch, seq, D_MODEL)


def kernel(x_prompt, x_sample, norm_g, w_in, b_in, q_norm_g, k_norm_g, sink, conv_w, m_norm_g,
           w_att_out, w_m_out, w_out):
    w_main, b_main, w_if, w_if_t, b_if, b_if_c = _prep_weights(w_in, b_in)
    wa, wm, wo = w_att_out.astype(BF16), w_m_out.astype(BF16), w_out.astype(BF16)
    layers = []
    for l in range(DEPTH):
        layers.append(dict(
            norm_g=norm_g[l][None, :], w_main=w_main[l], b_main=b_main[l], w_if=w_if[l], w_if_t=w_if_t[l],
            b_if=b_if[l], b_if_c=b_if_c[l],
            gq=jnp.tile(q_norm_g[l], 2)[None, :], gk=jnp.tile(k_norm_g[l], 2)[None, :],
            sink=sink[l][None, :], conv_w=conv_w[l], m_norm_g=m_norm_g[l][None, :],
            w_att_out=wa[l], w_m_out=wm[l], w_out=wo[l]))
    outs = []
    for x in (x_prompt, x_sample):
        outs.append(_trunk(x, _rope_tables(x.shape[1]), layers))
    return tuple(outs)
```

```python
import functools

import jax
import jax.numpy as jnp
from jax import lax
from jax.experimental import pallas as pl
from jax.experimental.pallas import tpu as pltpu

F32 = jnp.float32
BF16 = jnp.bfloat16

D_MODEL = 1024
DEPTH = 4
ATT_HEADS = 16
ATT_KV_HEADS = 4
ATT_GROUP = ATT_HEADS // ATT_KV_HEADS
ATT_HEAD_DIM = 64
ATT_WIDTH = ATT_HEADS * ATT_HEAD_DIM
ATT_KV_WIDTH = ATT_KV_HEADS * ATT_HEAD_DIM
WINDOW = 128
ATT_BLOCK = 128
ROPE_THETA = 500000.0
ROPE_DIM = ATT_HEAD_DIM // 4
ROPE_HALF = ROPE_DIM // 2
M_HEADS = 8
M_HEAD_DIM = 128
M_WIDTH = M_HEADS * M_HEAD_DIM
M_CHUNK = 128
CONV_K = 3
NORM_EPS = 1e-6
NEG = -1e30

LANES = 128
VMEM_LIMIT_BYTES = 56 * 1024 * 1024

_REF_SPLITS = (ATT_WIDTH, ATT_KV_WIDTH, ATT_KV_WIDTH, ATT_WIDTH,
               M_WIDTH, M_WIDTH, M_WIDTH, M_WIDTH, M_WIDTH,
               M_HEADS, M_HEADS, M_HEADS, M_HEADS, 2 * D_MODEL)
_REF_OFF = [0]
for _w in _REF_SPLITS:
    _REF_OFF.append(_REF_OFF[-1] + _w)
(_R_AQ, _R_AK, _R_AV, _R_AZ, _R_MQ, _R_MK, _R_MV, _R_MO, _R_MZ,
 _R_IF, _R_FF, _R_IB, _R_FB, _R_GATES) = _REF_OFF[:-1]

P_GATES = 0
P_AQ = 2 * D_MODEL
P_AZ = P_AQ + ATT_WIDTH
P_AK = P_AZ + ATT_WIDTH
P_AV = P_AK + ATT_KV_WIDTH
P_MQ = P_AV + ATT_KV_WIDTH
P_MK = P_MQ + M_WIDTH
P_MV = P_MK + M_WIDTH
P_MO = P_MV + M_WIDTH
P_MZ = P_MO + M_WIDTH
P_WIDTH = P_MZ + M_WIDTH
N_IFG = 4 * M_HEADS

ROW_LI_F, ROW_B_F, ROW_LI_B, ROW_B_B = 0, M_HEADS, 2 * M_HEADS, 3 * M_HEADS
LANE_C_F, LANE_C_B = 0, 2 * M_HEADS

PROJ_DTYPE = BF16
IN_TM = 512
IN_TN = 512
OUT_TM = 512
ATT_TQ = 512
ATT_NSB = ATT_TQ // ATT_BLOCK
MLSTM_CONV_UNROLL = 2
MLSTM_SCAN_UNROLL = 4
MLSTM_OUT_GROUP = 4


def _sigmoid(x):
    return 1.0 / (1.0 + jnp.exp(-x))


def _silu(x):
    return x * _sigmoid(x)


def _log_sigmoid(x):
    return jnp.minimum(x, 0.0) - jnp.log1p(jnp.exp(-jnp.abs(x)))


def _bf16_terms(x):
    hi = x.astype(BF16)
    r1 = x - hi.astype(F32)
    mid = r1.astype(BF16)
    lo = (r1 - mid.astype(F32)).astype(BF16)
    return hi, mid, lo


def _split_dot_left(sel, x):
    return sum(jnp.dot(sel, t, preferred_element_type=F32) for t in _bf16_terms(x))


def _split_dot_right(x, sel):
    return sum(jnp.dot(t, sel, preferred_element_type=F32) for t in _bf16_terms(x))


def _in_proj_kernel(x_ref, g_ref, w_ref, b_ref, wif_ref, wift_ref, bif_ref, bifc_ref,
                    proj_ref, grow_ref, ccol_ref):
    x = x_ref[...]
    xn = x * lax.rsqrt(jnp.mean(x * x, axis=-1, keepdims=True) + NORM_EPS) * g_ref[...]
    xn = xn.astype(BF16)
    for j in range(P_WIDTH // IN_TN):
        cols = slice(j * IN_TN, (j + 1) * IN_TN)
        acc = jnp.dot(xn, w_ref[:, cols], preferred_element_type=F32) + b_ref[:, cols]
        proj_ref[:, cols] = acc.astype(proj_ref.dtype)

    gc = jnp.dot(xn, wif_ref[...], preferred_element_type=F32) + bif_ref[...]
    gr = lax.dot_general(wift_ref[...], xn, (((1,), (1,)), ((), ())),
                         preferred_element_type=F32) + bifc_ref[...]
    ls_c = _log_sigmoid(gc)
    ls_r = _log_sigmoid(gr)

    r = lax.broadcasted_iota(jnp.int32, (M_CHUNK, M_CHUNK), 0)
    c = lax.broadcasted_iota(jnp.int32, (M_CHUNK, M_CHUNK), 1)
    tril = (c <= r).astype(BF16)
    triu = (c >= r).astype(BF16)
    lane = lax.broadcasted_iota(jnp.int32, (M_CHUNK, LANES), 1)
    fwd_lanes = lane < 2 * M_HEADS

    grow_ref[ROW_LI_F:ROW_LI_F + M_HEADS, :] = gr[0:M_HEADS, :]
    grow_ref[ROW_LI_B:ROW_LI_B + M_HEADS, :] = gr[2 * M_HEADS:3 * M_HEADS, :]
    for ci in range(x_ref.shape[0] // M_CHUNK):
        rows = slice(ci * M_CHUNK, (ci + 1) * M_CHUNK)
        lsc = ls_c[rows, :]
        b_f = _split_dot_left(tril, lsc)
        b_b = _split_dot_left(triu, lsc)
        ccol_ref[rows, :] = gc[rows, :] - pltpu.roll(jnp.where(fwd_lanes, b_f, b_b), LANES - M_HEADS, 1)
        grow_ref[ROW_B_F:ROW_B_F + M_HEADS, rows] = _split_dot_right(ls_r[M_HEADS:2 * M_HEADS, rows], triu)
        grow_ref[ROW_B_B:ROW_B_B + M_HEADS, rows] = _split_dot_right(ls_r[3 * M_HEADS:4 * M_HEADS, rows], tril)


def _in_proj(x, g, w, b, wif, wift, bif, bifc):
    t = x.shape[0]
    const = lambda i: (0, 0)
    return pl.pallas_call(
        _in_proj_kernel,
        grid=(t // IN_TM,),
        in_specs=[
            pl.BlockSpec((IN_TM, D_MODEL), lambda i: (i, 0)),
            pl.BlockSpec((1, D_MODEL), const),
            pl.BlockSpec((D_MODEL, P_WIDTH), const, pipeline_mode=pl.Buffered(1)),
            pl.BlockSpec((1, P_WIDTH), const),
            pl.BlockSpec((D_MODEL, LANES), const),
            pl.BlockSpec((N_IFG, D_MODEL), const),
            pl.BlockSpec((1, LANES), const),
            pl.BlockSpec((N_IFG, 1), const),
        ],
        out_specs=[
            pl.BlockSpec((IN_TM, P_WIDTH), lambda i: (i, 0)),
            pl.BlockSpec((N_IFG, IN_TM), lambda i: (0, i)),
            pl.BlockSpec((IN_TM, LANES), lambda i: (i, 0)),
        ],
        out_shape=[
            jax.ShapeDtypeStruct((t, P_WIDTH), PROJ_DTYPE),
            jax.ShapeDtypeStruct((N_IFG, t), F32),
            jax.ShapeDtypeStruct((t, LANES), F32),
        ],
        compiler_params=pltpu.CompilerParams(
            dimension_semantics=("arbitrary",), vmem_limit_bytes=VMEM_LIMIT_BYTES),
        name="in_proj",
    )(x, g, w, b, wif, wift, bif, bifc)


def _norm_rope_tile(x, gain, tab):
    lane = lax.broadcasted_iota(jnp.int32, x.shape, 1)
    left = lane < ATT_HEAD_DIM
    sq = x * x
    ss_l = jnp.sum(jnp.where(left, sq, 0.0), axis=-1, keepdims=True)
    ss_r = jnp.sum(jnp.where(left, 0.0, sq), axis=-1, keepdims=True)
    inv = jnp.where(left, lax.rsqrt(ss_l / ATT_HEAD_DIM + NORM_EPS),
                    lax.rsqrt(ss_r / ATT_HEAD_DIM + NORM_EPS))
    y = x * inv * gain
    return (y * tab[0]
            + pltpu.roll(y, LANES - ROPE_HALF, 1) * tab[1]
            + pltpu.roll(y, ROPE_HALF, 1) * tab[2])


_NT = (((1,), (1,)), ((), ()))


def _attention_kernel(q_ref, z_ref, kp_ref, kc_ref, kn_ref, vp_ref, vc_ref, vn_ref,
                      tp_ref, tc_ref, tn_ref, gq_ref, gk_ref, sink_ref, out_ref,
                      qs, ks, vts, rqs, out_t, *, n_blocks):
    ti = pl.program_id(1)
    bq = ATT_BLOCK
    scale = ATT_HEAD_DIM ** -0.5

    qf = q_ref[...].astype(F32)
    sq = qf * qf
    sq_hi = sq.astype(BF16)
    sq_lo = (sq - sq_hi.astype(F32)).astype(BF16)
    head_of_col = lax.broadcasted_iota(jnp.int32, (ATT_HEADS, ATT_WIDTH), 1) >> (ATT_HEAD_DIM.bit_length() - 1)
    pick = (head_of_col == lax.broadcasted_iota(jnp.int32, (ATT_HEADS, ATT_WIDTH), 0)).astype(BF16)
    ssq = (lax.dot_general(pick, sq_hi, _NT, preferred_element_type=F32)
           + lax.dot_general(pick, sq_lo, _NT, preferred_element_type=F32))
    rqs[...] = lax.rsqrt(ssq / ATT_HEAD_DIM + NORM_EPS) * scale
    tab_q = tc_ref[...]
    for p in range(ATT_WIDTH // LANES):
        y = qf[:, p * LANES:(p + 1) * LANES] * gq_ref[...]
        y = (y * tab_q[0] + pltpu.roll(y, LANES - ROPE_HALF, 1) * tab_q[1]
             + pltpu.roll(y, ROPE_HALF, 1) * tab_q[2]).astype(BF16)
        for j in range(ATT_NSB):
            slab = (j * (ATT_WIDTH // LANES) + p) * bq
            qs[slab:slab + bq, :] = y[j * bq:(j + 1) * bq, :]

    r0 = 0
    for k_ref, v_ref, t_ref in ((kp_ref, vp_ref, tp_ref), (kc_ref, vc_ref, tc_ref), (kn_ref, vn_ref, tn_ref)):
        nr = k_ref.shape[0]
        tab = t_ref[...]
        left = lax.broadcasted_iota(jnp.int32, (nr, LANES), 1) < ATT_HEAD_DIM
        for t in range(ATT_KV_WIDTH // LANES):
            kt = _norm_rope_tile(k_ref[:, t * LANES:(t + 1) * LANES].astype(F32), gk_ref[...], tab)
            sw = pltpu.roll(kt, ATT_HEAD_DIM, 1)
            ks[2 * t, 0, r0:r0 + nr, :] = jnp.where(left, kt, 0.0).astype(BF16)
            ks[2 * t, 1, r0:r0 + nr, :] = jnp.where(left, 0.0, sw).astype(BF16)
            ks[2 * t + 1, 0, r0:r0 + nr, :] = jnp.where(left, sw, 0.0).astype(BF16)
            ks[2 * t + 1, 1, r0:r0 + nr, :] = jnp.where(left, 0.0, kt).astype(BF16)
        vts[:, r0:r0 + nr] = v_ref[...].astype(F32).T.astype(BF16)
        r0 += nr

    def group_heads(h):
        return (ATT_GROUP * h, ATT_GROUP * h + 2, ATT_GROUP * h + 1, ATT_GROUP * h + 3)

    sink_rows = [jnp.concatenate([jnp.broadcast_to(sink_ref[:, c:c + 1], (1, bq)) for c in group_heads(h)], axis=1)
                 for h in range(ATT_KV_HEADS)]

    kk = lax.broadcasted_iota(jnp.int32, (3 * bq, bq), 0)
    qq = lax.broadcasted_iota(jnp.int32, (3 * bq, bq), 1)
    rel = kk - bq - qq
    in_band = jnp.where(rel >= -WINDOW, jnp.where(rel <= WINDOW, 1, 0), 0)

    def sub_block(j, carry):
        blk = ti * ATT_NSB + j
        has_prev = jnp.where(kk >= bq, 1, jnp.where(blk > 0, 1, 0))
        has_next = jnp.where(kk < 2 * bq, 1, jnp.where(blk < n_blocks - 1, 1, 0))
        bias = jnp.where(in_band * has_prev * has_next > 0, 0.0, NEG).astype(F32)
        bias4 = jnp.concatenate([bias] * ATT_GROUP, axis=1)
        row0 = pl.multiple_of(j * bq, bq)
        for h in range(ATT_KV_HEADS):
            heads = group_heads(h)
            slab = pl.multiple_of((j * (ATT_WIDTH // LANES) + 2 * h) * bq, bq)
            q_pairs = qs[pl.ds(slab, 2 * bq), :]
            s = jnp.concatenate(
                [lax.dot_general(ks[h, v, pl.ds(row0, 3 * bq), :], q_pairs, _NT, preferred_element_type=F32)
                 for v in range(2)], axis=1)
            rq_row = jnp.concatenate([rqs[c:c + 1, pl.ds(row0, bq)] for c in heads], axis=1)
            s = s * rq_row + bias4
            m = jnp.maximum(jnp.max(s, axis=0, keepdims=True), sink_rows[h])
            p = jnp.exp(s - m)
            denom = jnp.sum(p, axis=0, keepdims=True) + jnp.exp(sink_rows[h] - m)
            o = jnp.dot(vts[h * ATT_HEAD_DIM:(h + 1) * ATT_HEAD_DIM, pl.ds(row0, 3 * bq)], p.astype(BF16),
                        preferred_element_type=F32) * (1.0 / denom)
            for i, c in enumerate(heads):
                out_t[c * ATT_HEAD_DIM:(c + 1) * ATT_HEAD_DIM, :] = o[:, i * bq:(i + 1) * bq]
        att = out_t[...].T
        z = z_ref[pl.ds(row0, bq), :].astype(F32)
        out_ref[pl.ds(row0, bq), :] = (att * _silu(z)).astype(out_ref.dtype)
        return carry

    lax.fori_loop(0, ATT_NSB, sub_block, 0)


def _attention(proj, rope_tab, gq, gk, sink, batch, seq):
    t = proj.shape[0]
    nb = seq // ATT_BLOCK
    nt = seq // ATT_TQ
    bq = ATT_BLOCK
    kcol, vcol = P_AK // ATT_KV_WIDTH, P_AV // ATT_KV_WIDTH

    def cur(col):
        return lambda b, i: (b * nt + i, col)

    def prev(col):
        return lambda b, i: (b * nb + jnp.maximum(i * ATT_NSB - 1, 0), col)

    def nxt(col):
        return lambda b, i: (b * nb + jnp.minimum((i + 1) * ATT_NSB, nb - 1), col)

    const = lambda b, i: (0, 0)
    return pl.pallas_call(
        functools.partial(_attention_kernel, n_blocks=nb),
        grid=(batch, nt),
        in_specs=[
            pl.BlockSpec((ATT_TQ, ATT_WIDTH), cur(P_AQ // ATT_WIDTH)),
            pl.BlockSpec((ATT_TQ, ATT_WIDTH), cur(P_AZ // ATT_WIDTH)),
            pl.BlockSpec((bq, ATT_KV_WIDTH), prev(kcol)),
            pl.BlockSpec((ATT_TQ, ATT_KV_WIDTH), cur(kcol)),
            pl.BlockSpec((bq, ATT_KV_WIDTH), nxt(kcol)),
            pl.BlockSpec((bq, ATT_KV_WIDTH), prev(vcol)),
            pl.BlockSpec((ATT_TQ, ATT_KV_WIDTH), cur(vcol)),
            pl.BlockSpec((bq, ATT_KV_WIDTH), nxt(vcol)),
            pl.BlockSpec((3, bq, LANES), lambda b, i: (0, jnp.maximum(i * ATT_NSB - 1, 0), 0)),
            pl.BlockSpec((3, ATT_TQ, LANES), lambda b, i: (0, i, 0)),
            pl.BlockSpec((3, bq, LANES), lambda b, i: (0, jnp.minimum((i + 1) * ATT_NSB, nb - 1), 0)),
            pl.BlockSpec((1, LANES), const),
            pl.BlockSpec((1, LANES), const),
            pl.BlockSpec((1, ATT_HEADS), const),
        ],
        out_specs=pl.BlockSpec((ATT_TQ, ATT_WIDTH), lambda b, i: (b * nt + i, 0)),
        out_shape=jax.ShapeDtypeStruct((t, ATT_WIDTH), BF16),
        scratch_shapes=[
            pltpu.VMEM((ATT_NSB * (ATT_WIDTH // LANES) * bq, LANES), BF16),
            pltpu.VMEM((ATT_KV_HEADS, 2, ATT_TQ + 2 * bq, LANES), BF16),
            pltpu.VMEM((ATT_KV_WIDTH, ATT_TQ + 2 * bq), BF16),
            pltpu.VMEM((ATT_HEADS, ATT_TQ), F32),
            pltpu.VMEM((ATT_WIDTH, bq), F32),
        ],
        compiler_params=pltpu.CompilerParams(
            dimension_semantics=("arbitrary", "arbitrary"), vmem_limit_bytes=VMEM_LIMIT_BYTES),
        name="attention",
    )(proj, proj, proj, proj, proj, proj, proj, proj, rope_tab, rope_tab, rope_tab, gq, gk, sink)


def _mlstm_kernel(q_ref, k_ref, v_ref, o_ref, z_ref, grow_ref, ccol_ref, cwq_ref, cwk_ref, ng_ref,
                  out_ref, qct_s, kc_s, vt_s, cct_s, nf_s, nb_s, mf_s, mb_s, *, seq):
    hd = pl.program_id(1)
    L = M_CHUNK
    dv = M_HEAD_DIM
    nc = seq // L
    kscale = M_HEAD_DIM ** -0.5
    sub = 16

    row_id = lax.broadcasted_iota(jnp.int32, (L, LANES), 0)

    def conv_body(ci, carry):
        t0 = pl.multiple_of(ci * L, L)
        tp = pl.multiple_of(jnp.maximum(t0 - sub, 0), sub)
        tn = pl.multiple_of(jnp.minimum(t0 + L, seq - sub), sub)
        for src, w_ref, is_k in ((q_ref, cwq_ref, False), (k_ref, cwk_ref, True)):
            x = src[pl.ds(t0, L), :].astype(F32)
            before = src[pl.ds(tp, sub), :].astype(F32)[sub - 1:sub, :]
            after = src[pl.ds(tn, sub), :].astype(F32)[0:1, :]
            before = jnp.where(ci > 0, before, 0.0)
            after = jnp.where(ci < nc - 1, after, 0.0)
            xm = jnp.where(row_id == 0, before, pltpu.roll(x, 1, 0))
            xp = jnp.where(row_id == L - 1, after, pltpu.roll(x, L - 1, 0))
            y = xm * w_ref[0:1, :] + x * w_ref[1:2, :] + xp * w_ref[2:3, :]
            y = _silu(y)
            if is_k:
                kc_s[pl.ds(t0, L), :] = (y * kscale).astype(BF16)
            else:
                qct_s[:, pl.ds(t0, L)] = y.T.astype(BF16)
        vt_s[:, pl.ds(t0, L)] = v_ref[pl.ds(t0, L), :].astype(F32).T.astype(BF16)
        return carry

    lax.fori_loop(0, nc, conv_body, 0, unroll=MLSTM_CONV_UNROLL)

    head_row = lax.broadcasted_iota(jnp.int32, (M_HEADS, L), 0) == hd

    def gate_row(base, t0):
        rows = grow_ref[base:base + M_HEADS, pl.ds(t0, L)]
        return jnp.sum(jnp.where(head_row, rows, 0.0), axis=0, keepdims=True)

    def scan_dir(t0, ci, state, li_base, b_base, last_lane, n_s, m_s, row0):
        ct_st, n_st, m_st = state
        cct_s[ci, row0:row0 + dv, :] = ct_st.astype(BF16)
        n_s[ci] = jnp.broadcast_to(n_st, (8, LANES))
        m_s[ci] = jnp.broadcast_to(m_st, (8, LANES))
        li = gate_row(li_base, t0)
        b = gate_row(b_base, t0)
        b_last = b[:, last_lane:last_lane + 1]
        g = b_last - b + li
        m_new = jnp.maximum(b_last + m_st, jnp.max(g, axis=-1, keepdims=True))
        w_c = jnp.exp(b_last + m_st - m_new)
        w_k = jnp.exp(g - m_new)
        vtw = (vt_s[:, pl.ds(t0, L)].astype(F32) * w_k).astype(BF16)
        k = kc_s[pl.ds(t0, L), :]
        ct_new = w_c * ct_st + jnp.dot(vtw, k, preferred_element_type=F32)
        wk8 = jnp.broadcast_to(w_k, (8, L)).astype(BF16)
        n_upd = jnp.dot(wk8, k, preferred_element_type=F32)[0:1, :]
        n_new = w_c * n_st + n_upd
        return ct_new, n_new, m_new

    def scan_body(j, carry):
        st_f, st_b = carry
        cf = j
        cb = nc - 1 - j
        st_f = scan_dir(pl.multiple_of(cf * L, L), cf, st_f, ROW_LI_F, ROW_B_F, L - 1, nf_s, mf_s, 0)
        st_b = scan_dir(pl.multiple_of(cb * L, L), cb, st_b, ROW_LI_B, ROW_B_B, 0, nb_s, mb_s, dv)
        return st_f, st_b

    init = (jnp.zeros((dv, M_HEAD_DIM), F32), jnp.zeros((1, M_HEAD_DIM), F32),
            jnp.full((1, 1), NEG, F32))
    lax.fori_loop(0, nc, scan_body, (init, init), unroll=MLSTM_SCAN_UNROLL)

    rr = lax.broadcasted_iota(jnp.int32, (L, L), 0)
    cc = lax.broadcasted_iota(jnp.int32, (L, L), 1)
    row8 = lax.broadcasted_iota(jnp.int32, (8, LANES), 0)
    sel_k = lax.broadcasted_iota(jnp.int32, (LANES, 2 * L), 0)
    sel_n = lax.broadcasted_iota(jnp.int32, (LANES, 2 * L), 1)
    sel_lane = jnp.where(sel_n < L, LANE_C_F + hd, LANE_C_B + hd)
    sel = jnp.where(sel_k == sel_lane, 1.0, 0.0).astype(BF16)

    keep_f = rr <= cc
    keep_b = rr >= cc
    G = MLSTM_OUT_GROUP

    def out_body(gi, carry):
        cis = [gi * G + g for g in range(G)]
        t0s = [pl.multiple_of(ci * L, L) for ci in cis]
        qts = [qct_s[:, pl.ds(t0, L)] for t0 in t0s]
        qk_ts = [jnp.dot(kc_s[pl.ds(t0, L), :], qt, preferred_element_type=F32)
                 for t0, qt in zip(t0s, qts)]
        cqs = [jnp.dot(cct_s[ci], qt, preferred_element_type=F32) for ci, qt in zip(cis, qts)]
        n8s = [jnp.where(row8 == 0, nf_s[ci], jnp.where(row8 == 1, nb_s[ci], 0.0)).astype(BF16) for ci in cis]
        qns = [jnp.dot(n8, qt, preferred_element_type=F32) for n8, qt in zip(n8s, qts)]
        c_bcs = [_split_dot_right(ccol_ref[pl.ds(t0, L), :], sel) for t0 in t0s]

        items = []
        for g in range(G):
            items.append((g, c_bcs[g][:, :L], qns[g][0:1, :], ROW_B_F, keep_f, mf_s))
            items.append((g, c_bcs[g][:, L:], qns[g][1:2, :], ROW_B_B, keep_b, mb_s))
        b_rs = [gate_row(b_base, t0s[g]) for g, _, _, b_base, _, _ in items]
        m_prevs = [m_s[cis[g]][0:1, 0:1] for g, _, _, _, _, m_s in items]
        ds = [jnp.where(keep, b_r + c_s, NEG)
              for (_, c_s, _, _, keep, _), b_r in zip(items, b_rs)]
        m_inters = [b_r + m_prev for b_r, m_prev in zip(b_rs, m_prevs)]
        m_ts = [jnp.maximum(m_inter, jnp.max(d, axis=0, keepdims=True)) for m_inter, d in zip(m_inters, ds)]
        w_inters = [jnp.exp(m_inter - m_t) for m_inter, m_t in zip(m_inters, m_ts)]
        a_s = [jnp.exp(d - m_t) * qk_ts[it[0]] for d, m_t, it in zip(ds, m_ts, items)]
        dens = [w_inter * it[2] + jnp.sum(a, axis=0, keepdims=True) for w_inter, it, a in zip(w_inters, items, a_s)]
        invs = [1.0 / jnp.maximum(jnp.abs(den), jnp.exp(-m_t)) for den, m_t in zip(dens, m_ts)]
        ps = [a * inv for a, inv in zip(a_s, invs)]
        ws = [w_inter * inv for w_inter, inv in zip(w_inters, invs)]

        h_ts = [jnp.dot(vt_s[:, pl.ds(t0s[g], L)], (ps[2 * g] + ps[2 * g + 1]).astype(BF16),
                        preferred_element_type=F32)
                + cqs[g][:dv, :] * ws[2 * g] + cqs[g][dv:, :] * ws[2 * g + 1] for g in range(G)]
        hs = [_sigmoid(o_ref[pl.ds(t0, L), :].astype(F32)) * h_t.T for t0, h_t in zip(t0s, h_ts)]
        hs = [h * lax.rsqrt(jnp.mean(h * h, axis=-1, keepdims=True) + NORM_EPS) * ng_ref[...] for h in hs]
        for t0, h in zip(t0s, hs):
            out_ref[pl.ds(t0, L), :] = (h * _silu(z_ref[pl.ds(t0, L), :].astype(F32))).astype(out_ref.dtype)
        return carry

    lax.fori_loop(0, nc // G, out_body, 0)


def _mlstm(proj, grow, ccol, conv_w, ng, batch, seq):
    t = proj.shape[0]
    nc = seq // M_CHUNK

    def head_block(col0):
        return pl.BlockSpec((seq, M_HEAD_DIM), lambda b, h: (b, col0 // M_HEAD_DIM + h))

    return pl.pallas_call(
        functools.partial(_mlstm_kernel, seq=seq),
        grid=(batch, M_HEADS),
        in_specs=[
            head_block(P_MQ), head_block(P_MK), head_block(P_MV), head_block(P_MO), head_block(P_MZ),
            pl.BlockSpec((N_IFG, seq), lambda b, h: (0, b)),
            pl.BlockSpec((seq, LANES), lambda b, h: (b, 0)),
            pl.BlockSpec((CONV_K, M_HEAD_DIM), lambda b, h: (0, h)),
            pl.BlockSpec((CONV_K, M_HEAD_DIM), lambda b, h: (0, M_HEADS + h)),
            pl.BlockSpec((1, M_HEAD_DIM), lambda b, h: (0, h)),
        ],
        out_specs=pl.BlockSpec((seq, M_HEAD_DIM), lambda b, h: (b, h)),
        out_shape=jax.ShapeDtypeStruct((t, M_WIDTH), BF16),
        scratch_shapes=[
            pltpu.VMEM((M_HEAD_DIM, seq), BF16),
            pltpu.VMEM((seq, M_HEAD_DIM), BF16),
            pltpu.VMEM((M_HEAD_DIM, seq), BF16),
            pltpu.VMEM((nc, 2 * M_HEAD_DIM, M_HEAD_DIM), BF16),
            pltpu.VMEM((nc, 8, M_HEAD_DIM), F32),
            pltpu.VMEM((nc, 8, M_HEAD_DIM), F32),
            pltpu.VMEM((nc, 8, LANES), F32),
            pltpu.VMEM((nc, 8, LANES), F32),
        ],
        compiler_params=pltpu.CompilerParams(
            dimension_semantics=("arbitrary", "arbitrary"), vmem_limit_bytes=VMEM_LIMIT_BYTES),
        name="mlstm",
    )(proj, proj, proj, proj, proj, grow, ccol, conv_w, conv_w, ng)


def _out_proj_kernel(x_ref, a_ref, m_ref, gates_ref, wa_ref, wm_ref, wo_ref, out_ref):
    branch_a = jnp.dot(a_ref[...], wa_ref[...], preferred_element_type=F32)
    branch_m = jnp.dot(m_ref[...], wm_ref[...], preferred_element_type=F32)
    gates = _sigmoid(gates_ref[...].astype(F32))
    merged = gates[:, :D_MODEL] * branch_a + gates[:, D_MODEL:] * branch_m
    out_ref[...] = x_ref[...] + jnp.dot(merged.astype(BF16), wo_ref[...], preferred_element_type=F32)


def _out_proj(x, a, m, proj, wa, wm, wo):
    t = x.shape[0]
    tile = lambda i: (i, 0)
    const = lambda i: (0, 0)
    return pl.pallas_call(
        _out_proj_kernel,
        grid=(t // OUT_TM,),
        in_specs=[
            pl.BlockSpec((OUT_TM, D_MODEL), tile),
            pl.BlockSpec((OUT_TM, ATT_WIDTH), tile),
            pl.BlockSpec((OUT_TM, M_WIDTH), tile),
            pl.BlockSpec((OUT_TM, 2 * D_MODEL), lambda i: (i, P_GATES // (2 * D_MODEL))),
            pl.BlockSpec((ATT_WIDTH, D_MODEL), const),
            pl.BlockSpec((M_WIDTH, D_MODEL), const),
            pl.BlockSpec((D_MODEL, D_MODEL), const),
        ],
        out_specs=pl.BlockSpec((OUT_TM, D_MODEL), tile),
        out_shape=jax.ShapeDtypeStruct((t, D_MODEL), F32),
        compiler_params=pltpu.CompilerParams(
            dimension_semantics=("arbitrary",), vmem_limit_bytes=VMEM_LIMIT_BYTES),
        name="out_proj",
    )(x, a, m, proj, wa, wm, wo)


def _rope_tables(seq):
    inv = jnp.power(jnp.float32(ROPE_THETA), -jnp.arange(ROPE_HALF, dtype=F32) * 2.0 / ROPE_DIM)
    ang = jnp.arange(seq, dtype=F32)[:, None] * inv[None, :]
    cos, sin = jnp.cos(ang), jnp.sin(ang)
    pad = ATT_HEAD_DIM - ROPE_DIM
    ones = jnp.ones((seq, pad), F32)
    zeros = jnp.zeros((seq, pad), F32)
    zh = jnp.zeros((seq, ROPE_HALF), F32)
    c = jnp.concatenate([cos, cos, ones], axis=-1)
    s1 = jnp.concatenate([-sin, zh, zeros], axis=-1)
    s2 = jnp.concatenate([zh, sin, zeros], axis=-1)
    tab = jnp.stack([c, s1, s2])
    return jnp.concatenate([tab, tab], axis=-1)


def _prep_weights(w_in, b_in):
    order = [(_R_GATES, 2 * D_MODEL), (_R_AQ, ATT_WIDTH), (_R_AZ, ATT_WIDTH), (_R_AK, ATT_KV_WIDTH),
             (_R_AV, ATT_KV_WIDTH), (_R_MQ, M_WIDTH), (_R_MK, M_WIDTH), (_R_MV, M_WIDTH),
             (_R_MO, M_WIDTH), (_R_MZ, M_WIDTH)]
    w_main = jnp.concatenate([w_in[:, :, o:o + n] for o, n in order], axis=-1).astype(BF16)
    b_main = jnp.concatenate([b_in[:, o:o + n] for o, n in order], axis=-1)[:, None, :]
    w_if = w_in[:, :, _R_IF:_R_IF + N_IFG]
    b_if = b_in[:, _R_IF:_R_IF + N_IFG]
    w_if_pad = jnp.pad(w_if, ((0, 0), (0, 0), (0, LANES - N_IFG))).astype(BF16)
    b_if_pad = jnp.pad(b_if, ((0, 0), (0, LANES - N_IFG)))[:, None, :]
    w_if_t = jnp.swapaxes(w_if, 1, 2).astype(BF16)
    b_if_c = b_if[:, :, None]
    return w_main, b_main, w_if_pad, w_if_t, b_if_pad, b_if_c


def _trunk(x, rope_tab, layers):
    batch, seq, _ = x.shape
    xf = x.reshape(batch * seq, D_MODEL)
    for lw in layers:
        proj, grow, ccol = _in_proj(xf, lw["norm_g"], lw["w_main"], lw["b_main"], lw["w_if"], lw["w_if_t"],
                                    lw["b_if"], lw["b_if_c"])
        a = _attention(proj, rope_tab, lw["gq"], lw["gk"], lw["sink"], batch, seq)
        m = _mlstm(proj, grow, ccol, lw["conv_w"], lw["m_norm_g"], batch, seq)
        xf = _out_proj(xf, a, m, proj, lw["w_att_out"], lw["w_m_out"], lw["w_out"])
    return xf.reshape(batch, seq, D_MODEL)


def kernel(x_prompt, x_sample, norm_g, w_in, b_in, q_norm_g, k_norm_g, sink, conv_w, m_norm_g,
           w_att_out, w_m_out, w_out):
    w_main, b_main, w_if, w_if_t, b_if, b_if_c = _prep_weights(w_in, b_in)
    wa, wm, wo = w_att_out.astype(BF16), w_m_out.astype(BF16), w_out.astype(BF16)
    layers = []
    for l in range(DEPTH):
        layers.append(dict(
            norm_g=norm_g[l][None, :], w_main=w_main[l], b_main=b_main[l], w_if=w_if[l], w_if_t=w_if_t[l],
            b_if=b_if[l], b_if_c=b_if_c[l],
            gq=jnp.tile(q_norm_g[l], 2)[None, :], gk=jnp.tile(k_norm_g[l], 2)[None, :],
            sink=sink[l][None, :], conv_w=conv_w[l], m_norm_g=m_norm_g[l][None, :],
            w_att_out=wa[l], w_m_out=wm[l], w_out=wo[l]))
    outs = []
    for x in (x_prompt, x_sample):
        outs.append(_trunk(x, _rope_tables(x.shape[1]), layers))
    return tuple(outs)
```

```python
import functools
import math

import jax
import jax.numpy as jnp
from jax import lax
from jax.experimental import pallas as pl
from jax.experimental.pallas import tpu as pltpu

F32 = jnp.float32
BF16 = jnp.bfloat16

D_MODEL = 1024
DEPTH = 4
ATT_HEADS = 16
ATT_KV_HEADS = 4
ATT_GROUP = ATT_HEADS // ATT_KV_HEADS
ATT_HEAD_DIM = 64
ATT_WIDTH = ATT_HEADS * ATT_HEAD_DIM
ATT_KV_WIDTH = ATT_KV_HEADS * ATT_HEAD_DIM
WINDOW = 128
ATT_BLOCK = 128
ROPE_THETA = 500000.0
ROPE_DIM = ATT_HEAD_DIM // 4
ROPE_HALF = ROPE_DIM // 2
M_HEADS = 8
M_HEAD_DIM = 128
M_WIDTH = M_HEADS * M_HEAD_DIM
M_CHUNK = 128
CONV_K = 3
NORM_EPS = 1e-6
NEG = -1e30
LOG2E = math.log2(math.e)

LANES = 128
SUBLANES = 8
VMEM_LIMIT_BYTES = 56 * 1024 * 1024

_REF_SPLITS = (ATT_WIDTH, ATT_KV_WIDTH, ATT_KV_WIDTH, ATT_WIDTH,
               M_WIDTH, M_WIDTH, M_WIDTH, M_WIDTH, M_WIDTH,
               M_HEADS, M_HEADS, M_HEADS, M_HEADS, 2 * D_MODEL)
_REF_OFF = [0]
for _w in _REF_SPLITS:
    _REF_OFF.append(_REF_OFF[-1] + _w)
(_R_AQ, _R_AK, _R_AV, _R_AZ, _R_MQ, _R_MK, _R_MV, _R_MO, _R_MZ,
 _R_IF, _R_FF, _R_IB, _R_FB, _R_GATES) = _REF_OFF[:-1]

P_GATES = 0
P_AZ = P_GATES + 2 * D_MODEL
P_MO = P_AZ + ATT_WIDTH
P_MZ = P_MO + M_WIDTH
P_WIDTH = P_MZ + M_WIDTH
W_MQ = P_WIDTH
W_MK = W_MQ + M_WIDTH
W_AQ = W_MK + M_WIDTH
W_AK = W_AQ + ATT_WIDTH
W_WIDTH = W_AK + ATT_KV_WIDTH
VT_ROWS = M_WIDTH + ATT_KV_WIDTH
N_IFG = 4 * M_HEADS

ROW_LI_F, ROW_B_F, ROW_LI_B, ROW_B_B = 0, M_HEADS, 2 * M_HEADS, 3 * M_HEADS
LANE_C_F, LANE_C_B = 0, 2 * M_HEADS

PROJ_DTYPE = BF16
IN_TM = 512
IN_TN = 512
IN_PIECE = 256
VT_TN = 256
OUT_TM = 512
ATT_TQ = 512
ATT_NSB = ATT_TQ // ATT_BLOCK
ATT_PAIRS = ATT_WIDTH // LANES
MLSTM_SCAN_UNROLL = 4
MLSTM_OUT_GROUP = 4

_NT = (((1,), (1,)), ((), ()))


def _sigmoid(x):
    return 1.0 / (1.0 + jnp.exp(-x))


def _silu(x):
    return x * _sigmoid(x)


def _log_sigmoid(x):
    return jnp.minimum(x, 0.0) - jnp.log1p(jnp.exp(-jnp.abs(x)))


def _bf16_terms(x):
    hi = x.astype(BF16)
    r1 = x - hi.astype(F32)
    mid = r1.astype(BF16)
    lo = (r1 - mid.astype(F32)).astype(BF16)
    return hi, mid, lo


def _split_dot_left(sel, x):
    return sum(jnp.dot(sel, t, preferred_element_type=F32) for t in _bf16_terms(x))


def _split_dot_right(x, sel):
    return sum(jnp.dot(t, sel, preferred_element_type=F32) for t in _bf16_terms(x))


def _split_dot_nt(sel, x):
    return sum(lax.dot_general(sel, t, _NT, preferred_element_type=F32) for t in _bf16_terms(x))


def _rope(y, tab):
    return (y * tab[0]
            + pltpu.roll(y, LANES - ROPE_HALF, 1) * tab[1]
            + pltpu.roll(y, ROPE_HALF, 1) * tab[2])


def _norm_rope_tile(x, gain, tab):
    lane = lax.broadcasted_iota(jnp.int32, x.shape, 1)
    left = lane < ATT_HEAD_DIM
    sq = x * x
    ss_l = jnp.sum(jnp.where(left, sq, 0.0), axis=-1, keepdims=True)
    ss_r = jnp.sum(jnp.where(left, 0.0, sq), axis=-1, keepdims=True)
    inv = jnp.where(left, lax.rsqrt(ss_l / ATT_HEAD_DIM + NORM_EPS),
                    lax.rsqrt(ss_r / ATT_HEAD_DIM + NORM_EPS))
    return _rope(x * inv * gain, tab)


def _rmsnorm_bf16(x, g):
    return (x * lax.rsqrt(jnp.mean(x * x, axis=-1, keepdims=True) + NORM_EPS) * g).astype(BF16)


def _in_proj_kernel(x_ref, xprev_ref, xnext_ref, g_ref, w_ref, b_ref, wvt_ref, bvt_ref,
                    wif_ref, wift_ref, bif_ref, bifc_ref, cw_ref, tab_ref, gq_ref, gk_ref,
                    proj_ref, q4_ref, rq_ref, kn_ref, vta_ref, qct_ref, kc_ref, vtm_ref, grow_ref, ccol_ref,
                    *, tiles_per_seq):
    tm = x_ref.shape[0]
    tile_in_seq = pl.program_id(0) % tiles_per_seq
    xn = _rmsnorm_bf16(x_ref[...], g_ref[...])

    def chunk(c0, width):
        def piece(p0):
            cols = slice(p0, p0 + IN_PIECE)
            return lambda: jnp.dot(xn, w_ref[:, cols], preferred_element_type=F32) + b_ref[:, cols]
        return [piece(p0) for p0 in range(c0, c0 + width, IN_PIECE)]

    stages = []

    def store_plain(c0):
        def store(acc, p0):
            def run():
                proj_ref[:, p0:p0 + IN_PIECE] = acc.astype(proj_ref.dtype)
            return run
        return lambda accs: [store(acc, c0 + i * IN_PIECE) for i, acc in enumerate(accs)]

    plain = [(chunk(c0, IN_TN), store_plain(c0)) for c0 in range(0, P_WIDTH, IN_TN)]

    halo = _rmsnorm_bf16(jnp.concatenate([xprev_ref[...], xnext_ref[...]], axis=0), g_ref[...])
    halo_cols = slice(W_MQ, W_MQ + 2 * M_WIDTH)
    halo_p = jnp.dot(halo, w_ref[:, halo_cols], preferred_element_type=F32) + b_ref[:, halo_cols]
    before = jnp.where(tile_in_seq > 0, halo_p[SUBLANES - 1:SUBLANES, :], 0.0)
    after = jnp.where(tile_in_seq < tiles_per_seq - 1, halo_p[SUBLANES:SUBLANES + 1, :], 0.0)
    row_id = lax.broadcasted_iota(jnp.int32, (tm, IN_PIECE), 0)
    kscale = M_HEAD_DIM ** -0.5

    def conv_silu(c0):
        def conv(acc, p0):
            cols = slice(p0, p0 + IN_PIECE)

            def run():
                xm = jnp.where(row_id == 0, before[:, cols], pltpu.roll(acc, 1, 0))
                xp = jnp.where(row_id == tm - 1, after[:, cols], pltpu.roll(acc, tm - 1, 0))
                y = _silu(xm * cw_ref[0:1, cols] + acc * cw_ref[1:2, cols] + xp * cw_ref[2:3, cols])
                if p0 < M_WIDTH:
                    qct_ref[p0:p0 + IN_PIECE, :] = y.T.astype(BF16)
                else:
                    kc_ref[:, p0 - M_WIDTH:p0 - M_WIDTH + IN_PIECE] = (y * kscale).astype(BF16)
            return run
        return lambda accs: [conv(acc, c0 + i * IN_PIECE) for i, acc in enumerate(accs)]

    conv = [(chunk(W_MQ + c0, IN_TN), conv_silu(c0)) for c0 in range(0, 2 * M_WIDTH, IN_TN)]

    tab = tab_ref[...]
    heads_per_piece = IN_PIECE // ATT_HEAD_DIM
    head_of_col = lax.broadcasted_iota(jnp.int32, (SUBLANES, IN_PIECE), 1) >> (ATT_HEAD_DIM.bit_length() - 1)
    pick = jnp.where(head_of_col == lax.broadcasted_iota(jnp.int32, (SUBLANES, IN_PIECE), 0),
                     1.0, 0.0).astype(BF16)

    def q_rope(c0):
        def rope(acc, p0):
            def run():
                head0 = p0 // ATT_HEAD_DIM
                ssq = _split_dot_nt(pick, acc * acc)[:heads_per_piece, :]
                rq_ref[head0:head0 + heads_per_piece, :] = (
                    lax.rsqrt(ssq / ATT_HEAD_DIM + NORM_EPS) * (ATT_HEAD_DIM ** -0.5 * LOG2E))
                for p in range(IN_PIECE // LANES):
                    y = _rope(acc[:, p * LANES:(p + 1) * LANES] * gq_ref[...], tab).astype(BF16)
                    for j in range(tm // ATT_BLOCK):
                        q4_ref[j, p0 // LANES + p] = y[j * ATT_BLOCK:(j + 1) * ATT_BLOCK, :]
            return run
        return lambda accs: [rope(acc, c0 + i * IN_PIECE) for i, acc in enumerate(accs)]

    att_q = [(chunk(W_AQ + c0, IN_TN), q_rope(c0)) for c0 in range(0, ATT_WIDTH, IN_TN)]

    def k_norm_rope(accs):
        def norm_rope(acc, p0):
            def run():
                for t in range(IN_PIECE // LANES):
                    cols = slice(t * LANES, (t + 1) * LANES)
                    kn_ref[:, p0 + t * LANES:p0 + (t + 1) * LANES] = _norm_rope_tile(
                        acc[:, cols], gk_ref[...], tab).astype(BF16)
            return run
        return [norm_rope(acc, i * IN_PIECE) for i, acc in enumerate(accs)]

    att_k = (chunk(W_AK, ATT_KV_WIDTH), k_norm_rope)

    def v_t(r0):
        rows = slice(r0, r0 + VT_TN)

        def matmul():
            return lax.dot_general(wvt_ref[rows, :], xn, _NT, preferred_element_type=F32) + bvt_ref[rows, :]

        def store(vts_):
            def run():
                if r0 < M_WIDTH:
                    vtm_ref[rows, :] = vts_[0].astype(BF16)
                else:
                    vta_ref[r0 - M_WIDTH:r0 - M_WIDTH + VT_TN, :] = vts_[0].astype(BF16)
            return [run]
        return [matmul], store

    vts = [v_t(r0) for r0 in range(0, VT_ROWS, VT_TN)]

    def gate_matmul():
        gc = jnp.dot(xn, wif_ref[...], preferred_element_type=F32) + bif_ref[...]
        gr = lax.dot_general(wift_ref[...], xn, _NT, preferred_element_type=F32) + bifc_ref[...]
        return gc, gr

    def gate_epilogue(gates):
        gc, gr = gates[0]
        ls_c = _log_sigmoid(gc)
        ls_r = _log_sigmoid(gr)
        r = lax.broadcasted_iota(jnp.int32, (M_CHUNK, M_CHUNK), 0)
        c = lax.broadcasted_iota(jnp.int32, (M_CHUNK, M_CHUNK), 1)
        tril = jnp.where(c <= r, 1.0, 0.0).astype(BF16)
        triu = jnp.where(c >= r, 1.0, 0.0).astype(BF16)
        lane = lax.broadcasted_iota(jnp.int32, (M_CHUNK, LANES), 1)
        fwd_lanes = lane < 2 * M_HEADS
        grow_ref[ROW_LI_F:ROW_LI_F + M_HEADS, :] = gr[0:M_HEADS, :]
        grow_ref[ROW_LI_B:ROW_LI_B + M_HEADS, :] = gr[2 * M_HEADS:3 * M_HEADS, :]
        for ci in range(tm // M_CHUNK):
            rows = slice(ci * M_CHUNK, (ci + 1) * M_CHUNK)
            lsc = ls_c[rows, :]
            b_f = _split_dot_left(tril, lsc)
            b_b = _split_dot_left(triu, lsc)
            ccol_ref[rows, :] = gc[rows, :] - pltpu.roll(jnp.where(fwd_lanes, b_f, b_b), LANES - M_HEADS, 1)
            grow_ref[ROW_B_F:ROW_B_F + M_HEADS, rows] = _split_dot_right(ls_r[M_HEADS:2 * M_HEADS, rows], triu)
            grow_ref[ROW_B_B:ROW_B_B + M_HEADS, rows] = _split_dot_right(ls_r[3 * M_HEADS:4 * M_HEADS, rows], tril)

    light = plain + vts
    heavy = [([gate_matmul], lambda gates: [lambda: gate_epilogue(gates)])] + conv + att_q + [att_k]
    per_heavy = len(light) // len(heavy)
    for i, stage in enumerate(heavy):
        stages.append(stage)
        stages.extend(light[i * per_heavy:(i + 1) * per_heavy])
    stages.extend(light[len(heavy) * per_heavy:])

    epilogues = []
    for matmuls, make_epilogues in stages:
        results = []
        for i in range(max(len(matmuls), len(epilogues))):
            if i < len(matmuls):
                results.append(matmuls[i]())
            if i < len(epilogues):
                epilogues[i]()
        epilogues = make_epilogues(results)
    for run in epilogues:
        run()


def _in_proj(x, lw, rope_tab, seq):
    t = x.shape[0]
    tps = seq // IN_TM
    halo_blocks = IN_TM // SUBLANES
    const = lambda i: (0, 0)
    tile = lambda i: (i, 0)
    cols = lambda i: (0, i)
    resident = dict(pipeline_mode=pl.Buffered(1))
    return pl.pallas_call(
        functools.partial(_in_proj_kernel, tiles_per_seq=tps),
        grid=(t // IN_TM,),
        in_specs=[
            pl.BlockSpec((IN_TM, D_MODEL), tile),
            pl.BlockSpec((SUBLANES, D_MODEL), lambda i: (jnp.maximum(i * halo_blocks - 1, 0), 0)),
            pl.BlockSpec((SUBLANES, D_MODEL), lambda i: (jnp.minimum((i + 1) * halo_blocks, t // SUBLANES - 1), 0)),
            pl.BlockSpec((1, D_MODEL), const),
            pl.BlockSpec((D_MODEL, W_WIDTH), const, **resident),
            pl.BlockSpec((1, W_WIDTH), const),
            pl.BlockSpec((VT_ROWS, D_MODEL), const, **resident),
            pl.BlockSpec((VT_ROWS, 1), const),
            pl.BlockSpec((D_MODEL, LANES), const),
            pl.BlockSpec((N_IFG, D_MODEL), const),
            pl.BlockSpec((1, LANES), const),
            pl.BlockSpec((N_IFG, 1), const),
            pl.BlockSpec((CONV_K, 2 * M_WIDTH), const),
            pl.BlockSpec((3, IN_TM, LANES), lambda i: (0, i % tps, 0)),
            pl.BlockSpec((1, LANES), const),
            pl.BlockSpec((1, LANES), const),
        ],
        out_specs=[
            pl.BlockSpec((IN_TM, P_WIDTH), tile),
            pl.BlockSpec((IN_TM // ATT_BLOCK, ATT_PAIRS, ATT_BLOCK, LANES), lambda i: (i, 0, 0, 0)),
            pl.BlockSpec((ATT_HEADS, IN_TM), cols),
            pl.BlockSpec((IN_TM, ATT_KV_WIDTH), tile),
            pl.BlockSpec((ATT_KV_WIDTH, IN_TM), cols),
            pl.BlockSpec((M_WIDTH, IN_TM), cols),
            pl.BlockSpec((IN_TM, M_WIDTH), tile),
            pl.BlockSpec((M_WIDTH, IN_TM), cols),
            pl.BlockSpec((N_IFG, IN_TM), cols),
            pl.BlockSpec((IN_TM, LANES), tile),
        ],
        out_shape=[
            jax.ShapeDtypeStruct((t, P_WIDTH), PROJ_DTYPE),
            jax.ShapeDtypeStruct((t // ATT_BLOCK, ATT_PAIRS, ATT_BLOCK, LANES), BF16),
            jax.ShapeDtypeStruct((ATT_HEADS, t), F32),
            jax.ShapeDtypeStruct((t, ATT_KV_WIDTH), BF16),
            jax.ShapeDtypeStruct((ATT_KV_WIDTH, t), BF16),
            jax.ShapeDtypeStruct((M_WIDTH, t), BF16),
            jax.ShapeDtypeStruct((t, M_WIDTH), BF16),
            jax.ShapeDtypeStruct((M_WIDTH, t), BF16),
            jax.ShapeDtypeStruct((N_IFG, t), F32),
            jax.ShapeDtypeStruct((t, LANES), F32),
        ],
        compiler_params=pltpu.CompilerParams(
            dimension_semantics=("arbitrary",), vmem_limit_bytes=VMEM_LIMIT_BYTES),
        name="in_proj",
    )(x, x, x, lw["norm_g"], lw["w_main"], lw["b_main"], lw["w_vt"], lw["b_vt"], lw["w_if"], lw["w_if_t"],
      lw["b_if"], lw["b_if_c"], lw["conv_w"], rope_tab, lw["gq"], lw["gk"])


def _attention_kernel(q4_ref, rq_ref, z_ref, kp_ref, kc_ref, kn_ref, vp_ref, vc_ref, vn_ref, sink_ref, out_ref,
                      ks, vts, out_t, *, n_blocks):
    ti = pl.program_id(1)
    bq = ATT_BLOCK

    r0 = 0
    for k_ref, v_ref in ((kp_ref, vp_ref), (kc_ref, vc_ref), (kn_ref, vn_ref)):
        nr = k_ref.shape[0]
        left = lax.broadcasted_iota(jnp.int32, (nr, LANES), 1) < ATT_HEAD_DIM
        for t in range(ATT_KV_WIDTH // LANES):
            kt = k_ref[:, t * LANES:(t + 1) * LANES].astype(F32)
            sw = pltpu.roll(kt, ATT_HEAD_DIM, 1)
            ks[2 * t, 0, r0:r0 + nr, :] = jnp.where(left, kt, 0.0).astype(BF16)
            ks[2 * t, 1, r0:r0 + nr, :] = jnp.where(left, 0.0, sw).astype(BF16)
            ks[2 * t + 1, 0, r0:r0 + nr, :] = jnp.where(left, sw, 0.0).astype(BF16)
            ks[2 * t + 1, 1, r0:r0 + nr, :] = jnp.where(left, 0.0, kt).astype(BF16)
        vts[:, r0:r0 + nr] = v_ref[...]
        r0 += nr

    def group_heads(h):
        return (ATT_GROUP * h, ATT_GROUP * h + 2, ATT_GROUP * h + 1, ATT_GROUP * h + 3)

    sink_rows = [jnp.concatenate([jnp.broadcast_to(sink_ref[:, c:c + 1] * LOG2E, (1, bq)) for c in group_heads(h)],
                                 axis=1) for h in range(ATT_KV_HEADS)]

    kk = lax.broadcasted_iota(jnp.int32, (bq, bq), 0)
    qq = lax.broadcasted_iota(jnp.int32, (bq, bq), 1)
    prev_in_band = kk >= qq
    next_in_band = kk <= qq

    def sub_block(j, carry):
        blk = ti * ATT_NSB + j
        bias_prev = jnp.where(prev_in_band, jnp.where(blk > 0, 0.0, NEG), NEG).astype(F32)
        bias_next = jnp.where(next_in_band, jnp.where(blk < n_blocks - 1, 0.0, NEG), NEG).astype(F32)
        bias_prev = jnp.concatenate([bias_prev] * ATT_GROUP, axis=1)
        bias_next = jnp.concatenate([bias_next] * ATT_GROUP, axis=1)
        row0 = pl.multiple_of(j * bq, bq)
        q_slabs = q4_ref[j]
        for h in range(ATT_KV_HEADS):
            heads = group_heads(h)
            q_pairs = q_slabs[2 * h:2 * h + 2].reshape(2 * bq, LANES)
            s = jnp.concatenate(
                [lax.dot_general(ks[h, v, pl.ds(row0, 3 * bq), :], q_pairs, _NT, preferred_element_type=F32)
                 for v in range(2)], axis=1)
            rq_row = jnp.concatenate([rq_ref[c:c + 1, pl.ds(row0, bq)] for c in heads], axis=1)
            s = s * rq_row
            s = jnp.concatenate([s[:bq] + bias_prev, s[bq:2 * bq], s[2 * bq:] + bias_next], axis=0)
            m = jnp.maximum(jnp.max(s, axis=0, keepdims=True), sink_rows[h])
            p = jnp.exp2(s - m)
            denom = jnp.sum(p, axis=0, keepdims=True) + jnp.exp2(sink_rows[h] - m)
            o = jnp.dot(vts[h * ATT_HEAD_DIM:(h + 1) * ATT_HEAD_DIM, pl.ds(row0, 3 * bq)], p.astype(BF16),
                        preferred_element_type=F32) * (1.0 / denom)
            for i, c in enumerate(heads):
                out_t[c * ATT_HEAD_DIM:(c + 1) * ATT_HEAD_DIM, :] = o[:, i * bq:(i + 1) * bq]
        att = out_t[...].T
        z = z_ref[pl.ds(row0, bq), :].astype(F32)
        out_ref[pl.ds(row0, bq), :] = (att * _silu(z)).astype(out_ref.dtype)
        return carry

    lax.fori_loop(0, ATT_NSB, sub_block, 0)


def _attention(proj, q4, rq, kn, vta, sink, batch, seq):
    t = proj.shape[0]
    nb = seq // ATT_BLOCK
    nt = seq // ATT_TQ
    bq = ATT_BLOCK

    prev_blk = lambda b, i: b * nb + jnp.maximum(i * ATT_NSB - 1, 0)
    next_blk = lambda b, i: b * nb + jnp.minimum((i + 1) * ATT_NSB, nb - 1)
    return pl.pallas_call(
        functools.partial(_attention_kernel, n_blocks=nb),
        grid=(batch, nt),
        in_specs=[
            pl.BlockSpec((ATT_NSB, ATT_PAIRS, bq, LANES), lambda b, i: (b * nt + i, 0, 0, 0)),
            pl.BlockSpec((ATT_HEADS, ATT_TQ), lambda b, i: (0, b * nt + i)),
            pl.BlockSpec((ATT_TQ, ATT_WIDTH), lambda b, i: (b * nt + i, P_AZ // ATT_WIDTH)),
            pl.BlockSpec((bq, ATT_KV_WIDTH), lambda b, i: (prev_blk(b, i), 0)),
            pl.BlockSpec((ATT_TQ, ATT_KV_WIDTH), lambda b, i: (b * nt + i, 0)),
            pl.BlockSpec((bq, ATT_KV_WIDTH), lambda b, i: (next_blk(b, i), 0)),
            pl.BlockSpec((ATT_KV_WIDTH, bq), lambda b, i: (0, prev_blk(b, i))),
            pl.BlockSpec((ATT_KV_WIDTH, ATT_TQ), lambda b, i: (0, b * nt + i)),
            pl.BlockSpec((ATT_KV_WIDTH, bq), lambda b, i: (0, next_blk(b, i))),
            pl.BlockSpec((1, ATT_HEADS), lambda b, i: (0, 0)),
        ],
        out_specs=pl.BlockSpec((ATT_TQ, ATT_WIDTH), lambda b, i: (b * nt + i, 0)),
        out_shape=jax.ShapeDtypeStruct((t, ATT_WIDTH), BF16),
        scratch_shapes=[
            pltpu.VMEM((ATT_KV_HEADS, 2, ATT_TQ + 2 * bq, LANES), BF16),
            pltpu.VMEM((ATT_KV_WIDTH, ATT_TQ + 2 * bq), BF16),
            pltpu.VMEM((ATT_WIDTH, bq), F32),
        ],
        compiler_params=pltpu.CompilerParams(
            dimension_semantics=("arbitrary", "arbitrary"), vmem_limit_bytes=VMEM_LIMIT_BYTES),
        name="attention",
    )(q4, rq, proj, kn, kn, kn, vta, vta, vta, sink)


def _mlstm_kernel(qct_ref, kc_ref, vt_ref, o_ref, z_ref, grow_ref, ccol_ref, ng_ref,
                  out_ref, cct_s, nf_s, nb_s, mf_s, mb_s, *, seq):
    hd = pl.program_id(1)
    L = M_CHUNK
    dv = M_HEAD_DIM
    nc = seq // L

    head_row = lax.broadcasted_iota(jnp.int32, (M_HEADS, L), 0) == hd

    def gate_row(base, t0):
        rows = grow_ref[base:base + M_HEADS, pl.ds(t0, L)]
        return jnp.sum(jnp.where(head_row, rows, 0.0), axis=0, keepdims=True)

    def scan_dir(t0, ci, state, li_base, b_base, last_lane, n_s, m_s, row0):
        ct_st, n_st, m_st = state
        cct_s[ci, row0:row0 + dv, :] = ct_st.astype(BF16)
        n_s[ci] = jnp.broadcast_to(n_st, (SUBLANES, LANES))
        m_s[ci] = jnp.broadcast_to(m_st, (SUBLANES, LANES))
        li = gate_row(li_base, t0)
        b = gate_row(b_base, t0)
        b_last = b[:, last_lane:last_lane + 1]
        g = b_last - b + li
        m_new = jnp.maximum(b_last + m_st, jnp.max(g, axis=-1, keepdims=True))
        w_c = jnp.exp(b_last + m_st - m_new)
        w_k = jnp.exp(g - m_new)
        vtw = (vt_ref[:, pl.ds(t0, L)].astype(F32) * w_k).astype(BF16)
        k = kc_ref[pl.ds(t0, L), :]
        ct_new = w_c * ct_st + jnp.dot(vtw, k, preferred_element_type=F32)
        wk8 = jnp.broadcast_to(w_k, (SUBLANES, L)).astype(BF16)
        n_upd = jnp.dot(wk8, k, preferred_element_type=F32)[0:1, :]
        n_new = w_c * n_st + n_upd
        return ct_new, n_new, m_new

    def scan_body(j, carry):
        st_f, st_b = carry
        cf = j
        cb = nc - 1 - j
        st_f = scan_dir(pl.multiple_of(cf * L, L), cf, st_f, ROW_LI_F, ROW_B_F, L - 1, nf_s, mf_s, 0)
        st_b = scan_dir(pl.multiple_of(cb * L, L), cb, st_b, ROW_LI_B, ROW_B_B, 0, nb_s, mb_s, dv)
        return st_f, st_b

    init = (jnp.zeros((dv, M_HEAD_DIM), F32), jnp.zeros((1, M_HEAD_DIM), F32),
            jnp.full((1, 1), NEG, F32))
    lax.fori_loop(0, nc, scan_body, (init, init), unroll=MLSTM_SCAN_UNROLL)

    rr = lax.broadcasted_iota(jnp.int32, (L, L), 0)
    cc = lax.broadcasted_iota(jnp.int32, (L, L), 1)
    row8 = lax.broadcasted_iota(jnp.int32, (SUBLANES, LANES), 0)
    sel_k = lax.broadcasted_iota(jnp.int32, (LANES, 2 * L), 0)
    sel_n = lax.broadcasted_iota(jnp.int32, (LANES, 2 * L), 1)
    sel_lane = jnp.where(sel_n < L, LANE_C_F + hd, LANE_C_B + hd)
    sel = jnp.where(sel_k == sel_lane, 1.0, 0.0).astype(BF16)
    keep_f = rr <= cc
    keep_b = rr >= cc
    G = MLSTM_OUT_GROUP

    def out_body(gi, carry):
        cis = [gi * G + g for g in range(G)]
        t0s = [pl.multiple_of(ci * L, L) for ci in cis]
        qts = [qct_ref[:, pl.ds(t0, L)] for t0 in t0s]
        qk_ts = [jnp.dot(kc_ref[pl.ds(t0, L), :], qt, preferred_element_type=F32)
                 for t0, qt in zip(t0s, qts)]
        cqs = [jnp.dot(cct_s[ci], qt, preferred_element_type=F32) for ci, qt in zip(cis, qts)]
        n8s = [jnp.where(row8 == 0, nf_s[ci], jnp.where(row8 == 1, nb_s[ci], 0.0)).astype(BF16) for ci in cis]
        qns = [jnp.dot(n8, qt, preferred_element_type=F32) for n8, qt in zip(n8s, qts)]
        c_bcs = [_split_dot_right(ccol_ref[pl.ds(t0, L), :], sel) for t0 in t0s]

        items = []
        for g in range(G):
            items.append((g, c_bcs[g][:, :L], qns[g][0:1, :], ROW_B_F, keep_f, mf_s))
            items.append((g, c_bcs[g][:, L:], qns[g][1:2, :], ROW_B_B, keep_b, mb_s))
        b_rs = [gate_row(b_base, t0s[g]) for g, _, _, b_base, _, _ in items]
        m_prevs = [m_s[cis[g]][0:1, 0:1] for g, _, _, _, _, m_s in items]
        ds = [jnp.where(keep, b_r + c_s, NEG)
              for (_, c_s, _, _, keep, _), b_r in zip(items, b_rs)]
        m_inters = [b_r + m_prev for b_r, m_prev in zip(b_rs, m_prevs)]
        m_ts = [jnp.maximum(m_inter, jnp.max(d, axis=0, keepdims=True)) for m_inter, d in zip(m_inters, ds)]
        w_inters = [jnp.exp(m_inter - m_t) for m_inter, m_t in zip(m_inters, m_ts)]
        a_s = [jnp.exp(d - m_t) * qk_ts[it[0]] for d, m_t, it in zip(ds, m_ts, items)]
        dens = [w_inter * it[2] + jnp.sum(a, axis=0, keepdims=True) for w_inter, it, a in zip(w_inters, items, a_s)]
        invs = [1.0 / jnp.maximum(jnp.abs(den), jnp.exp(-m_t)) for den, m_t in zip(dens, m_ts)]
        ps = [a * inv for a, inv in zip(a_s, invs)]
        ws = [w_inter * inv for w_inter, inv in zip(w_inters, invs)]

        h_ts = [jnp.dot(vt_ref[:, pl.ds(t0s[g], L)], (ps[2 * g] + ps[2 * g + 1]).astype(BF16),
                        preferred_element_type=F32)
                + cqs[g][:dv, :] * ws[2 * g] + cqs[g][dv:, :] * ws[2 * g + 1] for g in range(G)]
        hs = [_sigmoid(o_ref[pl.ds(t0, L), :].astype(F32)) * h_t.T for t0, h_t in zip(t0s, h_ts)]
        hs = [h * lax.rsqrt(jnp.mean(h * h, axis=-1, keepdims=True) + NORM_EPS) * ng_ref[...] for h in hs]
        for t0, h in zip(t0s, hs):
            out_ref[pl.ds(t0, L), :] = (h * _silu(z_ref[pl.ds(t0, L), :].astype(F32))).astype(out_ref.dtype)
        return carry

    lax.fori_loop(0, nc // G, out_body, 0)


def _mlstm(proj, qct, kc, vtm, grow, ccol, ng, batch, seq):
    t = proj.shape[0]
    nc = seq // M_CHUNK
    head_rows = lambda b, h: (h, b)
    head_cols = lambda b, h: (b, h)

    def proj_head(col0):
        return pl.BlockSpec((seq, M_HEAD_DIM), lambda b, h: (b, col0 // M_HEAD_DIM + h))

    return pl.pallas_call(
        functools.partial(_mlstm_kernel, seq=seq),
        grid=(batch, M_HEADS),
        in_specs=[
            pl.BlockSpec((M_HEAD_DIM, seq), head_rows),
            pl.BlockSpec((seq, M_HEAD_DIM), head_cols),
            pl.BlockSpec((M_HEAD_DIM, seq), head_rows),
            proj_head(P_MO), proj_head(P_MZ),
            pl.BlockSpec((N_IFG, seq), lambda b, h: (0, b)),
            pl.BlockSpec((seq, LANES), lambda b, h: (b, 0)),
            pl.BlockSpec((1, M_HEAD_DIM), lambda b, h: (0, h)),
        ],
        out_specs=pl.BlockSpec((seq, M_HEAD_DIM), head_cols),
        out_shape=jax.ShapeDtypeStruct((t, M_WIDTH), BF16),
        scratch_shapes=[
            pltpu.VMEM((nc, 2 * M_HEAD_DIM, M_HEAD_DIM), BF16),
            pltpu.VMEM((nc, SUBLANES, M_HEAD_DIM), F32),
            pltpu.VMEM((nc, SUBLANES, M_HEAD_DIM), F32),
            pltpu.VMEM((nc, SUBLANES, LANES), F32),
            pltpu.VMEM((nc, SUBLANES, LANES), F32),
        ],
        compiler_params=pltpu.CompilerParams(
            dimension_semantics=("arbitrary", "arbitrary"), vmem_limit_bytes=VMEM_LIMIT_BYTES),
        name="mlstm",
    )(qct, kc, vtm, proj, proj, grow, ccol, ng)


def _out_proj_kernel(x_ref, a_ref, m_ref, gates_ref, wa_ref, wm_ref, wo_ref, out_ref):
    branch_a = jnp.dot(a_ref[...], wa_ref[...], preferred_element_type=F32)
    branch_m = jnp.dot(m_ref[...], wm_ref[...], preferred_element_type=F32)
    gates = _sigmoid(gates_ref[...].astype(F32))
    merged = gates[:, :D_MODEL] * branch_a + gates[:, D_MODEL:] * branch_m
    out_ref[...] = x_ref[...] + jnp.dot(merged.astype(BF16), wo_ref[...], preferred_element_type=F32)


def _out_proj(x, a, m, proj, wa, wm, wo):
    t = x.shape[0]
    tile = lambda i: (i, 0)
    const = lambda i: (0, 0)
    return pl.pallas_call(
        _out_proj_kernel,
        grid=(t // OUT_TM,),
        in_specs=[
            pl.BlockSpec((OUT_TM, D_MODEL), tile),
            pl.BlockSpec((OUT_TM, ATT_WIDTH), tile),
            pl.BlockSpec((OUT_TM, M_WIDTH), tile),
            pl.BlockSpec((OUT_TM, 2 * D_MODEL), lambda i: (i, P_GATES // (2 * D_MODEL))),
            pl.BlockSpec((ATT_WIDTH, D_MODEL), const),
            pl.BlockSpec((M_WIDTH, D_MODEL), const),
            pl.BlockSpec((D_MODEL, D_MODEL), const),
        ],
        out_specs=pl.BlockSpec((OUT_TM, D_MODEL), tile),
        out_shape=jax.ShapeDtypeStruct((t, D_MODEL), F32),
        compiler_params=pltpu.CompilerParams(
            dimension_semantics=("arbitrary",), vmem_limit_bytes=VMEM_LIMIT_BYTES),
        name="out_proj",
    )(x, a, m, proj, wa, wm, wo)


def _rope_tables(seq):
    inv = jnp.power(jnp.float32(ROPE_THETA), -jnp.arange(ROPE_HALF, dtype=F32) * 2.0 / ROPE_DIM)
    ang = jnp.arange(seq, dtype=F32)[:, None] * inv[None, :]
    cos, sin = jnp.cos(ang), jnp.sin(ang)
    pad = ATT_HEAD_DIM - ROPE_DIM
    ones = jnp.ones((seq, pad), F32)
    zeros = jnp.zeros((seq, pad), F32)
    zh = jnp.zeros((seq, ROPE_HALF), F32)
    c = jnp.concatenate([cos, cos, ones], axis=-1)
    s1 = jnp.concatenate([-sin, zh, zeros], axis=-1)
    s2 = jnp.concatenate([zh, sin, zeros], axis=-1)
    tab = jnp.stack([c, s1, s2])
    return jnp.concatenate([tab, tab], axis=-1)


def _prep_weights(w_in, b_in):
    order = [(_R_GATES, 2 * D_MODEL), (_R_AZ, ATT_WIDTH), (_R_MO, M_WIDTH), (_R_MZ, M_WIDTH),
             (_R_MQ, M_WIDTH), (_R_MK, M_WIDTH), (_R_AQ, ATT_WIDTH), (_R_AK, ATT_KV_WIDTH)]
    w_main = jnp.concatenate([w_in[:, :, o:o + n] for o, n in order], axis=-1).astype(BF16)
    b_main = jnp.concatenate([b_in[:, o:o + n] for o, n in order], axis=-1)[:, None, :]
    v_order = [(_R_MV, M_WIDTH), (_R_AV, ATT_KV_WIDTH)]
    w_vt = jnp.swapaxes(jnp.concatenate([w_in[:, :, o:o + n] for o, n in v_order], axis=-1), 1, 2).astype(BF16)
    b_vt = jnp.concatenate([b_in[:, o:o + n] for o, n in v_order], axis=-1)[:, :, None]
    w_if = w_in[:, :, _R_IF:_R_IF + N_IFG]
    b_if = b_in[:, _R_IF:_R_IF + N_IFG]
    w_if_pad = jnp.pad(w_if, ((0, 0), (0, 0), (0, LANES - N_IFG))).astype(BF16)
    b_if_pad = jnp.pad(b_if, ((0, 0), (0, LANES - N_IFG)))[:, None, :]
    w_if_t = jnp.swapaxes(w_if, 1, 2).astype(BF16)
    b_if_c = b_if[:, :, None]
    return w_main, b_main, w_vt, b_vt, w_if_pad, w_if_t, b_if_pad, b_if_c


def _trunk(x, rope_tab, layers):
    batch, seq, _ = x.shape
    xf = x.reshape(batch * seq, D_MODEL)
    for lw in layers:
        proj, q4, rq, kn, vta, qct, kc, vtm, grow, ccol = _in_proj(xf, lw, rope_tab, seq)
        a = _attention(proj, q4, rq, kn, vta, lw["sink"], batch, seq)
        m = _mlstm(proj, qct, kc, vtm, grow, ccol, lw["m_norm_g"], batch, seq)
        xf = _out_proj(xf, a, m, proj, lw["w_att_out"], lw["w_m_out"], lw["w_out"])
    return xf.reshape(batch, seq, D_MODEL)


def kernel(x_prompt, x_sample, norm_g, w_in, b_in, q_norm_g, k_norm_g, sink, conv_w, m_norm_g,
           w_att_out, w_m_out, w_out):
    w_main, b_main, w_vt, b_vt, w_if, w_if_t, b_if, b_if_c = _prep_weights(w_in, b_in)
    wa, wm, wo = w_att_out.astype(BF16), w_m_out.astype(BF16), w_out.astype(BF16)
    layers = []
    for l in range(DEPTH):
        layers.append(dict(
            norm_g=norm_g[l][None, :], w_main=w_main[l], b_main=b_main[l], w_vt=w_vt[l], b_vt=b_vt[l],
            w_if=w_if[l], w_if_t=w_if_t[l], b_if=b_if[l], b_if_c=b_if_c[l],
            gq=jnp.tile(q_norm_g[l], 2)[None, :], gk=jnp.tile(k_norm_g[l], 2)[None, :],
            sink=sink[l][None, :], conv_w=conv_w[l], m_norm_g=m_norm_g[l][None, :],
            w_att_out=wa[l], w_m_out=wm[l], w_out=wo[l]))
    outs = []
    for x in (x_prompt, x_sample):
        outs.append(_trunk(x, _rope_tables(x.shape[1]), layers))
    return tuple(outs)
```

```python
import functools
import math

import jax
import jax.numpy as jnp
from jax import lax
from jax.experimental import pallas as pl
from jax.experimental.pallas import tpu as pltpu

F32 = jnp.float32
BF16 = jnp.bfloat16

D_MODEL = 1024
DEPTH = 4
ATT_HEADS = 16
ATT_KV_HEADS = 4
ATT_GROUP = ATT_HEADS // ATT_KV_HEADS
ATT_HEAD_DIM = 64
ATT_WIDTH = ATT_HEADS * ATT_HEAD_DIM
ATT_KV_WIDTH = ATT_KV_HEADS * ATT_HEAD_DIM
WINDOW = 128
ATT_BLOCK = 128
ROPE_THETA = 500000.0
ROPE_DIM = ATT_HEAD_DIM // 4
ROPE_HALF = ROPE_DIM // 2
M_HEADS = 8
M_HEAD_DIM = 128
M_WIDTH = M_HEADS * M_HEAD_DIM
M_CHUNK = 128
CONV_K = 3
NORM_EPS = 1e-6
NEG = -1e30
LOG2E = math.log2(math.e)

LANES = 128
SUBLANES = 8
VMEM_LIMIT_BYTES = 56 * 1024 * 1024

_REF_SPLITS = (ATT_WIDTH, ATT_KV_WIDTH, ATT_KV_WIDTH, ATT_WIDTH,
               M_WIDTH, M_WIDTH, M_WIDTH, M_WIDTH, M_WIDTH,
               M_HEADS, M_HEADS, M_HEADS, M_HEADS, 2 * D_MODEL)
_REF_OFF = [0]
for _w in _REF_SPLITS:
    _REF_OFF.append(_REF_OFF[-1] + _w)
(_R_AQ, _R_AK, _R_AV, _R_AZ, _R_MQ, _R_MK, _R_MV, _R_MO, _R_MZ,
 _R_IF, _R_FF, _R_IB, _R_FB, _R_GATES) = _REF_OFF[:-1]

P_GATES = 0
P_AZ = P_GATES + 2 * D_MODEL
P_MO = P_AZ + ATT_WIDTH
P_MZ = P_MO + M_WIDTH
P_WIDTH = P_MZ + M_WIDTH
W_MQ = P_WIDTH
W_MK = W_MQ + M_WIDTH
W_AQ = W_MK + M_WIDTH
W_AK = W_AQ + ATT_WIDTH
W_WIDTH = W_AK + ATT_KV_WIDTH
VT_ROWS = M_WIDTH + ATT_KV_WIDTH
N_IFG = 4 * M_HEADS

ROW_LI_F, ROW_B_F, ROW_LI_B, ROW_B_B = 0, M_HEADS, 2 * M_HEADS, 3 * M_HEADS
LANE_C_F, LANE_C_B = 0, 2 * M_HEADS

PROJ_DTYPE = BF16
IN_TM = 512
IN_TN = 512
IN_PIECE = 256
VT_TN = 256
OUT_TM = 512
ATT_TQ = 512
ATT_NSB = ATT_TQ // ATT_BLOCK
ATT_PAIRS = ATT_WIDTH // LANES
MLSTM_SCAN_GROUP = 8
MLSTM_OUT_GROUP = 8

_NT = (((1,), (1,)), ((), ()))


def _sigmoid(x):
    return 1.0 / (1.0 + jnp.exp(-x))


def _silu(x):
    return x * _sigmoid(x)


def _log_sigmoid(x):
    return jnp.minimum(x, 0.0) - jnp.log1p(jnp.exp(-jnp.abs(x)))


def _bf16_terms(x):
    hi = x.astype(BF16)
    r1 = x - hi.astype(F32)
    mid = r1.astype(BF16)
    lo = (r1 - mid.astype(F32)).astype(BF16)
    return hi, mid, lo


def _split_dot_left(sel, x):
    return sum(jnp.dot(sel, t, preferred_element_type=F32) for t in _bf16_terms(x))


def _split_dot_right(x, sel):
    return sum(jnp.dot(t, sel, preferred_element_type=F32) for t in _bf16_terms(x))


def _split_dot_nt(sel, x):
    return sum(lax.dot_general(sel, t, _NT, preferred_element_type=F32) for t in _bf16_terms(x))


def _rope(y, tab):
    return (y * tab[0]
            + pltpu.roll(y, LANES - ROPE_HALF, 1) * tab[1]
            + pltpu.roll(y, ROPE_HALF, 1) * tab[2])


def _norm_rope_tile(x, gain, tab):
    lane = lax.broadcasted_iota(jnp.int32, x.shape, 1)
    left = lane < ATT_HEAD_DIM
    sq = x * x
    ss_l = jnp.sum(jnp.where(left, sq, 0.0), axis=-1, keepdims=True)
    ss_r = jnp.sum(jnp.where(left, 0.0, sq), axis=-1, keepdims=True)
    inv = jnp.where(left, lax.rsqrt(ss_l / ATT_HEAD_DIM + NORM_EPS),
                    lax.rsqrt(ss_r / ATT_HEAD_DIM + NORM_EPS))
    return _rope(x * inv * gain, tab)


def _rmsnorm_bf16(x, g):
    return (x * lax.rsqrt(jnp.mean(x * x, axis=-1, keepdims=True) + NORM_EPS) * g).astype(BF16)


def _in_proj_kernel(x_ref, xprev_ref, xnext_ref, g_ref, w_ref, b_ref, wvt_ref, bvt_ref,
                    wif_ref, wift_ref, bif_ref, bifc_ref, cw_ref, tab_ref, gq_ref, gk_ref,
                    proj_ref, q4_ref, rq_ref, kn_ref, vta_ref, qct_ref, kc_ref, vtm_ref, grow_ref, ccol_ref,
                    *, tiles_per_seq):
    tm = x_ref.shape[0]
    tile_in_seq = pl.program_id(0) % tiles_per_seq
    xn = _rmsnorm_bf16(x_ref[...], g_ref[...])

    def chunk(c0, width):
        def piece(p0):
            cols = slice(p0, p0 + IN_PIECE)
            return lambda: jnp.dot(xn, w_ref[:, cols], preferred_element_type=F32) + b_ref[:, cols]
        return [piece(p0) for p0 in range(c0, c0 + width, IN_PIECE)]

    stages = []

    def store_plain(c0):
        def store(acc, p0):
            def run():
                proj_ref[:, p0:p0 + IN_PIECE] = acc.astype(proj_ref.dtype)
            return run
        return lambda accs: [store(acc, c0 + i * IN_PIECE) for i, acc in enumerate(accs)]

    plain = [(chunk(c0, IN_TN), store_plain(c0)) for c0 in range(0, P_WIDTH, IN_TN)]

    halo = _rmsnorm_bf16(jnp.concatenate([xprev_ref[...], xnext_ref[...]], axis=0), g_ref[...])
    halo_cols = slice(W_MQ, W_MQ + 2 * M_WIDTH)
    halo_p = jnp.dot(halo, w_ref[:, halo_cols], preferred_element_type=F32) + b_ref[:, halo_cols]
    before = jnp.where(tile_in_seq > 0, halo_p[SUBLANES - 1:SUBLANES, :], 0.0)
    after = jnp.where(tile_in_seq < tiles_per_seq - 1, halo_p[SUBLANES:SUBLANES + 1, :], 0.0)
    row_id = lax.broadcasted_iota(jnp.int32, (tm, IN_PIECE), 0)
    kscale = M_HEAD_DIM ** -0.5

    def conv_silu(c0):
        def conv(acc, p0):
            cols = slice(p0, p0 + IN_PIECE)

            def run():
                xm = jnp.where(row_id == 0, before[:, cols], pltpu.roll(acc, 1, 0))
                xp = jnp.where(row_id == tm - 1, after[:, cols], pltpu.roll(acc, tm - 1, 0))
                y = _silu(xm * cw_ref[0:1, cols] + acc * cw_ref[1:2, cols] + xp * cw_ref[2:3, cols])
                if p0 < M_WIDTH:
                    qct_ref[p0:p0 + IN_PIECE, :] = y.T.astype(BF16)
                else:
                    kc_ref[:, p0 - M_WIDTH:p0 - M_WIDTH + IN_PIECE] = (y * kscale).astype(BF16)
            return run
        return lambda accs: [conv(acc, c0 + i * IN_PIECE) for i, acc in enumerate(accs)]

    conv = [(chunk(W_MQ + c0, IN_TN), conv_silu(c0)) for c0 in range(0, 2 * M_WIDTH, IN_TN)]

    tab = tab_ref[...]
    heads_per_piece = IN_PIECE // ATT_HEAD_DIM
    head_of_col = lax.broadcasted_iota(jnp.int32, (SUBLANES, IN_PIECE), 1) >> (ATT_HEAD_DIM.bit_length() - 1)
    pick = jnp.where(head_of_col == lax.broadcasted_iota(jnp.int32, (SUBLANES, IN_PIECE), 0),
                     1.0, 0.0).astype(BF16)

    def q_rope(c0):
        def rope(acc, p0):
            def run():
                head0 = p0 // ATT_HEAD_DIM
                ssq = _split_dot_nt(pick, acc * acc)[:heads_per_piece, :]
                rq_ref[head0:head0 + heads_per_piece, :] = (
                    lax.rsqrt(ssq / ATT_HEAD_DIM + NORM_EPS) * (ATT_HEAD_DIM ** -0.5 * LOG2E))
                for p in range(IN_PIECE // LANES):
                    y = _rope(acc[:, p * LANES:(p + 1) * LANES] * gq_ref[...], tab).astype(BF16)
                    for j in range(tm // ATT_BLOCK):
                        q4_ref[j, p0 // LANES + p] = y[j * ATT_BLOCK:(j + 1) * ATT_BLOCK, :]
            return run
        return lambda accs: [rope(acc, c0 + i * IN_PIECE) for i, acc in enumerate(accs)]

    att_q = [(chunk(W_AQ + c0, IN_TN), q_rope(c0)) for c0 in range(0, ATT_WIDTH, IN_TN)]

    def k_norm_rope(accs):
        def norm_rope(acc, p0):
            def run():
                for t in range(IN_PIECE // LANES):
                    cols = slice(t * LANES, (t + 1) * LANES)
                    kn_ref[:, p0 + t * LANES:p0 + (t + 1) * LANES] = _norm_rope_tile(
                        acc[:, cols], gk_ref[...], tab).astype(BF16)
            return run
        return [norm_rope(acc, i * IN_PIECE) for i, acc in enumerate(accs)]

    att_k = (chunk(W_AK, ATT_KV_WIDTH), k_norm_rope)

    def v_t(r0):
        rows = slice(r0, r0 + VT_TN)

        def matmul():
            return lax.dot_general(wvt_ref[rows, :], xn, _NT, preferred_element_type=F32) + bvt_ref[rows, :]

        def store(vts_):
            def run():
                if r0 < M_WIDTH:
                    vtm_ref[rows, :] = vts_[0].astype(BF16)
                else:
                    vta_ref[r0 - M_WIDTH:r0 - M_WIDTH + VT_TN, :] = vts_[0].astype(BF16)
            return [run]
        return [matmul], store

    vts = [v_t(r0) for r0 in range(0, VT_ROWS, VT_TN)]

    def gate_matmul():
        gc = jnp.dot(xn, wif_ref[...], preferred_element_type=F32) + bif_ref[...]
        gr = lax.dot_general(wift_ref[...], xn, _NT, preferred_element_type=F32) + bifc_ref[...]
        return gc, gr

    def gate_epilogue(gates):
        gc, gr = gates[0]
        ls_c = _log_sigmoid(gc)
        ls_r = _log_sigmoid(gr)
        r = lax.broadcasted_iota(jnp.int32, (M_CHUNK, M_CHUNK), 0)
        c = lax.broadcasted_iota(jnp.int32, (M_CHUNK, M_CHUNK), 1)
        tril = jnp.where(c <= r, 1.0, 0.0).astype(BF16)
        triu = jnp.where(c >= r, 1.0, 0.0).astype(BF16)
        lane = lax.broadcasted_iota(jnp.int32, (M_CHUNK, LANES), 1)
        fwd_lanes = lane < 2 * M_HEADS
        grow_ref[ROW_LI_F:ROW_LI_F + M_HEADS, :] = gr[0:M_HEADS, :]
        grow_ref[ROW_LI_B:ROW_LI_B + M_HEADS, :] = gr[2 * M_HEADS:3 * M_HEADS, :]
        for ci in range(tm // M_CHUNK):
            rows = slice(ci * M_CHUNK, (ci + 1) * M_CHUNK)
            lsc = ls_c[rows, :]
            b_f = _split_dot_left(tril, lsc)
            b_b = _split_dot_left(triu, lsc)
            ccol_ref[rows, :] = gc[rows, :] - pltpu.roll(jnp.where(fwd_lanes, b_f, b_b), LANES - M_HEADS, 1)
            grow_ref[ROW_B_F:ROW_B_F + M_HEADS, rows] = _split_dot_right(ls_r[M_HEADS:2 * M_HEADS, rows], triu)
            grow_ref[ROW_B_B:ROW_B_B + M_HEADS, rows] = _split_dot_right(ls_r[3 * M_HEADS:4 * M_HEADS, rows], tril)

    light = plain + vts
    heavy = [([gate_matmul], lambda gates: [lambda: gate_epilogue(gates)])] + conv + att_q + [att_k]
    per_heavy = len(light) // len(heavy)
    for i, stage in enumerate(heavy):
        stages.append(stage)
        stages.extend(light[i * per_heavy:(i + 1) * per_heavy])
    stages.extend(light[len(heavy) * per_heavy:])

    epilogues = []
    for matmuls, make_epilogues in stages:
        results = []
        for i in range(max(len(matmuls), len(epilogues))):
            if i < len(matmuls):
                results.append(matmuls[i]())
            if i < len(epilogues):
                epilogues[i]()
        epilogues = make_epilogues(results)
    for run in epilogues:
        run()


def _in_proj(x, lw, rope_tab, seq):
    t = x.shape[0]
    tps = seq // IN_TM
    halo_blocks = IN_TM // SUBLANES
    const = lambda i: (0, 0)
    tile = lambda i: (i, 0)
    cols = lambda i: (0, i)
    resident = dict(pipeline_mode=pl.Buffered(1))
    return pl.pallas_call(
        functools.partial(_in_proj_kernel, tiles_per_seq=tps),
        grid=(t // IN_TM,),
        in_specs=[
            pl.BlockSpec((IN_TM, D_MODEL), tile),
            pl.BlockSpec((SUBLANES, D_MODEL), lambda i: (jnp.maximum(i * halo_blocks - 1, 0), 0)),
            pl.BlockSpec((SUBLANES, D_MODEL), lambda i: (jnp.minimum((i + 1) * halo_blocks, t // SUBLANES - 1), 0)),
            pl.BlockSpec((1, D_MODEL), const),
            pl.BlockSpec((D_MODEL, W_WIDTH), const, **resident),
            pl.BlockSpec((1, W_WIDTH), const),
            pl.BlockSpec((VT_ROWS, D_MODEL), const, **resident),
            pl.BlockSpec((VT_ROWS, 1), const),
            pl.BlockSpec((D_MODEL, LANES), const),
            pl.BlockSpec((N_IFG, D_MODEL), const),
            pl.BlockSpec((1, LANES), const),
            pl.BlockSpec((N_IFG, 1), const),
            pl.BlockSpec((CONV_K, 2 * M_WIDTH), const),
            pl.BlockSpec((3, IN_TM, LANES), lambda i: (0, i % tps, 0)),
            pl.BlockSpec((1, LANES), const),
            pl.BlockSpec((1, LANES), const),
        ],
        out_specs=[
            pl.BlockSpec((IN_TM, P_WIDTH), tile),
            pl.BlockSpec((IN_TM // ATT_BLOCK, ATT_PAIRS, ATT_BLOCK, LANES), lambda i: (i, 0, 0, 0)),
            pl.BlockSpec((ATT_HEADS, IN_TM), cols),
            pl.BlockSpec((IN_TM, ATT_KV_WIDTH), tile),
            pl.BlockSpec((ATT_KV_WIDTH, IN_TM), cols),
            pl.BlockSpec((M_WIDTH, IN_TM), cols),
            pl.BlockSpec((IN_TM, M_WIDTH), tile),
            pl.BlockSpec((M_WIDTH, IN_TM), cols),
            pl.BlockSpec((N_IFG, IN_TM), cols),
            pl.BlockSpec((IN_TM, LANES), tile),
        ],
        out_shape=[
            jax.ShapeDtypeStruct((t, P_WIDTH), PROJ_DTYPE),
            jax.ShapeDtypeStruct((t // ATT_BLOCK, ATT_PAIRS, ATT_BLOCK, LANES), BF16),
            jax.ShapeDtypeStruct((ATT_HEADS, t), F32),
            jax.ShapeDtypeStruct((t, ATT_KV_WIDTH), BF16),
            jax.ShapeDtypeStruct((ATT_KV_WIDTH, t), BF16),
            jax.ShapeDtypeStruct((M_WIDTH, t), BF16),
            jax.ShapeDtypeStruct((t, M_WIDTH), BF16),
            jax.ShapeDtypeStruct((M_WIDTH, t), BF16),
            jax.ShapeDtypeStruct((N_IFG, t), F32),
            jax.ShapeDtypeStruct((t, LANES), F32),
        ],
        compiler_params=pltpu.CompilerParams(
            dimension_semantics=("arbitrary",), vmem_limit_bytes=VMEM_LIMIT_BYTES),
        name="in_proj",
    )(x, x, x, lw["norm_g"], lw["w_main"], lw["b_main"], lw["w_vt"], lw["b_vt"], lw["w_if"], lw["w_if_t"],
      lw["b_if"], lw["b_if_c"], lw["conv_w"], rope_tab, lw["gq"], lw["gk"])


def _attention_kernel(q4_ref, rq_ref, z_ref, kp_ref, kc_ref, kn_ref, vp_ref, vc_ref, vn_ref, sink_ref, out_ref,
                      ks, vts, out_t, *, n_blocks):
    ti = pl.program_id(1)
    bq = ATT_BLOCK

    r0 = 0
    for k_ref, v_ref in ((kp_ref, vp_ref), (kc_ref, vc_ref), (kn_ref, vn_ref)):
        nr = k_ref.shape[0]
        left = lax.broadcasted_iota(jnp.int32, (nr, LANES), 1) < ATT_HEAD_DIM
        for t in range(ATT_KV_WIDTH // LANES):
            kt = k_ref[:, t * LANES:(t + 1) * LANES].astype(F32)
            sw = pltpu.roll(kt, ATT_HEAD_DIM, 1)
            ks[2 * t, 0, r0:r0 + nr, :] = jnp.where(left, kt, 0.0).astype(BF16)
            ks[2 * t, 1, r0:r0 + nr, :] = jnp.where(left, 0.0, sw).astype(BF16)
            ks[2 * t + 1, 0, r0:r0 + nr, :] = jnp.where(left, sw, 0.0).astype(BF16)
            ks[2 * t + 1, 1, r0:r0 + nr, :] = jnp.where(left, 0.0, kt).astype(BF16)
        vts[:, r0:r0 + nr] = v_ref[...]
        r0 += nr

    def group_heads(h):
        return (ATT_GROUP * h, ATT_GROUP * h + 2, ATT_GROUP * h + 1, ATT_GROUP * h + 3)

    sink_rows = [jnp.concatenate([jnp.broadcast_to(sink_ref[:, c:c + 1] * LOG2E, (1, bq)) for c in group_heads(h)],
                                 axis=1) for h in range(ATT_KV_HEADS)]

    kk = lax.broadcasted_iota(jnp.int32, (bq, bq), 0)
    qq = lax.broadcasted_iota(jnp.int32, (bq, bq), 1)
    prev_in_band = kk >= qq
    next_in_band = kk <= qq

    def sub_block(j, carry):
        blk = ti * ATT_NSB + j
        bias_prev = jnp.where(prev_in_band, jnp.where(blk > 0, 0.0, NEG), NEG).astype(F32)
        bias_next = jnp.where(next_in_band, jnp.where(blk < n_blocks - 1, 0.0, NEG), NEG).astype(F32)
        bias_prev = jnp.concatenate([bias_prev] * ATT_GROUP, axis=1)
        bias_next = jnp.concatenate([bias_next] * ATT_GROUP, axis=1)
        row0 = pl.multiple_of(j * bq, bq)
        q_slabs = q4_ref[j]
        hs = range(ATT_KV_HEADS)
        q_pairs = [q_slabs[2 * h:2 * h + 2].reshape(2 * bq, LANES) for h in hs]
        ss = [jnp.concatenate(
            [lax.dot_general(ks[h, v, pl.ds(row0, 3 * bq), :], q_pairs[h], _NT, preferred_element_type=F32)
             for v in range(2)], axis=1) for h in hs]
        rq_rows = [jnp.concatenate([rq_ref[c:c + 1, pl.ds(row0, bq)] for c in group_heads(h)], axis=1) for h in hs]
        ss = [s * rq_row for s, rq_row in zip(ss, rq_rows)]
        ss = [jnp.concatenate([s[:bq] + bias_prev, s[bq:2 * bq], s[2 * bq:] + bias_next], axis=0) for s in ss]
        ms = [jnp.maximum(jnp.max(s, axis=0, keepdims=True), sink_rows[h]) for h, s in zip(hs, ss)]
        ps = [jnp.exp2(s - m) for s, m in zip(ss, ms)]
        denoms = [jnp.sum(p, axis=0, keepdims=True) + jnp.exp2(sink_rows[h] - m) for h, p, m in zip(hs, ps, ms)]
        os_ = [jnp.dot(vts[h * ATT_HEAD_DIM:(h + 1) * ATT_HEAD_DIM, pl.ds(row0, 3 * bq)], p.astype(BF16),
                       preferred_element_type=F32) * (1.0 / denom)
               for h, p, denom in zip(hs, ps, denoms)]
        for h, o in zip(hs, os_):
            for i, c in enumerate(group_heads(h)):
                out_t[c * ATT_HEAD_DIM:(c + 1) * ATT_HEAD_DIM, :] = o[:, i * bq:(i + 1) * bq]
        att = out_t[...].T
        z = z_ref[pl.ds(row0, bq), :].astype(F32)
        out_ref[pl.ds(row0, bq), :] = (att * _silu(z)).astype(out_ref.dtype)
        return carry

    lax.fori_loop(0, ATT_NSB, sub_block, 0)


def _attention(proj, q4, rq, kn, vta, sink, batch, seq):
    t = proj.shape[0]
    nb = seq // ATT_BLOCK
    nt = seq // ATT_TQ
    bq = ATT_BLOCK

    prev_blk = lambda b, i: b * nb + jnp.maximum(i * ATT_NSB - 1, 0)
    next_blk = lambda b, i: b * nb + jnp.minimum((i + 1) * ATT_NSB, nb - 1)
    return pl.pallas_call(
        functools.partial(_attention_kernel, n_blocks=nb),
        grid=(batch, nt),
        in_specs=[
            pl.BlockSpec((ATT_NSB, ATT_PAIRS, bq, LANES), lambda b, i: (b * nt + i, 0, 0, 0)),
            pl.BlockSpec((ATT_HEADS, ATT_TQ), lambda b, i: (0, b * nt + i)),
            pl.BlockSpec((ATT_TQ, ATT_WIDTH), lambda b, i: (b * nt + i, P_AZ // ATT_WIDTH)),
            pl.BlockSpec((bq, ATT_KV_WIDTH), lambda b, i: (prev_blk(b, i), 0)),
            pl.BlockSpec((ATT_TQ, ATT_KV_WIDTH), lambda b, i: (b * nt + i, 0)),
            pl.BlockSpec((bq, ATT_KV_WIDTH), lambda b, i: (next_blk(b, i), 0)),
            pl.BlockSpec((ATT_KV_WIDTH, bq), lambda b, i: (0, prev_blk(b, i))),
            pl.BlockSpec((ATT_KV_WIDTH, ATT_TQ), lambda b, i: (0, b * nt + i)),
            pl.BlockSpec((ATT_KV_WIDTH, bq), lambda b, i: (0, next_blk(b, i))),
            pl.BlockSpec((1, ATT_HEADS), lambda b, i: (0, 0)),
        ],
        out_specs=pl.BlockSpec((ATT_TQ, ATT_WIDTH), lambda b, i: (b * nt + i, 0)),
        out_shape=jax.ShapeDtypeStruct((t, ATT_WIDTH), BF16),
        scratch_shapes=[
            pltpu.VMEM((ATT_KV_HEADS, 2, ATT_TQ + 2 * bq, LANES), BF16),
            pltpu.VMEM((ATT_KV_WIDTH, ATT_TQ + 2 * bq), BF16),
            pltpu.VMEM((ATT_WIDTH, bq), F32),
        ],
        compiler_params=pltpu.CompilerParams(
            dimension_semantics=("arbitrary", "arbitrary"), vmem_limit_bytes=VMEM_LIMIT_BYTES),
        name="attention",
    )(q4, rq, proj, kn, kn, kn, vta, vta, vta, sink)


def _mlstm_kernel(qct_ref, kc_ref, vt_ref, o_ref, z_ref, grow_ref, ccol_ref, ng_ref,
                  out_ref, cct_s, nf_s, nb_s, mf_s, mb_s, *, seq):
    hd = pl.program_id(1)
    L = M_CHUNK
    dv = M_HEAD_DIM
    nc = seq // L

    head_row = lax.broadcasted_iota(jnp.int32, (M_HEADS, L), 0) == hd

    def gate_row(base, t0):
        rows = grow_ref[base:base + M_HEADS, pl.ds(t0, L)]
        return jnp.sum(jnp.where(head_row, rows, 0.0), axis=0, keepdims=True)

    U = MLSTM_SCAN_GROUP
    dirs = ((ROW_LI_F, ROW_B_F, L - 1, nf_s, mf_s, 0), (ROW_LI_B, ROW_B_B, 0, nb_s, mb_s, dv))

    def scan_body(gj, carry):
        cis = [[gj * U + u for u in range(U)], [nc - 1 - (gj * U + u) for u in range(U)]]
        t0s = [[pl.multiple_of(ci * L, L) for ci in row] for row in cis]
        lis = [[gate_row(dirs[d][0], t0) for t0 in t0s[d]] for d in range(2)]
        bs = [[gate_row(dirs[d][1], t0) for t0 in t0s[d]] for d in range(2)]
        b_lasts = [[b[:, dirs[d][2]:dirs[d][2] + 1] for b in bs[d]] for d in range(2)]
        gs = [[b_last - b + li for b_last, b, li in zip(b_lasts[d], bs[d], lis[d])] for d in range(2)]
        g_maxs = [[jnp.max(g, axis=-1, keepdims=True) for g in gs[d]] for d in range(2)]
        ks = [[kc_ref[pl.ds(t0, L), :] for t0 in t0s[d]] for d in range(2)]
        vts = [[vt_ref[:, pl.ds(t0, L)].astype(F32) for t0 in t0s[d]] for d in range(2)]

        ms = [[carry[d][2]] for d in range(2)]
        for d in range(2):
            for u in range(U):
                ms[d].append(jnp.maximum(b_lasts[d][u] + ms[d][u], g_maxs[d][u]))
        w_cs = [[jnp.exp(b_lasts[d][u] + ms[d][u] - ms[d][u + 1]) for u in range(U)] for d in range(2)]
        w_ks = [[jnp.exp(gs[d][u] - ms[d][u + 1]) for u in range(U)] for d in range(2)]
        c_upds = [[jnp.dot((vts[d][u] * w_ks[d][u]).astype(BF16), ks[d][u], preferred_element_type=F32)
                   for u in range(U)] for d in range(2)]
        n_upds = [[jnp.dot(jnp.broadcast_to(w_ks[d][u], (SUBLANES, L)).astype(BF16), ks[d][u],
                           preferred_element_type=F32)[0:1, :] for u in range(U)] for d in range(2)]

        out = []
        for d in range(2):
            _, _, _, n_s, m_s, row0 = dirs[d]
            ct_st, n_st, _ = carry[d]
            for u in range(U):
                ci = cis[d][u]
                cct_s[ci, row0:row0 + dv, :] = ct_st.astype(BF16)
                n_s[ci] = jnp.broadcast_to(n_st, (SUBLANES, LANES))
                m_s[ci] = jnp.broadcast_to(ms[d][u], (SUBLANES, LANES))
                ct_st = w_cs[d][u] * ct_st + c_upds[d][u]
                n_st = w_cs[d][u] * n_st + n_upds[d][u]
            out.append((ct_st, n_st, ms[d][U]))
        return tuple(out)

    init = (jnp.zeros((dv, M_HEAD_DIM), F32), jnp.zeros((1, M_HEAD_DIM), F32),
            jnp.full((1, 1), NEG, F32))
    lax.fori_loop(0, nc // U, scan_body, (init, init))

    rr = lax.broadcasted_iota(jnp.int32, (L, L), 0)
    cc = lax.broadcasted_iota(jnp.int32, (L, L), 1)
    row8 = lax.broadcasted_iota(jnp.int32, (SUBLANES, LANES), 0)
    sel_k = lax.broadcasted_iota(jnp.int32, (LANES, 2 * L), 0)
    sel_n = lax.broadcasted_iota(jnp.int32, (LANES, 2 * L), 1)
    sel_lane = jnp.where(sel_n < L, LANE_C_F + hd, LANE_C_B + hd)
    sel = jnp.where(sel_k == sel_lane, 1.0, 0.0).astype(BF16)
    keep_f = rr <= cc
    keep_b = rr >= cc
    G = MLSTM_OUT_GROUP

    def out_body(gi, carry):
        cis = [gi * G + g for g in range(G)]
        t0s = [pl.multiple_of(ci * L, L) for ci in cis]
        qts = [qct_ref[:, pl.ds(t0, L)] for t0 in t0s]
        qk_ts = [jnp.dot(kc_ref[pl.ds(t0, L), :], qt, preferred_element_type=F32)
                 for t0, qt in zip(t0s, qts)]
        cqs = [jnp.dot(cct_s[ci], qt, preferred_element_type=F32) for ci, qt in zip(cis, qts)]
        n8s = [jnp.where(row8 == 0, nf_s[ci], jnp.where(row8 == 1, nb_s[ci], 0.0)).astype(BF16) for ci in cis]
        qns = [jnp.dot(n8, qt, preferred_element_type=F32) for n8, qt in zip(n8s, qts)]
        c_bcs = [_split_dot_right(ccol_ref[pl.ds(t0, L), :], sel) for t0 in t0s]

        items = []
        for g in range(G):
            items.append((g, c_bcs[g][:, :L], qns[g][0:1, :], ROW_B_F, keep_f, mf_s))
            items.append((g, c_bcs[g][:, L:], qns[g][1:2, :], ROW_B_B, keep_b, mb_s))
        b_rs = [gate_row(b_base, t0s[g]) for g, _, _, b_base, _, _ in items]
        m_prevs = [m_s[cis[g]][0:1, 0:1] for g, _, _, _, _, m_s in items]
        ds = [jnp.where(keep, b_r + c_s, NEG)
              for (_, c_s, _, _, keep, _), b_r in zip(items, b_rs)]
        m_inters = [b_r + m_prev for b_r, m_prev in zip(b_rs, m_prevs)]
        m_ts = [jnp.maximum(m_inter, jnp.max(d, axis=0, keepdims=True)) for m_inter, d in zip(m_inters, ds)]
        w_inters = [jnp.exp(m_inter - m_t) for m_inter, m_t in zip(m_inters, m_ts)]
        a_s = [jnp.exp(d - m_t) * qk_ts[it[0]] for d, m_t, it in zip(ds, m_ts, items)]
        dens = [w_inter * it[2] + jnp.sum(a, axis=0, keepdims=True) for w_inter, it, a in zip(w_inters, items, a_s)]
        invs = [1.0 / jnp.maximum(jnp.abs(den), jnp.exp(-m_t)) for den, m_t in zip(dens, m_ts)]
        ps = [a * inv for a, inv in zip(a_s, invs)]
        ws = [w_inter * inv for w_inter, inv in zip(w_inters, invs)]

        h_ts = [jnp.dot(vt_ref[:, pl.ds(t0s[g], L)], (ps[2 * g] + ps[2 * g + 1]).astype(BF16),
                        preferred_element_type=F32)
                + cqs[g][:dv, :] * ws[2 * g] + cqs[g][dv:, :] * ws[2 * g + 1] for g in range(G)]
        hs = [_sigmoid(o_ref[pl.ds(t0, L), :].astype(F32)) * h_t.T for t0, h_t in zip(t0s, h_ts)]
        hs = [h * lax.rsqrt(jnp.mean(h * h, axis=-1, keepdims=True) + NORM_EPS) * ng_ref[...] for h in hs]
        for t0, h in zip(t0s, hs):
            out_ref[pl.ds(t0, L), :] = (h * _silu(z_ref[pl.ds(t0, L), :].astype(F32))).astype(out_ref.dtype)
        return carry

    lax.fori_loop(0, nc // G, out_body, 0)


def _mlstm(proj, qct, kc, vtm, grow, ccol, ng, batch, seq):
    t = proj.shape[0]
    nc = seq // M_CHUNK
    head_rows = lambda b, h: (h, b)
    head_cols = lambda b, h: (b, h)

    def proj_head(col0):
        return pl.BlockSpec((seq, M_HEAD_DIM), lambda b, h: (b, col0 // M_HEAD_DIM + h))

    return pl.pallas_call(
        functools.partial(_mlstm_kernel, seq=seq),
        grid=(batch, M_HEADS),
        in_specs=[
            pl.BlockSpec((M_HEAD_DIM, seq), head_rows),
            pl.BlockSpec((seq, M_HEAD_DIM), head_cols),
            pl.BlockSpec((M_HEAD_DIM, seq), head_rows),
            proj_head(P_MO), proj_head(P_MZ),
            pl.BlockSpec((N_IFG, seq), lambda b, h: (0, b)),
            pl.BlockSpec((seq, LANES), lambda b, h: (b, 0)),
            pl.BlockSpec((1, M_HEAD_DIM), lambda b, h: (0, h)),
        ],
        out_specs=pl.BlockSpec((seq, M_HEAD_DIM), head_cols),
        out_shape=jax.ShapeDtypeStruct((t, M_WIDTH), BF16),
        scratch_shapes=[
            pltpu.VMEM((nc, 2 * M_HEAD_DIM, M_HEAD_DIM), BF16),
            pltpu.VMEM((nc, SUBLANES, M_HEAD_DIM), F32),
            pltpu.VMEM((nc, SUBLANES, M_HEAD_DIM), F32),
            pltpu.VMEM((nc, SUBLANES, LANES), F32),
            pltpu.VMEM((nc, SUBLANES, LANES), F32),
        ],
        compiler_params=pltpu.CompilerParams(
            dimension_semantics=("arbitrary", "arbitrary"), vmem_limit_bytes=VMEM_LIMIT_BYTES),
        name="mlstm",
    )(qct, kc, vtm, proj, proj, grow, ccol, ng)


def _out_proj_kernel(x_ref, a_ref, m_ref, gates_ref, wa_ref, wm_ref, wo_ref, out_ref):
    branch_a = jnp.dot(a_ref[...], wa_ref[...], preferred_element_type=F32)
    branch_m = jnp.dot(m_ref[...], wm_ref[...], preferred_element_type=F32)
    gates = _sigmoid(gates_ref[...].astype(F32))
    merged = gates[:, :D_MODEL] * branch_a + gates[:, D_MODEL:] * branch_m
    out_ref[...] = x_ref[...] + jnp.dot(merged.astype(BF16), wo_ref[...], preferred_element_type=F32)


def _out_proj(x, a, m, proj, wa, wm, wo):
    t = x.shape[0]
    tile = lambda i: (i, 0)
    const = lambda i: (0, 0)
    return pl.pallas_call(
        _out_proj_kernel,
        grid=(t // OUT_TM,),
        in_specs=[
            pl.BlockSpec((OUT_TM, D_MODEL), tile),
            pl.BlockSpec((OUT_TM, ATT_WIDTH), tile),
            pl.BlockSpec((OUT_TM, M_WIDTH), tile),
            pl.BlockSpec((OUT_TM, 2 * D_MODEL), lambda i: (i, P_GATES // (2 * D_MODEL))),
            pl.BlockSpec((ATT_WIDTH, D_MODEL), const),
            pl.BlockSpec((M_WIDTH, D_MODEL), const),
            pl.BlockSpec((D_MODEL, D_MODEL), const),
        ],
        out_specs=pl.BlockSpec((OUT_TM, D_MODEL), tile),
        out_shape=jax.ShapeDtypeStruct((t, D_MODEL), F32),
        compiler_params=pltpu.CompilerParams(
            dimension_semantics=("arbitrary",), vmem_limit_bytes=VMEM_LIMIT_BYTES),
        name="out_proj",
    )(x, a, m, proj, wa, wm, wo)


def _rope_tables(seq):
    inv = jnp.power(jnp.float32(ROPE_THETA), -jnp.arange(ROPE_HALF, dtype=F32) * 2.0 / ROPE_DIM)
    ang = jnp.arange(seq, dtype=F32)[:, None] * inv[None, :]
    cos, sin = jnp.cos(ang), jnp.sin(ang)
    pad = ATT_HEAD_DIM - ROPE_DIM
    ones = jnp.ones((seq, pad), F32)
    zeros = jnp.zeros((seq, pad), F32)
    zh = jnp.zeros((seq, ROPE_HALF), F32)
    c = jnp.concatenate([cos, cos, ones], axis=-1)
    s1 = jnp.concatenate([-sin, zh, zeros], axis=-1)
    s2 = jnp.concatenate([zh, sin, zeros], axis=-1)
    tab = jnp.stack([c, s1, s2])
    return jnp.concatenate([tab, tab], axis=-1)


def _prep_weights(w_in, b_in):
    order = [(_R_GATES, 2 * D_MODEL), (_R_AZ, ATT_WIDTH), (_R_MO, M_WIDTH), (_R_MZ, M_WIDTH),
             (_R_MQ, M_WIDTH), (_R_MK, M_WIDTH), (_R_AQ, ATT_WIDTH), (_R_AK, ATT_KV_WIDTH)]
    w_main = jnp.concatenate([w_in[:, :, o:o + n] for o, n in order], axis=-1).astype(BF16)
    b_main = jnp.concatenate([b_in[:, o:o + n] for o, n in order], axis=-1)[:, None, :]
    v_order = [(_R_MV, M_WIDTH), (_R_AV, ATT_KV_WIDTH)]
    w_vt = jnp.swapaxes(jnp.concatenate([w_in[:, :, o:o + n] for o, n in v_order], axis=-1), 1, 2).astype(BF16)
    b_vt = jnp.concatenate([b_in[:, o:o + n] for o, n in v_order], axis=-1)[:, :, None]
    w_if = w_in[:, :, _R_IF:_R_IF + N_IFG]
    b_if = b_in[:, _R_IF:_R_IF + N_IFG]
    w_if_pad = jnp.pad(w_if, ((0, 0), (0, 0), (0, LANES - N_IFG))).astype(BF16)
    b_if_pad = jnp.pad(b_if, ((0, 0), (0, LANES - N_IFG)))[:, None, :]
    w_if_t = jnp.swapaxes(w_if, 1, 2).astype(BF16)
    b_if_c = b_if[:, :, None]
    return w_main, b_main, w_vt, b_vt, w_if_pad, w_if_t, b_if_pad, b_if_c


def _trunk(x, rope_tab, layers):
    batch, seq, _ = x.shape
    xf = x.reshape(batch * seq, D_MODEL)
    for lw in layers:
        proj, q4, rq, kn, vta, qct, kc, vtm, grow, ccol = _in_proj(xf, lw, rope_tab, seq)
        a = _attention(proj, q4, rq, kn, vta, lw["sink"], batch, seq)
        m = _mlstm(proj, qct, kc, vtm, grow, ccol, lw["m_norm_g"], batch, seq)
        xf = _out_proj(xf, a, m, proj, lw["w_att_out"], lw["w_m_out"], lw["w_out"])
    return xf.reshape(batch, seq, D_MODEL)


def kernel(x_prompt, x_sample, norm_g, w_in, b_in, q_norm_g, k_norm_g, sink, conv_w, m_norm_g,
           w_att_out, w_m_out, w_out):
    w_main, b_main, w_vt, b_vt, w_if, w_if_t, b_if, b_if_c = _prep_weights(w_in, b_in)
    wa, wm, wo = w_att_out.astype(BF16), w_m_out.astype(BF16), w_out.astype(BF16)
    layers = []
    for l in range(DEPTH):
        layers.append(dict(
            norm_g=norm_g[l][None, :], w_main=w_main[l], b_main=b_main[l], w_vt=w_vt[l], b_vt=b_vt[l],
            w_if=w_if[l], w_if_t=w_if_t[l], b_if=b_if[l], b_if_c=b_if_c[l],
            gq=jnp.tile(q_norm_g[l], 2)[None, :], gk=jnp.tile(k_norm_g[l], 2)[None, :],
            sink=sink[l][None, :], conv_w=conv_w[l], m_norm_g=m_norm_g[l][None, :],
            w_att_out=wa[l], w_m_out=wm[l], w_out=wo[l]))
    outs = []
    for x in (x_prompt, x_sample):
        outs.append(_trunk(x, _rope_tables(x.shape[1]), layers))
    return tuple(outs)
```

```python
import functools
import math

import jax
import jax.numpy as jnp
from jax import lax
from jax.experimental import pallas as pl
from jax.experimental.pallas import tpu as pltpu

F32 = jnp.float32
BF16 = jnp.bfloat16

D_MODEL = 1024
DEPTH = 4
ATT_HEADS = 16
ATT_KV_HEADS = 4
ATT_GROUP = ATT_HEADS // ATT_KV_HEADS
ATT_HEAD_DIM = 64
ATT_WIDTH = ATT_HEADS * ATT_HEAD_DIM
ATT_KV_WIDTH = ATT_KV_HEADS * ATT_HEAD_DIM
WINDOW = 128
ATT_BLOCK = 128
ROPE_THETA = 500000.0
ROPE_DIM = ATT_HEAD_DIM // 4
ROPE_HALF = ROPE_DIM // 2
M_HEADS = 8
M_HEAD_DIM = 128
M_WIDTH = M_HEADS * M_HEAD_DIM
M_CHUNK = 128
CONV_K = 3
NORM_EPS = 1e-6
NEG = -1e30
LOG2E = math.log2(math.e)

LANES = 128
SUBLANES = 8
VMEM_LIMIT_BYTES = 56 * 1024 * 1024
_REF_SPLITS = (ATT_WIDTH, ATT_KV_WIDTH, ATT_KV_WIDTH, ATT_WIDTH,
               M_WIDTH, M_WIDTH, M_WIDTH, M_WIDTH, M_WIDTH,
               M_HEADS, M_HEADS, M_HEADS, M_HEADS, 2 * D_MODEL)
_REF_OFF = [0]
for _w in _REF_SPLITS:
    _REF_OFF.append(_REF_OFF[-1] + _w)
(_R_AQ, _R_AK, _R_AV, _R_AZ, _R_MQ, _R_MK, _R_MV, _R_MO, _R_MZ,
 _R_IF, _R_FF, _R_IB, _R_FB, _R_GATES) = _REF_OFF[:-1]

P_GATES = 0
P_AZ = P_GATES + 2 * D_MODEL
P_MO = P_AZ + ATT_WIDTH
P_MZ = P_MO + M_WIDTH
P_WIDTH = P_MZ + M_WIDTH
W_MQ = P_WIDTH
W_MK = W_MQ + M_WIDTH
W_AQ = W_MK + M_WIDTH
W_AK = W_AQ + ATT_WIDTH
W_WIDTH = W_AK + ATT_KV_WIDTH
VT_ROWS = M_WIDTH + ATT_KV_WIDTH
N_IFG = 4 * M_HEADS

ROW_LI_F, ROW_B_F, ROW_LI_B, ROW_B_B = 0, M_HEADS, 2 * M_HEADS, 3 * M_HEADS
LANE_C_F, LANE_C_B = 0, 2 * M_HEADS

PROJ_DTYPE = BF16
IN_TM = 512
IN_TN = 512
IN_HALO = 16
IN_PIECE = 256
VT_TN = 256
OUT_TM = 512
ATT_TQ = 512
ATT_NSB = ATT_TQ // ATT_BLOCK
ATT_PAIRS = ATT_WIDTH // LANES
MLSTM_SCAN_GROUP = 8
MLSTM_OUT_GROUP = 8

_NT = (((1,), (1,)), ((), ()))


def _sigmoid(x):
    return 1.0 / (1.0 + jnp.exp(-x))


def _silu(x):
    return x * _sigmoid(x)


def _log_sigmoid(x):
    return jnp.minimum(x, 0.0) - jnp.log1p(jnp.exp(-jnp.abs(x)))


def _bf16_terms(x):
    hi = x.astype(BF16)
    r1 = x - hi.astype(F32)
    mid = r1.astype(BF16)
    lo = (r1 - mid.astype(F32)).astype(BF16)
    return hi, mid, lo


def _split_dot_left(sel, x):
    return sum(jnp.dot(sel, t, preferred_element_type=F32) for t in _bf16_terms(x))


def _split_dot_right(x, sel):
    return sum(jnp.dot(t, sel, preferred_element_type=F32) for t in _bf16_terms(x))


def _split_dot_nt(sel, x):
    return sum(lax.dot_general(sel, t, _NT, preferred_element_type=F32) for t in _bf16_terms(x))


def _rope(y, tab):
    return (y * tab[0]
            + pltpu.roll(y, LANES - ROPE_HALF, 1) * tab[1]
            + pltpu.roll(y, ROPE_HALF, 1) * tab[2])


def _norm_rope_tile(x, gain, tab):
    lane = lax.broadcasted_iota(jnp.int32, x.shape, 1)
    left = lane < ATT_HEAD_DIM
    sq = x * x
    ss_l = jnp.sum(jnp.where(left, sq, 0.0), axis=-1, keepdims=True)
    ss_r = jnp.sum(jnp.where(left, 0.0, sq), axis=-1, keepdims=True)
    inv = jnp.where(left, lax.rsqrt(ss_l / ATT_HEAD_DIM + NORM_EPS),
                    lax.rsqrt(ss_r / ATT_HEAD_DIM + NORM_EPS))
    return _rope(x * inv * gain, tab)


def _rmsnorm_bf16(x, g):
    return (x * lax.rsqrt(jnp.mean(x * x, axis=-1, keepdims=True) + NORM_EPS) * g).astype(BF16)


def _in_proj_kernel(x_ref, xprev_ref, xnext_ref, g_ref, w_ref, b_ref, wvt_ref, bvt_ref,
                    wif_ref, wift_ref, bif_ref, bifc_ref, cw_ref, tab_ref, gq_ref, gk_ref,
                    proj_ref, q4_ref, rq_ref, kn_ref, vta_ref, qct_ref, kc_ref, vtm_ref, grow_ref, ccol_ref,
                    xn_s, *, tiles_per_seq):
    tm = x_ref.shape[0]
    tile_in_seq = pl.program_id(0) % tiles_per_seq

    xn_s[0:IN_HALO, :] = _rmsnorm_bf16(xprev_ref[...], g_ref[...])
    xn_s[IN_HALO:IN_HALO + tm, :] = _rmsnorm_bf16(x_ref[...], g_ref[...])
    xn_s[IN_HALO + tm:, :] = _rmsnorm_bf16(xnext_ref[...], g_ref[...])
    xn = xn_s[IN_HALO:IN_HALO + tm, :]
    xn_halo = xn_s[...]

    def chunk(c0, width, lhs=None):
        lhs = xn if lhs is None else lhs

        def piece(p0):
            cols = slice(p0, p0 + IN_PIECE)
            return lambda: jnp.dot(lhs, w_ref[:, cols], preferred_element_type=F32) + b_ref[:, cols]
        return [piece(p0) for p0 in range(c0, c0 + width, IN_PIECE)]

    stages = []

    def store_plain(c0):
        def store(acc, p0):
            def run():
                proj_ref[:, p0:p0 + IN_PIECE] = acc.astype(proj_ref.dtype)
            return run
        return lambda accs: [store(acc, c0 + i * IN_PIECE) for i, acc in enumerate(accs)]

    plain = [(chunk(c0, IN_TN), store_plain(c0)) for c0 in range(0, P_WIDTH, IN_TN)]

    has_before = tile_in_seq > 0
    has_after = tile_in_seq < tiles_per_seq - 1
    row_id = lax.broadcasted_iota(jnp.int32, (tm, LANES), 0)
    kscale = M_HEAD_DIM ** -0.5
    lane_tiles = range(0, IN_PIECE, LANES)

    def conv_silu(c0):
        def conv(acc, p0, l0):
            cols = slice(p0 + l0, p0 + l0 + LANES)

            def run():
                a = acc[IN_HALO:IN_HALO + tm, l0:l0 + LANES]
                before = jnp.where(has_before, acc[IN_HALO - 1:IN_HALO, l0:l0 + LANES], 0.0)
                after = jnp.where(has_after, acc[IN_HALO + tm:IN_HALO + tm + 1, l0:l0 + LANES], 0.0)
                xm = jnp.where(row_id == 0, before, pltpu.roll(a, 1, 0))
                xp = jnp.where(row_id == tm - 1, after, pltpu.roll(a, tm - 1, 0))
                y = _silu(xm * cw_ref[0:1, cols] + a * cw_ref[1:2, cols] + xp * cw_ref[2:3, cols])
                if p0 < M_WIDTH:
                    qct_ref[cols, :] = y.T.astype(BF16)
                else:
                    kc_ref[:, p0 + l0 - M_WIDTH:p0 + l0 - M_WIDTH + LANES] = (y * kscale).astype(BF16)
            return run
        return lambda accs: [conv(acc, c0 + i * IN_PIECE, l0) for i, acc in enumerate(accs) for l0 in lane_tiles]

    conv = [(chunk(W_MQ + c0, IN_TN, lhs=xn_halo), conv_silu(c0)) for c0 in range(0, 2 * M_WIDTH, IN_TN)]

    tab = tab_ref[...]
    heads_per_tile = LANES // ATT_HEAD_DIM
    head_of_col = lax.broadcasted_iota(jnp.int32, (SUBLANES, LANES), 1) >> (ATT_HEAD_DIM.bit_length() - 1)
    pick = jnp.where(head_of_col == lax.broadcasted_iota(jnp.int32, (SUBLANES, LANES), 0),
                     1.0, 0.0).astype(BF16)

    def q_rope(c0):
        def rope(acc, p0, l0):
            def run():
                a = acc[:, l0:l0 + LANES]
                head0 = (p0 + l0) // ATT_HEAD_DIM
                ssq = _split_dot_nt(pick, a * a)[:heads_per_tile, :]
                rq_ref[head0:head0 + heads_per_tile, :] = (
                    lax.rsqrt(ssq / ATT_HEAD_DIM + NORM_EPS) * (ATT_HEAD_DIM ** -0.5 * LOG2E))
                y = _rope(a * gq_ref[...], tab).astype(BF16)
                for j in range(tm // ATT_BLOCK):
                    q4_ref[j, (p0 + l0) // LANES] = y[j * ATT_BLOCK:(j + 1) * ATT_BLOCK, :]
            return run
        return lambda accs: [rope(acc, c0 + i * IN_PIECE, l0) for i, acc in enumerate(accs) for l0 in lane_tiles]

    att_q = [(chunk(W_AQ + c0, IN_TN), q_rope(c0)) for c0 in range(0, ATT_WIDTH, IN_TN)]

    def k_norm_rope(accs):
        def norm_rope(acc, p0):
            def run():
                for t in range(IN_PIECE // LANES):
                    cols = slice(t * LANES, (t + 1) * LANES)
                    kn_ref[:, p0 + t * LANES:p0 + (t + 1) * LANES] = _norm_rope_tile(
                        acc[:, cols], gk_ref[...], tab).astype(BF16)
            return run
        return [norm_rope(acc, i * IN_PIECE) for i, acc in enumerate(accs)]

    att_k = (chunk(W_AK, ATT_KV_WIDTH), k_norm_rope)

    def v_t(r0):
        rows = slice(r0, r0 + VT_TN)

        def matmul():
            return lax.dot_general(wvt_ref[rows, :], xn, _NT, preferred_element_type=F32) + bvt_ref[rows, :]

        def store(vts_):
            def run():
                if r0 < M_WIDTH:
                    vtm_ref[rows, :] = vts_[0].astype(BF16)
                else:
                    vta_ref[r0 - M_WIDTH:r0 - M_WIDTH + VT_TN, :] = vts_[0].astype(BF16)
            return [run]
        return [matmul], store

    vts = [v_t(r0) for r0 in range(0, VT_ROWS, VT_TN)]

    def gate_matmul():
        gc = jnp.dot(xn, wif_ref[...], preferred_element_type=F32) + bif_ref[...]
        gr = lax.dot_general(wift_ref[...], xn, _NT, preferred_element_type=F32) + bifc_ref[...]
        return gc, gr

    def gate_epilogue(gates):
        gc, gr = gates[0]
        ls_c = _log_sigmoid(gc)
        ls_r = _log_sigmoid(gr)
        r = lax.broadcasted_iota(jnp.int32, (M_CHUNK, M_CHUNK), 0)
        c = lax.broadcasted_iota(jnp.int32, (M_CHUNK, M_CHUNK), 1)
        tril = jnp.where(c <= r, 1.0, 0.0).astype(BF16)
        triu = jnp.where(c >= r, 1.0, 0.0).astype(BF16)
        lane = lax.broadcasted_iota(jnp.int32, (M_CHUNK, LANES), 1)
        fwd_lanes = lane < 2 * M_HEADS
        grow_ref[ROW_LI_F:ROW_LI_F + M_HEADS, :] = gr[0:M_HEADS, :]
        grow_ref[ROW_LI_B:ROW_LI_B + M_HEADS, :] = gr[2 * M_HEADS:3 * M_HEADS, :]
        for ci in range(tm // M_CHUNK):
            rows = slice(ci * M_CHUNK, (ci + 1) * M_CHUNK)
            lsc = ls_c[rows, :]
            b_f = _split_dot_left(tril, lsc)
            b_b = _split_dot_left(triu, lsc)
            cc = gc[rows, :] - pltpu.roll(jnp.where(fwd_lanes, b_f, b_b), LANES - M_HEADS, 1)
            for ti, term in enumerate(_bf16_terms(cc)):
                ccol_ref[ti, rows, :] = term
            grow_ref[ROW_B_F:ROW_B_F + M_HEADS, rows] = _split_dot_right(ls_r[M_HEADS:2 * M_HEADS, rows], triu)
            grow_ref[ROW_B_B:ROW_B_B + M_HEADS, rows] = _split_dot_right(ls_r[3 * M_HEADS:4 * M_HEADS, rows], tril)

    light = plain + vts
    heavy = [([gate_matmul], lambda gates: [lambda: gate_epilogue(gates)])] + conv + att_q + [att_k]
    per_heavy = len(light) // len(heavy)
    for i, stage in enumerate(heavy):
        stages.append(stage)
        stages.extend(light[i * per_heavy:(i + 1) * per_heavy])
    stages.extend(light[len(heavy) * per_heavy:])

    n_matmuls = sum(len(matmuls) for matmuls, _ in stages)
    queue = []
    emitted = 0
    for matmuls, make_epilogues in stages:
        results = []
        for matmul in matmuls:
            results.append(matmul())
            emitted += 1
            take = -(-len(queue) // max(n_matmuls - emitted, 1)) if queue else 0
            for run in queue[:take]:
                run()
            queue = queue[take:]
        queue.extend(make_epilogues(results))
    for run in queue:
        run()


def _in_proj(x, lw, rope_tab, seq):
    t = x.shape[0]
    tps = seq // IN_TM
    halo_blocks = IN_TM // IN_HALO
    const = lambda i: (0, 0)
    tile = lambda i: (i, 0)
    cols = lambda i: (0, i)
    resident = dict(pipeline_mode=pl.Buffered(1))
    return pl.pallas_call(
        functools.partial(_in_proj_kernel, tiles_per_seq=tps),
        grid=(t // IN_TM,),
        in_specs=[
            pl.BlockSpec((IN_TM, D_MODEL), tile),
            pl.BlockSpec((IN_HALO, D_MODEL), lambda i: (jnp.maximum(i * halo_blocks - 1, 0), 0)),
            pl.BlockSpec((IN_HALO, D_MODEL), lambda i: (jnp.minimum((i + 1) * halo_blocks, t // IN_HALO - 1), 0)),
            pl.BlockSpec((1, D_MODEL), const),
            pl.BlockSpec((D_MODEL, W_WIDTH), const, **resident),
            pl.BlockSpec((1, W_WIDTH), const),
            pl.BlockSpec((VT_ROWS, D_MODEL), const, **resident),
            pl.BlockSpec((VT_ROWS, 1), const),
            pl.BlockSpec((D_MODEL, LANES), const),
            pl.BlockSpec((N_IFG, D_MODEL), const),
            pl.BlockSpec((1, LANES), const),
            pl.BlockSpec((N_IFG, 1), const),
            pl.BlockSpec((CONV_K, 2 * M_WIDTH), const),
            pl.BlockSpec((3, IN_TM, LANES), lambda i: (0, i % tps, 0)),
            pl.BlockSpec((1, LANES), const),
            pl.BlockSpec((1, LANES), const),
        ],
        out_specs=[
            pl.BlockSpec((IN_TM, P_WIDTH), tile),
            pl.BlockSpec((IN_TM // ATT_BLOCK, ATT_PAIRS, ATT_BLOCK, LANES), lambda i: (i, 0, 0, 0)),
            pl.BlockSpec((ATT_HEADS, IN_TM), cols),
            pl.BlockSpec((IN_TM, ATT_KV_WIDTH), tile),
            pl.BlockSpec((ATT_KV_WIDTH, IN_TM), cols),
            pl.BlockSpec((M_WIDTH, IN_TM), cols),
            pl.BlockSpec((IN_TM, M_WIDTH), tile),
            pl.BlockSpec((M_WIDTH, IN_TM), cols),
            pl.BlockSpec((N_IFG, IN_TM), cols),
            pl.BlockSpec((3, IN_TM, LANES), lambda i: (0, i, 0)),
        ],
        out_shape=[
            jax.ShapeDtypeStruct((t, P_WIDTH), PROJ_DTYPE),
            jax.ShapeDtypeStruct((t // ATT_BLOCK, ATT_PAIRS, ATT_BLOCK, LANES), BF16),
            jax.ShapeDtypeStruct((ATT_HEADS, t), F32),
            jax.ShapeDtypeStruct((t, ATT_KV_WIDTH), BF16),
            jax.ShapeDtypeStruct((ATT_KV_WIDTH, t), BF16),
            jax.ShapeDtypeStruct((M_WIDTH, t), BF16),
            jax.ShapeDtypeStruct((t, M_WIDTH), BF16),
            jax.ShapeDtypeStruct((M_WIDTH, t), BF16),
            jax.ShapeDtypeStruct((N_IFG, t), F32),
            jax.ShapeDtypeStruct((3, t, LANES), BF16),
        ],
        scratch_shapes=[pltpu.VMEM((IN_TM + 2 * IN_HALO, D_MODEL), BF16)],
        compiler_params=pltpu.CompilerParams(
            dimension_semantics=("arbitrary",), vmem_limit_bytes=VMEM_LIMIT_BYTES),
        name="in_proj",
    )(x, x, x, lw["norm_g"], lw["w_main"], lw["b_main"], lw["w_vt"], lw["b_vt"], lw["w_if"], lw["w_if_t"],
      lw["b_if"], lw["b_if_c"], lw["conv_w"], rope_tab, lw["gq"], lw["gk"])


def _attention_kernel(q4_ref, rq_ref, z_ref, kp_ref, kc_ref, kn_ref, vp_ref, vc_ref, vn_ref, sink_ref, out_ref,
                      ks, vts, out_t, *, n_blocks):
    ti = pl.program_id(1)
    bq = ATT_BLOCK

    r0 = 0
    for k_ref, v_ref in ((kp_ref, vp_ref), (kc_ref, vc_ref), (kn_ref, vn_ref)):
        nr = k_ref.shape[0]
        left = lax.broadcasted_iota(jnp.int32, (nr, LANES), 1) < ATT_HEAD_DIM
        for t in range(ATT_KV_WIDTH // LANES):
            kt = k_ref[:, t * LANES:(t + 1) * LANES].astype(F32)
            sw = pltpu.roll(kt, ATT_HEAD_DIM, 1)
            ks[2 * t, 0, r0:r0 + nr, :] = jnp.where(left, kt, 0.0).astype(BF16)
            ks[2 * t, 1, r0:r0 + nr, :] = jnp.where(left, 0.0, sw).astype(BF16)
            ks[2 * t + 1, 0, r0:r0 + nr, :] = jnp.where(left, sw, 0.0).astype(BF16)
            ks[2 * t + 1, 1, r0:r0 + nr, :] = jnp.where(left, 0.0, kt).astype(BF16)
        vts[:, r0:r0 + nr] = v_ref[...]
        r0 += nr

    def group_heads(h):
        return (ATT_GROUP * h, ATT_GROUP * h + 2, ATT_GROUP * h + 1, ATT_GROUP * h + 3)

    sink_rows = [jnp.concatenate([jnp.broadcast_to(sink_ref[:, c:c + 1] * LOG2E, (1, bq)) for c in group_heads(h)],
                                 axis=1) for h in range(ATT_KV_HEADS)]

    kk = lax.broadcasted_iota(jnp.int32, (bq, bq), 0)
    qq = lax.broadcasted_iota(jnp.int32, (bq, bq), 1)
    prev_in_band = kk >= qq
    next_in_band = kk <= qq

    def sub_block(j, carry):
        blk = ti * ATT_NSB + j
        bias_prev = jnp.where(prev_in_band, jnp.where(blk > 0, 0.0, NEG), NEG).astype(F32)
        bias_next = jnp.where(next_in_band, jnp.where(blk < n_blocks - 1, 0.0, NEG), NEG).astype(F32)
        bias_prev = jnp.concatenate([bias_prev] * ATT_GROUP, axis=1)
        bias_next = jnp.concatenate([bias_next] * ATT_GROUP, axis=1)
        row0 = pl.multiple_of(j * bq, bq)
        q_slabs = q4_ref[j]
        hs = range(ATT_KV_HEADS)
        q_pairs = [q_slabs[2 * h:2 * h + 2].reshape(2 * bq, LANES) for h in hs]
        ss = [jnp.concatenate(
            [lax.dot_general(ks[h, v, pl.ds(row0, 3 * bq), :], q_pairs[h], _NT, preferred_element_type=F32)
             for v in range(2)], axis=1) for h in hs]
        rq_rows = [jnp.concatenate([rq_ref[c:c + 1, pl.ds(row0, bq)] for c in group_heads(h)], axis=1) for h in hs]
        ss = [s * rq_row for s, rq_row in zip(ss, rq_rows)]
        ss = [jnp.concatenate([s[:bq] + bias_prev, s[bq:2 * bq], s[2 * bq:] + bias_next], axis=0) for s in ss]
        ms = [jnp.maximum(jnp.max(s, axis=0, keepdims=True), sink_rows[h]) for h, s in zip(hs, ss)]
        ps = [jnp.exp2(s - m) for s, m in zip(ss, ms)]
        denoms = [jnp.sum(p, axis=0, keepdims=True) + jnp.exp2(sink_rows[h] - m) for h, p, m in zip(hs, ps, ms)]
        os_ = [jnp.dot(vts[h * ATT_HEAD_DIM:(h + 1) * ATT_HEAD_DIM, pl.ds(row0, 3 * bq)], p.astype(BF16),
                       preferred_element_type=F32) * (1.0 / denom)
               for h, p, denom in zip(hs, ps, denoms)]
        for h, o in zip(hs, os_):
            for i, c in enumerate(group_heads(h)):
                out_t[c * ATT_HEAD_DIM:(c + 1) * ATT_HEAD_DIM, :] = o[:, i * bq:(i + 1) * bq]
        att = out_t[...].T
        z = z_ref[pl.ds(row0, bq), :].astype(F32)
        out_ref[pl.ds(row0, bq), :] = (att * _silu(z)).astype(out_ref.dtype)
        return carry

    lax.fori_loop(0, ATT_NSB, sub_block, 0)


def _attention(proj, q4, rq, kn, vta, sink, batch, seq):
    t = proj.shape[0]
    nb = seq // ATT_BLOCK
    nt = seq // ATT_TQ
    bq = ATT_BLOCK

    prev_blk = lambda b, i: b * nb + jnp.maximum(i * ATT_NSB - 1, 0)
    next_blk = lambda b, i: b * nb + jnp.minimum((i + 1) * ATT_NSB, nb - 1)
    return pl.pallas_call(
        functools.partial(_attention_kernel, n_blocks=nb),
        grid=(batch, nt),
        in_specs=[
            pl.BlockSpec((ATT_NSB, ATT_PAIRS, bq, LANES), lambda b, i: (b * nt + i, 0, 0, 0)),
            pl.BlockSpec((ATT_HEADS, ATT_TQ), lambda b, i: (0, b * nt + i)),
            pl.BlockSpec((ATT_TQ, ATT_WIDTH), lambda b, i: (b * nt + i, P_AZ // ATT_WIDTH)),
            pl.BlockSpec((bq, ATT_KV_WIDTH), lambda b, i: (prev_blk(b, i), 0)),
            pl.BlockSpec((ATT_TQ, ATT_KV_WIDTH), lambda b, i: (b * nt + i, 0)),
            pl.BlockSpec((bq, ATT_KV_WIDTH), lambda b, i: (next_blk(b, i), 0)),
            pl.BlockSpec((ATT_KV_WIDTH, bq), lambda b, i: (0, prev_blk(b, i))),
            pl.BlockSpec((ATT_KV_WIDTH, ATT_TQ), lambda b, i: (0, b * nt + i)),
            pl.BlockSpec((ATT_KV_WIDTH, bq), lambda b, i: (0, next_blk(b, i))),
            pl.BlockSpec((1, ATT_HEADS), lambda b, i: (0, 0)),
        ],
        out_specs=pl.BlockSpec((ATT_TQ, ATT_WIDTH), lambda b, i: (b * nt + i, 0)),
        out_shape=jax.ShapeDtypeStruct((t, ATT_WIDTH), BF16),
        scratch_shapes=[
            pltpu.VMEM((ATT_KV_HEADS, 2, ATT_TQ + 2 * bq, LANES), BF16),
            pltpu.VMEM((ATT_KV_WIDTH, ATT_TQ + 2 * bq), BF16),
            pltpu.VMEM((ATT_WIDTH, bq), F32),
        ],
        compiler_params=pltpu.CompilerParams(
            dimension_semantics=("arbitrary", "arbitrary"), vmem_limit_bytes=VMEM_LIMIT_BYTES),
        name="attention",
    )(q4, rq, proj, kn, kn, kn, vta, vta, vta, sink)


def _mlstm_kernel(qct_ref, kc_ref, vt_ref, o_ref, z_ref, grow_ref, ccol_ref, ng_ref,
                  out_ref, cct_s, nf_s, nb_s, mf_s, mb_s, *, seq):
    hd = pl.program_id(1)
    L = M_CHUNK
    dv = M_HEAD_DIM
    nc = seq // L

    head_row = lax.broadcasted_iota(jnp.int32, (M_HEADS, L), 0) == hd

    def gate_row(base, t0):
        rows = grow_ref[base:base + M_HEADS, pl.ds(t0, L)]
        return jnp.sum(jnp.where(head_row, rows, 0.0), axis=0, keepdims=True)

    U = MLSTM_SCAN_GROUP
    dirs = ((ROW_LI_F, ROW_B_F, L - 1, nf_s, mf_s, 0), (ROW_LI_B, ROW_B_B, 0, nb_s, mb_s, dv))

    def scan_body(gj, carry):
        cis = [[gj * U + u for u in range(U)], [nc - 1 - (gj * U + u) for u in range(U)]]
        t0s = [[pl.multiple_of(ci * L, L) for ci in row] for row in cis]
        lis = [[gate_row(dirs[d][0], t0) for t0 in t0s[d]] for d in range(2)]
        bs = [[gate_row(dirs[d][1], t0) for t0 in t0s[d]] for d in range(2)]
        b_lasts = [[b[:, dirs[d][2]:dirs[d][2] + 1] for b in bs[d]] for d in range(2)]
        gs = [[b_last - b + li for b_last, b, li in zip(b_lasts[d], bs[d], lis[d])] for d in range(2)]
        g_maxs = [[jnp.max(g, axis=-1, keepdims=True) for g in gs[d]] for d in range(2)]
        ks = [[kc_ref[pl.ds(t0, L), :] for t0 in t0s[d]] for d in range(2)]
        vts = [[vt_ref[:, pl.ds(t0, L)].astype(F32) for t0 in t0s[d]] for d in range(2)]

        ms = [[carry[d][2]] for d in range(2)]
        for d in range(2):
            for u in range(U):
                ms[d].append(jnp.maximum(b_lasts[d][u] + ms[d][u], g_maxs[d][u]))
        w_cs = [[jnp.exp(b_lasts[d][u] + ms[d][u] - ms[d][u + 1]) for u in range(U)] for d in range(2)]
        w_ks = [[jnp.exp(gs[d][u] - ms[d][u + 1]) for u in range(U)] for d in range(2)]
        c_upds = [[jnp.dot((vts[d][u] * w_ks[d][u]).astype(BF16), ks[d][u], preferred_element_type=F32)
                   for u in range(U)] for d in range(2)]
        n_upds = [[jnp.dot(jnp.broadcast_to(w_ks[d][u], (SUBLANES, L)).astype(BF16), ks[d][u],
                           preferred_element_type=F32)[0:1, :] for u in range(U)] for d in range(2)]

        out = []
        for d in range(2):
            _, _, _, n_s, m_s, row0 = dirs[d]
            ct_st, n_st, _ = carry[d]
            for u in range(U):
                ci = cis[d][u]
                cct_s[ci, row0:row0 + dv, :] = ct_st.astype(BF16)
                n_s[ci] = jnp.broadcast_to(n_st, (SUBLANES, LANES))
                m_s[ci] = jnp.broadcast_to(ms[d][u], (SUBLANES, LANES))
                ct_st = w_cs[d][u] * ct_st + c_upds[d][u]
                n_st = w_cs[d][u] * n_st + n_upds[d][u]
            out.append((ct_st, n_st, ms[d][U]))
        return tuple(out)

    init = (jnp.zeros((dv, M_HEAD_DIM), F32), jnp.zeros((1, M_HEAD_DIM), F32),
            jnp.full((1, 1), NEG, F32))
    lax.fori_loop(0, nc // U, scan_body, (init, init))

    rr = lax.broadcasted_iota(jnp.int32, (L, L), 0)
    cc = lax.broadcasted_iota(jnp.int32, (L, L), 1)
    row8 = lax.broadcasted_iota(jnp.int32, (SUBLANES, LANES), 0)
    sel_k = lax.broadcasted_iota(jnp.int32, (LANES, 2 * L), 0)
    sel_n = lax.broadcasted_iota(jnp.int32, (LANES, 2 * L), 1)
    sel_lane = jnp.where(sel_n < L, LANE_C_F + hd, LANE_C_B + hd)
    sel = jnp.where(sel_k == sel_lane, 1.0, 0.0).astype(BF16)
    keep_f = rr <= cc
    keep_b = rr >= cc
    G = MLSTM_OUT_GROUP

    def out_body(gi, carry):
        cis = [gi * G + g for g in range(G)]
        t0s = [pl.multiple_of(ci * L, L) for ci in cis]
        qts = [qct_ref[:, pl.ds(t0, L)] for t0 in t0s]
        qk_ts = [jnp.dot(kc_ref[pl.ds(t0, L), :], qt, preferred_element_type=F32)
                 for t0, qt in zip(t0s, qts)]
        cqs = [jnp.dot(cct_s[ci], qt, preferred_element_type=F32) for ci, qt in zip(cis, qts)]
        n8s = [jnp.where(row8 == 0, nf_s[ci], jnp.where(row8 == 1, nb_s[ci], 0.0)).astype(BF16) for ci in cis]
        qns = [jnp.dot(n8, qt, preferred_element_type=F32) for n8, qt in zip(n8s, qts)]
        c_bcs = [sum(jnp.dot(ccol_ref[ti, pl.ds(t0, L), :], sel, preferred_element_type=F32) for ti in range(3))
                 for t0 in t0s]

        items = []
        for g in range(G):
            items.append((g, c_bcs[g][:, :L], qns[g][0:1, :], ROW_B_F, keep_f, mf_s))
            items.append((g, c_bcs[g][:, L:], qns[g][1:2, :], ROW_B_B, keep_b, mb_s))
        b_rs = [gate_row(b_base, t0s[g]) for g, _, _, b_base, _, _ in items]
        m_prevs = [m_s[cis[g]][0:1, 0:1] for g, _, _, _, _, m_s in items]
        ds = [jnp.where(keep, b_r + c_s, NEG)
              for (_, c_s, _, _, keep, _), b_r in zip(items, b_rs)]
        m_inters = [b_r + m_prev for b_r, m_prev in zip(b_rs, m_prevs)]
        m_ts = [jnp.maximum(m_inter, jnp.max(d, axis=0, keepdims=True)) for m_inter, d in zip(m_inters, ds)]
        w_inters = [jnp.exp(m_inter - m_t) for m_inter, m_t in zip(m_inters, m_ts)]
        a_s = [jnp.exp(d - m_t) * qk_ts[it[0]] for d, m_t, it in zip(ds, m_ts, items)]
        dens = [w_inter * it[2] + jnp.sum(a, axis=0, keepdims=True) for w_inter, it, a in zip(w_inters, items, a_s)]
        invs = [1.0 / jnp.maximum(jnp.abs(den), jnp.exp(-m_t)) for den, m_t in zip(dens, m_ts)]
        ps = [a * inv for a, inv in zip(a_s, invs)]
        ws = [w_inter * inv for w_inter, inv in zip(w_inters, invs)]

        h_ts = [jnp.dot(vt_ref[:, pl.ds(t0s[g], L)], (ps[2 * g] + ps[2 * g + 1]).astype(BF16),
                        preferred_element_type=F32)
                + cqs[g][:dv, :] * ws[2 * g] + cqs[g][dv:, :] * ws[2 * g + 1] for g in range(G)]
        hs = [_sigmoid(o_ref[pl.ds(t0, L), :].astype(F32)) * h_t.T for t0, h_t in zip(t0s, h_ts)]
        hs = [h * lax.rsqrt(jnp.mean(h * h, axis=-1, keepdims=True) + NORM_EPS) * ng_ref[...] for h in hs]
        for t0, h in zip(t0s, hs):
            out_ref[pl.ds(t0, L), :] = (h * _silu(z_ref[pl.ds(t0, L), :].astype(F32))).astype(out_ref.dtype)
        return carry

    lax.fori_loop(0, nc // G, out_body, 0)


def _mlstm(proj, qct, kc, vtm, grow, ccol, ng, batch, seq):
    t = proj.shape[0]
    nc = seq // M_CHUNK
    head_rows = lambda b, h: (h, b)
    head_cols = lambda b, h: (b, h)

    def proj_head(col0):
        return pl.BlockSpec((seq, M_HEAD_DIM), lambda b, h: (b, col0 // M_HEAD_DIM + h))

    return pl.pallas_call(
        functools.partial(_mlstm_kernel, seq=seq),
        grid=(batch, M_HEADS),
        in_specs=[
            pl.BlockSpec((M_HEAD_DIM, seq), head_rows),
            pl.BlockSpec((seq, M_HEAD_DIM), head_cols),
            pl.BlockSpec((M_HEAD_DIM, seq), head_rows),
            proj_head(P_MO), proj_head(P_MZ),
            pl.BlockSpec((N_IFG, seq), lambda b, h: (0, b)),
            pl.BlockSpec((3, seq, LANES), lambda b, h: (0, b, 0)),
            pl.BlockSpec((1, M_HEAD_DIM), lambda b, h: (0, h)),
        ],
        out_specs=pl.BlockSpec((seq, M_HEAD_DIM), head_cols),
        out_shape=jax.ShapeDtypeStruct((t, M_WIDTH), BF16),
        scratch_shapes=[
            pltpu.VMEM((nc, 2 * M_HEAD_DIM, M_HEAD_DIM), BF16),
            pltpu.VMEM((nc, SUBLANES, M_HEAD_DIM), F32),
            pltpu.VMEM((nc, SUBLANES, M_HEAD_DIM), F32),
            pltpu.VMEM((nc, SUBLANES, LANES), F32),
            pltpu.VMEM((nc, SUBLANES, LANES), F32),
        ],
        compiler_params=pltpu.CompilerParams(
            dimension_semantics=("arbitrary", "arbitrary"), vmem_limit_bytes=VMEM_LIMIT_BYTES),
        name="mlstm",
    )(qct, kc, vtm, proj, proj, grow, ccol, ng)


def _out_proj_kernel(x_ref, a_ref, m_ref, gates_ref, wa_ref, wm_ref, wo_ref, out_ref):
    branch_a = jnp.dot(a_ref[...], wa_ref[...], preferred_element_type=F32)
    branch_m = jnp.dot(m_ref[...], wm_ref[...], preferred_element_type=F32)
    gates = _sigmoid(gates_ref[...].astype(F32))
    merged = gates[:, :D_MODEL] * branch_a + gates[:, D_MODEL:] * branch_m
    out_ref[...] = x_ref[...] + jnp.dot(merged.astype(BF16), wo_ref[...], preferred_element_type=F32)


def _out_proj(x, a, m, proj, wa, wm, wo):
    t = x.shape[0]
    tile = lambda i: (i, 0)
    const = lambda i: (0, 0)
    return pl.pallas_call(
        _out_proj_kernel,
        grid=(t // OUT_TM,),
        in_specs=[
            pl.BlockSpec((OUT_TM, D_MODEL), tile),
            pl.BlockSpec((OUT_TM, ATT_WIDTH), tile),
            pl.BlockSpec((OUT_TM, M_WIDTH), tile),
            pl.BlockSpec((OUT_TM, 2 * D_MODEL), lambda i: (i, P_GATES // (2 * D_MODEL))),
            pl.BlockSpec((ATT_WIDTH, D_MODEL), const),
            pl.BlockSpec((M_WIDTH, D_MODEL), const),
            pl.BlockSpec((D_MODEL, D_MODEL), const),
        ],
        out_specs=pl.BlockSpec((OUT_TM, D_MODEL), tile),
        out_shape=jax.ShapeDtypeStruct((t, D_MODEL), F32),
        compiler_params=pltpu.CompilerParams(
            dimension_semantics=("arbitrary",), vmem_limit_bytes=VMEM_LIMIT_BYTES),
        name="out_proj",
    )(x, a, m, proj, wa, wm, wo)


def _rope_tables(seq):
    inv = jnp.power(jnp.float32(ROPE_THETA), -jnp.arange(ROPE_HALF, dtype=F32) * 2.0 / ROPE_DIM)
    ang = jnp.arange(seq, dtype=F32)[:, None] * inv[None, :]
    cos, sin = jnp.cos(ang), jnp.sin(ang)
    pad = ATT_HEAD_DIM - ROPE_DIM
    ones = jnp.ones((seq, pad), F32)
    zeros = jnp.zeros((seq, pad), F32)
    zh = jnp.zeros((seq, ROPE_HALF), F32)
    c = jnp.concatenate([cos, cos, ones], axis=-1)
    s1 = jnp.concatenate([-sin, zh, zeros], axis=-1)
    s2 = jnp.concatenate([zh, sin, zeros], axis=-1)
    tab = jnp.stack([c, s1, s2])
    return jnp.concatenate([tab, tab], axis=-1)


def _prep_weights(w_in, b_in):
    order = [(_R_GATES, 2 * D_MODEL), (_R_AZ, ATT_WIDTH), (_R_MO, M_WIDTH), (_R_MZ, M_WIDTH),
             (_R_MQ, M_WIDTH), (_R_MK, M_WIDTH), (_R_AQ, ATT_WIDTH), (_R_AK, ATT_KV_WIDTH)]
    w_main = jnp.concatenate([w_in[:, :, o:o + n] for o, n in order], axis=-1).astype(BF16)
    b_main = jnp.concatenate([b_in[:, o:o + n] for o, n in order], axis=-1)[:, None, :]
    v_order = [(_R_MV, M_WIDTH), (_R_AV, ATT_KV_WIDTH)]
    w_vt = jnp.swapaxes(jnp.concatenate([w_in[:, :, o:o + n] for o, n in v_order], axis=-1), 1, 2).astype(BF16)
    b_vt = jnp.concatenate([b_in[:, o:o + n] for o, n in v_order], axis=-1)[:, :, None]
    w_if = w_in[:, :, _R_IF:_R_IF + N_IFG]
    b_if = b_in[:, _R_IF:_R_IF + N_IFG]
    w_if_pad = jnp.pad(w_if, ((0, 0), (0, 0), (0, LANES - N_IFG))).astype(BF16)
    b_if_pad = jnp.pad(b_if, ((0, 0), (0, LANES - N_IFG)))[:, None, :]
    w_if_t = jnp.swapaxes(w_if, 1, 2).astype(BF16)
    b_if_c = b_if[:, :, None]
    return w_main, b_main, w_vt, b_vt, w_if_pad, w_if_t, b_if_pad, b_if_c


def _trunk(x, rope_tab, layers):
    batch, seq, _ = x.shape
    xf = x.reshape(batch * seq, D_MODEL)
    for lw in layers:
        proj, q4, rq, kn, vta, qct, kc, vtm, grow, ccol = _in_proj(xf, lw, rope_tab, seq)
        a = _attention(proj, q4, rq, kn, vta, lw["sink"], batch, seq)
        m = _mlstm(proj, qct, kc, vtm, grow, ccol, lw["m_norm_g"], batch, seq)
        xf = _out_proj(xf, a, m, proj, lw["w_att_out"], lw["w_m_out"], lw["w_out"])
    return xf.reshape(batch, seq, D_MODEL)


def kernel(x_prompt, x_sample, norm_g, w_in, b_in, q_norm_g, k_norm_g, sink, conv_w, m_norm_g,
           w_att_out, w_m_out, w_out):
    w_main, b_main, w_vt, b_vt, w_if, w_if_t, b_if, b_if_c = _prep_weights(w_in, b_in)
    wa, wm, wo = w_att_out.astype(BF16), w_m_out.astype(BF16), w_out.astype(BF16)
    layers = []
    for l in range(DEPTH):
        layers.append(dict(
            norm_g=norm_g[l][None, :], w_main=w_main[l], b_main=b_main[l], w_vt=w_vt[l], b_vt=b_vt[l],
            w_if=w_if[l], w_if_t=w_if_t[l], b_if=b_if[l], b_if_c=b_if_c[l],
            gq=jnp.tile(q_norm_g[l], 2)[None, :], gk=jnp.tile(k_norm_g[l], 2)[None, :],
            sink=sink[l][None, :], conv_w=conv_w[l], m_norm_g=m_norm_g[l][None, :],
            w_att_out=wa[l], w_m_out=wm[l], w_out=wo[l]))
    outs = []
    for x in (x_prompt, x_sample):
        outs.append(_trunk(x, _rope_tables(x.shape[1]), layers))
    return tuple(outs)
```

```python
import functools
import math

import jax
import jax.numpy as jnp
from jax import lax
from jax.experimental import pallas as pl
from jax.experimental.pallas import tpu as pltpu

F32 = jnp.float32
BF16 = jnp.bfloat16

D_MODEL = 1024
DEPTH = 4
ATT_HEADS = 16
ATT_KV_HEADS = 4
ATT_GROUP = ATT_HEADS // ATT_KV_HEADS
ATT_HEAD_DIM = 64
ATT_WIDTH = ATT_HEADS * ATT_HEAD_DIM
ATT_KV_WIDTH = ATT_KV_HEADS * ATT_HEAD_DIM
WINDOW = 128
ATT_BLOCK = 128
ROPE_THETA = 500000.0
ROPE_DIM = ATT_HEAD_DIM // 4
ROPE_HALF = ROPE_DIM // 2
M_HEADS = 8
M_HEAD_DIM = 128
M_WIDTH = M_HEADS * M_HEAD_DIM
M_CHUNK = 128
CONV_K = 3
NORM_EPS = 1e-6
NEG = -1e30
LOG2E = math.log2(math.e)

LANES = 128
SUBLANES = 8
VMEM_LIMIT_BYTES = 56 * 1024 * 1024
_REF_SPLITS = (ATT_WIDTH, ATT_KV_WIDTH, ATT_KV_WIDTH, ATT_WIDTH,
               M_WIDTH, M_WIDTH, M_WIDTH, M_WIDTH, M_WIDTH,
               M_HEADS, M_HEADS, M_HEADS, M_HEADS, 2 * D_MODEL)
_REF_OFF = [0]
for _w in _REF_SPLITS:
    _REF_OFF.append(_REF_OFF[-1] + _w)
(_R_AQ, _R_AK, _R_AV, _R_AZ, _R_MQ, _R_MK, _R_MV, _R_MO, _R_MZ,
 _R_IF, _R_FF, _R_IB, _R_FB, _R_GATES) = _REF_OFF[:-1]

P_GATES = 0
P_AZ = P_GATES + 2 * D_MODEL
P_MO = P_AZ + ATT_WIDTH
P_MZ = P_MO + M_WIDTH
P_WIDTH = P_MZ + M_WIDTH
W_MQ = P_WIDTH
W_MK = W_MQ + M_WIDTH
W_AQ = W_MK + M_WIDTH
W_AK = W_AQ + ATT_WIDTH
W_WIDTH = W_AK + ATT_KV_WIDTH
VT_ROWS = M_WIDTH + ATT_KV_WIDTH
N_IFG = 4 * M_HEADS

ROW_LI_F, ROW_B_F, ROW_LI_B, ROW_B_B = 0, M_HEADS, 2 * M_HEADS, 3 * M_HEADS
LANE_C_F, LANE_C_B = 0, 2 * M_HEADS

PROJ_DTYPE = BF16
IN_TM = 512
IN_TN = 512
IN_HALO = 16
IN_PIECE = 256
VT_TN = 256
OUT_TM = 512
ATT_TQ = 512
ATT_NSB = ATT_TQ // ATT_BLOCK
ATT_PAIRS = ATT_WIDTH // LANES
ATT_KVAR_WIDTH = ATT_KV_HEADS * 2 * LANES
MLSTM_SCAN_GROUP = 8
MLSTM_OUT_GROUP = 8

_NT = (((1,), (1,)), ((), ()))


def _sigmoid(x):
    return 1.0 / (1.0 + jnp.exp(-x))


def _silu(x):
    return x * _sigmoid(x)


def _log_sigmoid(x):
    return jnp.minimum(x, 0.0) - jnp.log1p(jnp.exp(-jnp.abs(x)))


def _bf16_terms(x):
    hi = x.astype(BF16)
    r1 = x - hi.astype(F32)
    mid = r1.astype(BF16)
    lo = (r1 - mid.astype(F32)).astype(BF16)
    return hi, mid, lo


def _split_dot_left(sel, x):
    return sum(jnp.dot(sel, t, preferred_element_type=F32) for t in _bf16_terms(x))


def _split_dot_right(x, sel):
    return sum(jnp.dot(t, sel, preferred_element_type=F32) for t in _bf16_terms(x))


def _split_dot_nt(sel, x):
    return sum(lax.dot_general(sel, t, _NT, preferred_element_type=F32) for t in _bf16_terms(x))


def _rope(y, tab):
    return (y * tab[0]
            + pltpu.roll(y, LANES - ROPE_HALF, 1) * tab[1]
            + pltpu.roll(y, ROPE_HALF, 1) * tab[2])


def _norm_rope_tile(x, gain, tab):
    lane = lax.broadcasted_iota(jnp.int32, x.shape, 1)
    left = lane < ATT_HEAD_DIM
    sq = x * x
    ss_l = jnp.sum(jnp.where(left, sq, 0.0), axis=-1, keepdims=True)
    ss_r = jnp.sum(jnp.where(left, 0.0, sq), axis=-1, keepdims=True)
    inv = jnp.where(left, lax.rsqrt(ss_l / ATT_HEAD_DIM + NORM_EPS),
                    lax.rsqrt(ss_r / ATT_HEAD_DIM + NORM_EPS))
    return _rope(x * inv * gain, tab)


def _rmsnorm_bf16(x, g):
    return (x * lax.rsqrt(jnp.mean(x * x, axis=-1, keepdims=True) + NORM_EPS) * g).astype(BF16)


def _in_proj_kernel(x_ref, xprev_ref, xnext_ref, g_ref, w_ref, b_ref, wvt_ref, bvt_ref,
                    wif_ref, wift_ref, bif_ref, bifc_ref, cw_ref, tab_ref, gq_ref, gk_ref,
                    proj_ref, q4_ref, kv_ref, vta_ref, qct_ref, kc_ref, vtm_ref, grow_ref, ccol_ref,
                    xn_s, *, tiles_per_seq):
    tm = x_ref.shape[0]
    tile_in_seq = pl.program_id(0) % tiles_per_seq

    xn_s[0:IN_HALO, :] = _rmsnorm_bf16(xprev_ref[...], g_ref[...])
    xn_s[IN_HALO:IN_HALO + tm, :] = _rmsnorm_bf16(x_ref[...], g_ref[...])
    xn_s[IN_HALO + tm:, :] = _rmsnorm_bf16(xnext_ref[...], g_ref[...])
    xn = xn_s[IN_HALO:IN_HALO + tm, :]
    xn_halo = xn_s[...]

    def chunk(c0, width, lhs=None):
        lhs = xn if lhs is None else lhs

        def piece(p0):
            cols = slice(p0, p0 + IN_PIECE)
            return lambda: jnp.dot(lhs, w_ref[:, cols], preferred_element_type=F32) + b_ref[:, cols]
        return [piece(p0) for p0 in range(c0, c0 + width, IN_PIECE)]

    stages = []

    def store_plain(c0):
        def store(acc, p0):
            def run():
                proj_ref[:, p0:p0 + IN_PIECE] = acc.astype(proj_ref.dtype)
            return run
        return lambda accs: [store(acc, c0 + i * IN_PIECE) for i, acc in enumerate(accs)]

    plain = [(chunk(c0, IN_TN), store_plain(c0)) for c0 in range(0, P_WIDTH, IN_TN)]

    has_before = tile_in_seq > 0
    has_after = tile_in_seq < tiles_per_seq - 1
    row_id = lax.broadcasted_iota(jnp.int32, (tm, LANES), 0)
    kscale = M_HEAD_DIM ** -0.5
    lane_tiles = range(0, IN_PIECE, LANES)

    def conv_silu(c0):
        def conv(acc, p0, l0):
            cols = slice(p0 + l0, p0 + l0 + LANES)

            def run():
                a = acc[IN_HALO:IN_HALO + tm, l0:l0 + LANES]
                before = jnp.where(has_before, acc[IN_HALO - 1:IN_HALO, l0:l0 + LANES], 0.0)
                after = jnp.where(has_after, acc[IN_HALO + tm:IN_HALO + tm + 1, l0:l0 + LANES], 0.0)
                xm = jnp.where(row_id == 0, before, pltpu.roll(a, 1, 0))
                xp = jnp.where(row_id == tm - 1, after, pltpu.roll(a, tm - 1, 0))
                y = _silu(xm * cw_ref[0:1, cols] + a * cw_ref[1:2, cols] + xp * cw_ref[2:3, cols])
                if p0 < M_WIDTH:
                    qct_ref[cols, :] = y.T.astype(BF16)
                else:
                    kc_ref[:, p0 + l0 - M_WIDTH:p0 + l0 - M_WIDTH + LANES] = (y * kscale).astype(BF16)
            return run
        return lambda accs: [conv(acc, c0 + i * IN_PIECE, l0) for i, acc in enumerate(accs) for l0 in lane_tiles]

    conv = [(chunk(W_MQ + c0, IN_TN, lhs=xn_halo), conv_silu(c0)) for c0 in range(0, 2 * M_WIDTH, IN_TN)]

    tab = tab_ref[...]

    def q_rope(c0):
        def rope(acc, p0, l0):
            def run():
                y = _norm_rope_tile(acc[:, l0:l0 + LANES], gq_ref[...], tab).astype(BF16)
                for j in range(tm // ATT_BLOCK):
                    q4_ref[j, (p0 + l0) // LANES] = y[j * ATT_BLOCK:(j + 1) * ATT_BLOCK, :]
            return run
        return lambda accs: [rope(acc, c0 + i * IN_PIECE, l0) for i, acc in enumerate(accs) for l0 in lane_tiles]

    att_q = [(chunk(W_AQ + c0, IN_TN), q_rope(c0)) for c0 in range(0, ATT_WIDTH, IN_TN)]

    left_half = lax.broadcasted_iota(jnp.int32, (tm, LANES), 1) < ATT_HEAD_DIM

    def k_norm_rope(accs):
        def norm_rope(acc, p0):
            def run():
                for t in range(IN_PIECE // LANES):
                    kt = _norm_rope_tile(acc[:, t * LANES:(t + 1) * LANES], gk_ref[...], tab)
                    sw = pltpu.roll(kt, ATT_HEAD_DIM, 1)
                    head = (p0 // LANES + t) * 2
                    variants = ((head, 0, jnp.where(left_half, kt, 0.0)), (head, 1, jnp.where(left_half, 0.0, sw)),
                                (head + 1, 0, jnp.where(left_half, sw, 0.0)), (head + 1, 1, jnp.where(left_half, 0.0, kt)))
                    for h, v, val in variants:
                        col = (2 * h + v) * LANES
                        kv_ref[:, col:col + LANES] = val.astype(BF16)
            return run
        return [norm_rope(acc, i * IN_PIECE) for i, acc in enumerate(accs)]

    att_k = (chunk(W_AK, ATT_KV_WIDTH), k_norm_rope)

    def v_t(r0):
        rows = slice(r0, r0 + VT_TN)

        def matmul():
            return lax.dot_general(wvt_ref[rows, :], xn, _NT, preferred_element_type=F32) + bvt_ref[rows, :]

        def store(vts_):
            def run():
                if r0 < M_WIDTH:
                    vtm_ref[rows, :] = vts_[0].astype(BF16)
                else:
                    vta_ref[r0 - M_WIDTH:r0 - M_WIDTH + VT_TN, :] = vts_[0].astype(BF16)
            return [run]
        return [matmul], store

    vts = [v_t(r0) for r0 in range(0, VT_ROWS, VT_TN)]

    def gate_matmul():
        gc = jnp.dot(xn, wif_ref[...], preferred_element_type=F32) + bif_ref[...]
        gr = lax.dot_general(wift_ref[...], xn, _NT, preferred_element_type=F32) + bifc_ref[...]
        return gc, gr

    def gate_epilogue(gates):
        gc, gr = gates[0]
        ls_c = _log_sigmoid(gc)
        ls_r = _log_sigmoid(gr)
        r = lax.broadcasted_iota(jnp.int32, (M_CHUNK, M_CHUNK), 0)
        c = lax.broadcasted_iota(jnp.int32, (M_CHUNK, M_CHUNK), 1)
        tril = jnp.where(c <= r, 1.0, 0.0).astype(BF16)
        triu = jnp.where(c >= r, 1.0, 0.0).astype(BF16)
        lane = lax.broadcasted_iota(jnp.int32, (M_CHUNK, LANES), 1)
        fwd_lanes = lane < 2 * M_HEADS
        grow_ref[ROW_LI_F:ROW_LI_F + M_HEADS, :] = gr[0:M_HEADS, :]
        grow_ref[ROW_LI_B:ROW_LI_B + M_HEADS, :] = gr[2 * M_HEADS:3 * M_HEADS, :]
        for ci in range(tm // M_CHUNK):
            rows = slice(ci * M_CHUNK, (ci + 1) * M_CHUNK)
            lsc = ls_c[rows, :]
            b_f = _split_dot_left(tril, lsc)
            b_b = _split_dot_left(triu, lsc)
            cc = gc[rows, :] - pltpu.roll(jnp.where(fwd_lanes, b_f, b_b), LANES - M_HEADS, 1)
            for ti, term in enumerate(_bf16_terms(cc)):
                ccol_ref[ti, rows, :] = term
            grow_ref[ROW_B_F:ROW_B_F + M_HEADS, rows] = _split_dot_right(ls_r[M_HEADS:2 * M_HEADS, rows], triu)
            grow_ref[ROW_B_B:ROW_B_B + M_HEADS, rows] = _split_dot_right(ls_r[3 * M_HEADS:4 * M_HEADS, rows], tril)

    light = plain + vts
    heavy = [([gate_matmul], lambda gates: [lambda: gate_epilogue(gates)])] + conv + att_q + [att_k]
    per_heavy = len(light) // len(heavy)
    for i, stage in enumerate(heavy):
        stages.append(stage)
        stages.extend(light[i * per_heavy:(i + 1) * per_heavy])
    stages.extend(light[len(heavy) * per_heavy:])

    n_matmuls = sum(len(matmuls) for matmuls, _ in stages)
    queue = []
    emitted = 0
    for matmuls, make_epilogues in stages:
        results = []
        for matmul in matmuls:
            results.append(matmul())
            emitted += 1
            take = -(-len(queue) // max(n_matmuls - emitted, 1)) if queue else 0
            for run in queue[:take]:
                run()
            queue = queue[take:]
        queue.extend(make_epilogues(results))
    for run in queue:
        run()


def _in_proj(x, lw, rope_tab, seq):
    t = x.shape[0]
    tps = seq // IN_TM
    halo_blocks = IN_TM // IN_HALO
    const = lambda i: (0, 0)
    tile = lambda i: (i, 0)
    cols = lambda i: (0, i)
    resident = dict(pipeline_mode=pl.Buffered(1))
    return pl.pallas_call(
        functools.partial(_in_proj_kernel, tiles_per_seq=tps),
        grid=(t // IN_TM,),
        in_specs=[
            pl.BlockSpec((IN_TM, D_MODEL), tile),
            pl.BlockSpec((IN_HALO, D_MODEL), lambda i: (jnp.maximum(i * halo_blocks - 1, 0), 0)),
            pl.BlockSpec((IN_HALO, D_MODEL), lambda i: (jnp.minimum((i + 1) * halo_blocks, t // IN_HALO - 1), 0)),
            pl.BlockSpec((1, D_MODEL), const),
            pl.BlockSpec((D_MODEL, W_WIDTH), const, **resident),
            pl.BlockSpec((1, W_WIDTH), const),
            pl.BlockSpec((VT_ROWS, D_MODEL), const, **resident),
            pl.BlockSpec((VT_ROWS, 1), const),
            pl.BlockSpec((D_MODEL, LANES), const),
            pl.BlockSpec((N_IFG, D_MODEL), const),
            pl.BlockSpec((1, LANES), const),
            pl.BlockSpec((N_IFG, 1), const),
            pl.BlockSpec((CONV_K, 2 * M_WIDTH), const),
            pl.BlockSpec((3, IN_TM, LANES), lambda i: (0, i % tps, 0)),
            pl.BlockSpec((1, LANES), const),
            pl.BlockSpec((1, LANES), const),
        ],
        out_specs=[
            pl.BlockSpec((IN_TM, P_WIDTH), tile),
            pl.BlockSpec((IN_TM // ATT_BLOCK, ATT_PAIRS, ATT_BLOCK, LANES), lambda i: (i, 0, 0, 0)),
            pl.BlockSpec((IN_TM, ATT_KVAR_WIDTH), tile),
            pl.BlockSpec((ATT_KV_WIDTH, IN_TM), cols),
            pl.BlockSpec((M_WIDTH, IN_TM), cols),
            pl.BlockSpec((IN_TM, M_WIDTH), tile),
            pl.BlockSpec((M_WIDTH, IN_TM), cols),
            pl.BlockSpec((N_IFG, IN_TM), cols),
            pl.BlockSpec((3, IN_TM, LANES), lambda i: (0, i, 0)),
        ],
        out_shape=[
            jax.ShapeDtypeStruct((t, P_WIDTH), PROJ_DTYPE),
            jax.ShapeDtypeStruct((t // ATT_BLOCK, ATT_PAIRS, ATT_BLOCK, LANES), BF16),
            jax.ShapeDtypeStruct((t, ATT_KVAR_WIDTH), BF16),
            jax.ShapeDtypeStruct((ATT_KV_WIDTH, t), BF16),
            jax.ShapeDtypeStruct((M_WIDTH, t), BF16),
            jax.ShapeDtypeStruct((t, M_WIDTH), BF16),
            jax.ShapeDtypeStruct((M_WIDTH, t), BF16),
            jax.ShapeDtypeStruct((N_IFG, t), F32),
            jax.ShapeDtypeStruct((3, t, LANES), BF16),
        ],
        scratch_shapes=[pltpu.VMEM((IN_TM + 2 * IN_HALO, D_MODEL), BF16)],
        compiler_params=pltpu.CompilerParams(
            dimension_semantics=("arbitrary",), vmem_limit_bytes=VMEM_LIMIT_BYTES),
        name="in_proj",
    )(x, x, x, lw["norm_g"], lw["w_main"], lw["b_main"], lw["w_vt"], lw["b_vt"], lw["w_if"], lw["w_if_t"],
      lw["b_if"], lw["b_if_c"], lw["conv_w"], rope_tab, lw["gq"], lw["gk"])


def _attention_kernel(q4_ref, z_ref, kp_ref, kc_ref, kn_ref, vp_ref, vc_ref, vn_ref, sink_ref, out_ref,
                      ks, vts, out_t, *, n_blocks):
    ti = pl.program_id(1)
    bq = ATT_BLOCK

    r0 = 0
    for k_ref, v_ref in ((kp_ref, vp_ref), (kc_ref, vc_ref), (kn_ref, vn_ref)):
        nr = k_ref.shape[0]
        ks[r0:r0 + nr, :] = k_ref[...]
        vts[:, r0:r0 + nr] = v_ref[...]
        r0 += nr

    def group_heads(h):
        return (ATT_GROUP * h, ATT_GROUP * h + 2, ATT_GROUP * h + 1, ATT_GROUP * h + 3)

    sink_rows = [jnp.concatenate([jnp.broadcast_to(sink_ref[:, c:c + 1] * LOG2E, (1, bq)) for c in group_heads(h)],
                                 axis=1) for h in range(ATT_KV_HEADS)]

    kk = lax.broadcasted_iota(jnp.int32, (bq, bq), 0)
    qq = lax.broadcasted_iota(jnp.int32, (bq, bq), 1)
    prev_in_band = kk >= qq
    next_in_band = kk <= qq

    def sub_block(j, carry):
        blk = ti * ATT_NSB + j
        bias_prev = jnp.where(prev_in_band, jnp.where(blk > 0, 0.0, NEG), NEG).astype(F32)
        bias_next = jnp.where(next_in_band, jnp.where(blk < n_blocks - 1, 0.0, NEG), NEG).astype(F32)
        bias_prev = jnp.concatenate([bias_prev] * ATT_GROUP, axis=1)
        bias_next = jnp.concatenate([bias_next] * ATT_GROUP, axis=1)
        row0 = pl.multiple_of(j * bq, bq)
        q_slabs = q4_ref[j]
        hs = range(ATT_KV_HEADS)
        q_pairs = [q_slabs[2 * h:2 * h + 2].reshape(2 * bq, LANES) for h in hs]
        ss = [jnp.concatenate(
            [lax.dot_general(ks[pl.ds(row0, 3 * bq), (2 * h + v) * LANES:(2 * h + v + 1) * LANES], q_pairs[h], _NT,
                             preferred_element_type=F32)
             for v in range(2)], axis=1) for h in hs]
        ss = [jnp.concatenate([s[:bq] + bias_prev, s[bq:2 * bq], s[2 * bq:] + bias_next], axis=0) for s in ss]
        ms = [jnp.maximum(jnp.max(s, axis=0, keepdims=True), sink_rows[h]) for h, s in zip(hs, ss)]
        ps = [jnp.exp2(s - m) for s, m in zip(ss, ms)]
        denoms = [jnp.sum(p, axis=0, keepdims=True) + jnp.exp2(sink_rows[h] - m) for h, p, m in zip(hs, ps, ms)]
        os_ = [jnp.dot(vts[h * ATT_HEAD_DIM:(h + 1) * ATT_HEAD_DIM, pl.ds(row0, 3 * bq)], p.astype(BF16),
                       preferred_element_type=F32) * (1.0 / denom)
               for h, p, denom in zip(hs, ps, denoms)]
        for h, o in zip(hs, os_):
            for i, c in enumerate(group_heads(h)):
                out_t[c * ATT_HEAD_DIM:(c + 1) * ATT_HEAD_DIM, :] = o[:, i * bq:(i + 1) * bq]
        att = out_t[...].T
        z = z_ref[pl.ds(row0, bq), :].astype(F32)
        out_ref[pl.ds(row0, bq), :] = (att * _silu(z)).astype(out_ref.dtype)
        return carry

    lax.fori_loop(0, ATT_NSB, sub_block, 0)


def _attention(proj, q4, kv, vta, sink, batch, seq):
    t = proj.shape[0]
    nb = seq // ATT_BLOCK
    nt = seq // ATT_TQ
    bq = ATT_BLOCK

    prev_blk = lambda b, i: b * nb + jnp.maximum(i * ATT_NSB - 1, 0)
    next_blk = lambda b, i: b * nb + jnp.minimum((i + 1) * ATT_NSB, nb - 1)
    return pl.pallas_call(
        functools.partial(_attention_kernel, n_blocks=nb),
        grid=(batch, nt),
        in_specs=[
            pl.BlockSpec((ATT_NSB, ATT_PAIRS, bq, LANES), lambda b, i: (b * nt + i, 0, 0, 0)),
            pl.BlockSpec((ATT_TQ, ATT_WIDTH), lambda b, i: (b * nt + i, P_AZ // ATT_WIDTH)),
            pl.BlockSpec((bq, ATT_KVAR_WIDTH), lambda b, i: (prev_blk(b, i), 0)),
            pl.BlockSpec((ATT_TQ, ATT_KVAR_WIDTH), lambda b, i: (b * nt + i, 0)),
            pl.BlockSpec((bq, ATT_KVAR_WIDTH), lambda b, i: (next_blk(b, i), 0)),
            pl.BlockSpec((ATT_KV_WIDTH, bq), lambda b, i: (0, prev_blk(b, i))),
            pl.BlockSpec((ATT_KV_WIDTH, ATT_TQ), lambda b, i: (0, b * nt + i)),
            pl.BlockSpec((ATT_KV_WIDTH, bq), lambda b, i: (0, next_blk(b, i))),
            pl.BlockSpec((1, ATT_HEADS), lambda b, i: (0, 0)),
        ],
        out_specs=pl.BlockSpec((ATT_TQ, ATT_WIDTH), lambda b, i: (b * nt + i, 0)),
        out_shape=jax.ShapeDtypeStruct((t, ATT_WIDTH), BF16),
        scratch_shapes=[
            pltpu.VMEM((ATT_TQ + 2 * bq, ATT_KVAR_WIDTH), BF16),
            pltpu.VMEM((ATT_KV_WIDTH, ATT_TQ + 2 * bq), BF16),
            pltpu.VMEM((ATT_WIDTH, bq), F32),
        ],
        compiler_params=pltpu.CompilerParams(
            dimension_semantics=("arbitrary", "arbitrary"), vmem_limit_bytes=VMEM_LIMIT_BYTES),
        name="attention",
    )(q4, proj, kv, kv, kv, vta, vta, vta, sink)


def _mlstm_kernel(qct_ref, kc_ref, vt_ref, o_ref, z_ref, grow_ref, ccol_ref, ng_ref,
                  out_ref, cct_s, nf_s, nb_s, mf_s, mb_s, *, seq):
    hd = pl.program_id(1)
    L = M_CHUNK
    dv = M_HEAD_DIM
    nc = seq // L

    head_row = lax.broadcasted_iota(jnp.int32, (M_HEADS, L), 0) == hd

    def gate_row(base, t0):
        rows = grow_ref[base:base + M_HEADS, pl.ds(t0, L)]
        return jnp.sum(jnp.where(head_row, rows, 0.0), axis=0, keepdims=True)

    U = MLSTM_SCAN_GROUP
    dirs = ((ROW_LI_F, ROW_B_F, L - 1, nf_s, mf_s, 0), (ROW_LI_B, ROW_B_B, 0, nb_s, mb_s, dv))

    def scan_body(gj, carry):
        cis = [[gj * U + u for u in range(U)], [nc - 1 - (gj * U + u) for u in range(U)]]
        t0s = [[pl.multiple_of(ci * L, L) for ci in row] for row in cis]
        lis = [[gate_row(dirs[d][0], t0) for t0 in t0s[d]] for d in range(2)]
        bs = [[gate_row(dirs[d][1], t0) for t0 in t0s[d]] for d in range(2)]
        b_lasts = [[b[:, dirs[d][2]:dirs[d][2] + 1] for b in bs[d]] for d in range(2)]
        gs = [[b_last - b + li for b_last, b, li in zip(b_lasts[d], bs[d], lis[d])] for d in range(2)]
        g_maxs = [[jnp.max(g, axis=-1, keepdims=True) for g in gs[d]] for d in range(2)]
        ks = [[kc_ref[pl.ds(t0, L), :] for t0 in t0s[d]] for d in range(2)]
        vts = [[vt_ref[:, pl.ds(t0, L)].astype(F32) for t0 in t0s[d]] for d in range(2)]

        ms = [[carry[d][2]] for d in range(2)]
        for d in range(2):
            for u in range(U):
                ms[d].append(jnp.maximum(b_lasts[d][u] + ms[d][u], g_maxs[d][u]))
        w_cs = [[jnp.exp(b_lasts[d][u] + ms[d][u] - ms[d][u + 1]) for u in range(U)] for d in range(2)]
        w_ks = [[jnp.exp(gs[d][u] - ms[d][u + 1]) for u in range(U)] for d in range(2)]
        c_upds = [[jnp.dot((vts[d][u] * w_ks[d][u]).astype(BF16), ks[d][u], preferred_element_type=F32)
                   for u in range(U)] for d in range(2)]
        n_upds = [[jnp.dot(jnp.broadcast_to(w_ks[d][u], (SUBLANES, L)).astype(BF16), ks[d][u],
                           preferred_element_type=F32)[0:1, :] for u in range(U)] for d in range(2)]

        out = []
        for d in range(2):
            _, _, _, n_s, m_s, row0 = dirs[d]
            ct_st, n_st, _ = carry[d]
            for u in range(U):
                ci = cis[d][u]
                cct_s[ci, row0:row0 + dv, :] = ct_st.astype(BF16)
                n_s[ci] = jnp.broadcast_to(n_st, (SUBLANES, LANES))
                m_s[ci] = jnp.broadcast_to(ms[d][u], (SUBLANES, LANES))
                ct_st = w_cs[d][u] * ct_st + c_upds[d][u]
                n_st = w_cs[d][u] * n_st + n_upds[d][u]
            out.append((ct_st, n_st, ms[d][U]))
        return tuple(out)

    init = (jnp.zeros((dv, M_HEAD_DIM), F32), jnp.zeros((1, M_HEAD_DIM), F32),
            jnp.full((1, 1), NEG, F32))
    lax.fori_loop(0, nc // U, scan_body, (init, init))

    rr = lax.broadcasted_iota(jnp.int32, (L, L), 0)
    cc = lax.broadcasted_iota(jnp.int32, (L, L), 1)
    row8 = lax.broadcasted_iota(jnp.int32, (SUBLANES, LANES), 0)
    sel_k = lax.broadcasted_iota(jnp.int32, (LANES, 2 * L), 0)
    sel_n = lax.broadcasted_iota(jnp.int32, (LANES, 2 * L), 1)
    sel_lane = jnp.where(sel_n < L, LANE_C_F + hd, LANE_C_B + hd)
    sel = jnp.where(sel_k == sel_lane, 1.0, 0.0).astype(BF16)
    keep_f = rr <= cc
    keep_b = rr >= cc
    G = MLSTM_OUT_GROUP

    def out_body(gi, carry):
        cis = [gi * G + g for g in range(G)]
        t0s = [pl.multiple_of(ci * L, L) for ci in cis]
        qts = [qct_ref[:, pl.ds(t0, L)] for t0 in t0s]
        qk_ts = [jnp.dot(kc_ref[pl.ds(t0, L), :], qt, preferred_element_type=F32)
                 for t0, qt in zip(t0s, qts)]
        cqs = [jnp.dot(cct_s[ci], qt, preferred_element_type=F32) for ci, qt in zip(cis, qts)]
        n8s = [jnp.where(row8 == 0, nf_s[ci], jnp.where(row8 == 1, nb_s[ci], 0.0)).astype(BF16) for ci in cis]
        qns = [jnp.dot(n8, qt, preferred_element_type=F32) for n8, qt in zip(n8s, qts)]
        c_bcs = [sum(jnp.dot(ccol_ref[ti, pl.ds(t0, L), :], sel, preferred_element_type=F32) for ti in range(3))
                 for t0 in t0s]

        items = []
        for g in range(G):
            items.append((g, c_bcs[g][:, :L], qns[g][0:1, :], ROW_B_F, keep_f, mf_s))
            items.append((g, c_bcs[g][:, L:], qns[g][1:2, :], ROW_B_B, keep_b, mb_s))
        b_rs = [gate_row(b_base, t0s[g]) for g, _, _, b_base, _, _ in items]
        m_prevs = [m_s[cis[g]][0:1, 0:1] for g, _, _, _, _, m_s in items]
        ds = [jnp.where(keep, b_r + c_s, NEG)
              for (_, c_s, _, _, keep, _), b_r in zip(items, b_rs)]
        m_inters = [b_r + m_prev for b_r, m_prev in zip(b_rs, m_prevs)]
        m_ts = [jnp.maximum(m_inter, jnp.max(d, axis=0, keepdims=True)) for m_inter, d in zip(m_inters, ds)]
        w_inters = [jnp.exp(m_inter - m_t) for m_inter, m_t in zip(m_inters, m_ts)]
        a_s = [jnp.exp(d - m_t) * qk_ts[it[0]] for d, m_t, it in zip(ds, m_ts, items)]
        dens = [w_inter * it[2] + jnp.sum(a, axis=0, keepdims=True) for w_inter, it, a in zip(w_inters, items, a_s)]
        invs = [1.0 / jnp.maximum(jnp.abs(den), jnp.exp(-m_t)) for den, m_t in zip(dens, m_ts)]
        ps = [a * inv for a, inv in zip(a_s, invs)]
        ws = [w_inter * inv for w_inter, inv in zip(w_inters, invs)]

        h_ts = [jnp.dot(vt_ref[:, pl.ds(t0s[g], L)], (ps[2 * g] + ps[2 * g + 1]).astype(BF16),
                        preferred_element_type=F32)
                + cqs[g][:dv, :] * ws[2 * g] + cqs[g][dv:, :] * ws[2 * g + 1] for g in range(G)]
        hs = [_sigmoid(o_ref[pl.ds(t0, L), :].astype(F32)) * h_t.T for t0, h_t in zip(t0s, h_ts)]
        hs = [h * lax.rsqrt(jnp.mean(h * h, axis=-1, keepdims=True) + NORM_EPS) * ng_ref[...] for h in hs]
        for t0, h in zip(t0s, hs):
            out_ref[pl.ds(t0, L), :] = (h * _silu(z_ref[pl.ds(t0, L), :].astype(F32))).astype(out_ref.dtype)
        return carry

    lax.fori_loop(0, nc // G, out_body, 0)


def _mlstm(proj, qct, kc, vtm, grow, ccol, ng, batch, seq):
    t = proj.shape[0]
    nc = seq // M_CHUNK
    head_rows = lambda b, h: (h, b)
    head_cols = lambda b, h: (b, h)

    def proj_head(col0):
        return pl.BlockSpec((seq, M_HEAD_DIM), lambda b, h: (b, col0 // M_HEAD_DIM + h))

    return pl.pallas_call(
        functools.partial(_mlstm_kernel, seq=seq),
        grid=(batch, M_HEADS),
        in_specs=[
            pl.BlockSpec((M_HEAD_DIM, seq), head_rows),
            pl.BlockSpec((seq, M_HEAD_DIM), head_cols),
            pl.BlockSpec((M_HEAD_DIM, seq), head_rows),
            proj_head(P_MO), proj_head(P_MZ),
            pl.BlockSpec((N_IFG, seq), lambda b, h: (0, b)),
            pl.BlockSpec((3, seq, LANES), lambda b, h: (0, b, 0)),
            pl.BlockSpec((1, M_HEAD_DIM), lambda b, h: (0, h)),
        ],
        out_specs=pl.BlockSpec((seq, M_HEAD_DIM), head_cols),
        out_shape=jax.ShapeDtypeStruct((t, M_WIDTH), BF16),
        scratch_shapes=[
            pltpu.VMEM((nc, 2 * M_HEAD_DIM, M_HEAD_DIM), BF16),
            pltpu.VMEM((nc, SUBLANES, M_HEAD_DIM), F32),
            pltpu.VMEM((nc, SUBLANES, M_HEAD_DIM), F32),
            pltpu.VMEM((nc, SUBLANES, LANES), F32),
            pltpu.VMEM((nc, SUBLANES, LANES), F32),
        ],
        compiler_params=pltpu.CompilerParams(
            dimension_semantics=("arbitrary", "arbitrary"), vmem_limit_bytes=VMEM_LIMIT_BYTES),
        name="mlstm",
    )(qct, kc, vtm, proj, proj, grow, ccol, ng)


def _out_proj_kernel(x_ref, a_ref, m_ref, gates_ref, wa_ref, wm_ref, wo_ref, out_ref):
    branch_a = jnp.dot(a_ref[...], wa_ref[...], preferred_element_type=F32)
    branch_m = jnp.dot(m_ref[...], wm_ref[...], preferred_element_type=F32)
    gates = _sigmoid(gates_ref[...].astype(F32))
    merged = gates[:, :D_MODEL] * branch_a + gates[:, D_MODEL:] * branch_m
    out_ref[...] = x_ref[...] + jnp.dot(merged.astype(BF16), wo_ref[...], preferred_element_type=F32)


def _out_proj(x, a, m, proj, wa, wm, wo):
    t = x.shape[0]
    tile = lambda i: (i, 0)
    const = lambda i: (0, 0)
    return pl.pallas_call(
        _out_proj_kernel,
        grid=(t // OUT_TM,),
        in_specs=[
            pl.BlockSpec((OUT_TM, D_MODEL), tile),
            pl.BlockSpec((OUT_TM, ATT_WIDTH), tile),
            pl.BlockSpec((OUT_TM, M_WIDTH), tile),
            pl.BlockSpec((OUT_TM, 2 * D_MODEL), lambda i: (i, P_GATES // (2 * D_MODEL))),
            pl.BlockSpec((ATT_WIDTH, D_MODEL), const),
            pl.BlockSpec((M_WIDTH, D_MODEL), const),
            pl.BlockSpec((D_MODEL, D_MODEL), const),
        ],
        out_specs=pl.BlockSpec((OUT_TM, D_MODEL), tile),
        out_shape=jax.ShapeDtypeStruct((t, D_MODEL), F32),
        compiler_params=pltpu.CompilerParams(
            dimension_semantics=("arbitrary",), vmem_limit_bytes=VMEM_LIMIT_BYTES),
        name="out_proj",
    )(x, a, m, proj, wa, wm, wo)


def _rope_tables(seq):
    inv = jnp.power(jnp.float32(ROPE_THETA), -jnp.arange(ROPE_HALF, dtype=F32) * 2.0 / ROPE_DIM)
    ang = jnp.arange(seq, dtype=F32)[:, None] * inv[None, :]
    cos, sin = jnp.cos(ang), jnp.sin(ang)
    pad = ATT_HEAD_DIM - ROPE_DIM
    ones = jnp.ones((seq, pad), F32)
    zeros = jnp.zeros((seq, pad), F32)
    zh = jnp.zeros((seq, ROPE_HALF), F32)
    c = jnp.concatenate([cos, cos, ones], axis=-1)
    s1 = jnp.concatenate([-sin, zh, zeros], axis=-1)
    s2 = jnp.concatenate([zh, sin, zeros], axis=-1)
    tab = jnp.stack([c, s1, s2])
    return jnp.concatenate([tab, tab], axis=-1)


def _prep_weights(w_in, b_in):
    order = [(_R_GATES, 2 * D_MODEL), (_R_AZ, ATT_WIDTH), (_R_MO, M_WIDTH), (_R_MZ, M_WIDTH),
             (_R_MQ, M_WIDTH), (_R_MK, M_WIDTH), (_R_AQ, ATT_WIDTH), (_R_AK, ATT_KV_WIDTH)]
    w_main = jnp.concatenate([w_in[:, :, o:o + n] for o, n in order], axis=-1).astype(BF16)
    b_main = jnp.concatenate([b_in[:, o:o + n] for o, n in order], axis=-1)[:, None, :]
    v_order = [(_R_MV, M_WIDTH), (_R_AV, ATT_KV_WIDTH)]
    w_vt = jnp.swapaxes(jnp.concatenate([w_in[:, :, o:o + n] for o, n in v_order], axis=-1), 1, 2).astype(BF16)
    b_vt = jnp.concatenate([b_in[:, o:o + n] for o, n in v_order], axis=-1)[:, :, None]
    w_if = w_in[:, :, _R_IF:_R_IF + N_IFG]
    b_if = b_in[:, _R_IF:_R_IF + N_IFG]
    w_if_pad = jnp.pad(w_if, ((0, 0), (0, 0), (0, LANES - N_IFG))).astype(BF16)
    b_if_pad = jnp.pad(b_if, ((0, 0), (0, LANES - N_IFG)))[:, None, :]
    w_if_t = jnp.swapaxes(w_if, 1, 2).astype(BF16)
    b_if_c = b_if[:, :, None]
    return w_main, b_main, w_vt, b_vt, w_if_pad, w_if_t, b_if_pad, b_if_c


def _trunk(x, rope_tab, layers):
    batch, seq, _ = x.shape
    xf = x.reshape(batch * seq, D_MODEL)
    for lw in layers:
        proj, q4, kv, vta, qct, kc, vtm, grow, ccol = _in_proj(xf, lw, rope_tab, seq)
        a = _attention(proj, q4, kv, vta, lw["sink"], batch, seq)
        m = _mlstm(proj, qct, kc, vtm, grow, ccol, lw["m_norm_g"], batch, seq)
        xf = _out_proj(xf, a, m, proj, lw["w_att_out"], lw["w_m_out"], lw["w_out"])
    return xf.reshape(batch, seq, D_MODEL)


def kernel(x_prompt, x_sample, norm_g, w_in, b_in, q_norm_g, k_norm_g, sink, conv_w, m_norm_g,
           w_att_out, w_m_out, w_out):
    w_main, b_main, w_vt, b_vt, w_if, w_if_t, b_if, b_if_c = _prep_weights(w_in, b_in)
    wa, wm, wo = w_att_out.astype(BF16), w_m_out.astype(BF16), w_out.astype(BF16)
    layers = []
    for l in range(DEPTH):
        layers.append(dict(
            norm_g=norm_g[l][None, :], w_main=w_main[l], b_main=b_main[l], w_vt=w_vt[l], b_vt=b_vt[l],
            w_if=w_if[l], w_if_t=w_if_t[l], b_if=b_if[l], b_if_c=b_if_c[l],
            gq=jnp.tile(q_norm_g[l] * (ATT_HEAD_DIM ** -0.5 * LOG2E), 2)[None, :],
            gk=jnp.tile(k_norm_g[l], 2)[None, :],
            sink=sink[l][None, :], conv_w=conv_w[l], m_norm_g=m_norm_g[l][None, :],
            w_att_out=wa[l], w_m_out=wm[l], w_out=wo[l]))
    outs = []
    for x in (x_prompt, x_sample):
        outs.append(_trunk(x, _rope_tables(x.shape[1]), layers))
    return tuple(outs)
```

```python
import functools
import math

import jax
import jax.numpy as jnp
from jax import lax
from jax.experimental import pallas as pl
from jax.experimental.pallas import tpu as pltpu

F32 = jnp.float32
BF16 = jnp.bfloat16

D_MODEL = 1024
DEPTH = 4
ATT_HEADS = 16
ATT_KV_HEADS = 4
ATT_GROUP = ATT_HEADS // ATT_KV_HEADS
ATT_HEAD_DIM = 64
ATT_WIDTH = ATT_HEADS * ATT_HEAD_DIM
ATT_KV_WIDTH = ATT_KV_HEADS * ATT_HEAD_DIM
WINDOW = 128
ATT_BLOCK = 128
ROPE_THETA = 500000.0
ROPE_DIM = ATT_HEAD_DIM // 4
ROPE_HALF = ROPE_DIM // 2
M_HEADS = 8
M_HEAD_DIM = 128
M_WIDTH = M_HEADS * M_HEAD_DIM
M_CHUNK = 128
CONV_K = 3
NORM_EPS = 1e-6
NEG = -1e30
LOG2E = math.log2(math.e)

LANES = 128
SUBLANES = 8
VMEM_LIMIT_BYTES = 56 * 1024 * 1024
_REF_SPLITS = (ATT_WIDTH, ATT_KV_WIDTH, ATT_KV_WIDTH, ATT_WIDTH,
               M_WIDTH, M_WIDTH, M_WIDTH, M_WIDTH, M_WIDTH,
               M_HEADS, M_HEADS, M_HEADS, M_HEADS, 2 * D_MODEL)
_REF_OFF = [0]
for _w in _REF_SPLITS:
    _REF_OFF.append(_REF_OFF[-1] + _w)
(_R_AQ, _R_AK, _R_AV, _R_AZ, _R_MQ, _R_MK, _R_MV, _R_MO, _R_MZ,
 _R_IF, _R_FF, _R_IB, _R_FB, _R_GATES) = _REF_OFF[:-1]

P_GATES = 0
P_AZ = P_GATES + 2 * D_MODEL
P_MO = P_AZ + ATT_WIDTH
P_MZ = P_MO + M_WIDTH
P_WIDTH = P_MZ + M_WIDTH
W_MQ = P_WIDTH
W_MK = W_MQ + M_WIDTH
W_AQ = W_MK + M_WIDTH
W_AK = W_AQ + ATT_WIDTH
W_WIDTH = W_AK + ATT_KV_WIDTH
VT_ROWS = M_WIDTH + ATT_KV_WIDTH
N_IFG = 4 * M_HEADS

ROW_LI_F, ROW_B_F, ROW_LI_B, ROW_B_B = 0, M_HEADS, 2 * M_HEADS, 3 * M_HEADS
LANE_C_F, LANE_C_B = 0, 2 * M_HEADS

PROJ_DTYPE = BF16
IN_TM = 512
IN_TN = 512
IN_EPILOGUES_PER_MATMUL = 2
IN_HALO = 16
IN_PIECE = 256
VT_TN = 256
OUT_TM = 512
ATT_TQ = 512
ATT_NSB = ATT_TQ // ATT_BLOCK
ATT_PAIRS = ATT_WIDTH // LANES
ATT_KVAR_WIDTH = ATT_KV_HEADS * 2 * LANES
MLSTM_SCAN_GROUP = 16
MLSTM_OUT_GROUP = 16

_NT = (((1,), (1,)), ((), ()))


def _sigmoid(x):
    return 1.0 / (1.0 + jnp.exp(-x))


def _silu(x):
    return x * _sigmoid(x)


def _log_sigmoid(x):
    return jnp.minimum(x, 0.0) - jnp.log1p(jnp.exp(-jnp.abs(x)))


def _bf16_terms(x):
    hi = x.astype(BF16)
    r1 = x - hi.astype(F32)
    mid = r1.astype(BF16)
    lo = (r1 - mid.astype(F32)).astype(BF16)
    return hi, mid, lo


def _split_dot_left(sel, x):
    return sum(jnp.dot(sel, t, preferred_element_type=F32) for t in _bf16_terms(x))


def _split_dot_right(x, sel):
    return sum(jnp.dot(t, sel, preferred_element_type=F32) for t in _bf16_terms(x))


def _split_dot_nt(sel, x):
    return sum(lax.dot_general(sel, t, _NT, preferred_element_type=F32) for t in _bf16_terms(x))


def _rope(y, tab):
    return (y * tab[0]
            + pltpu.roll(y, LANES - ROPE_HALF, 1) * tab[1]
            + pltpu.roll(y, ROPE_HALF, 1) * tab[2])


def _norm_rope_tile(x, gain, tab):
    lane = lax.broadcasted_iota(jnp.int32, x.shape, 1)
    left = lane < ATT_HEAD_DIM
    sq = x * x
    ss_l = jnp.sum(jnp.where(left, sq, 0.0), axis=-1, keepdims=True)
    ss_r = jnp.sum(jnp.where(left, 0.0, sq), axis=-1, keepdims=True)
    inv = jnp.where(left, lax.rsqrt(ss_l / ATT_HEAD_DIM + NORM_EPS),
                    lax.rsqrt(ss_r / ATT_HEAD_DIM + NORM_EPS))
    return _rope(x * inv * gain, tab)


def _rmsnorm_bf16(x, g):
    return (x * lax.rsqrt(jnp.mean(x * x, axis=-1, keepdims=True) + NORM_EPS) * g).astype(BF16)


def _in_proj_kernel(x_ref, xprev_ref, xnext_ref, g_ref, w_ref, b_ref, wvt_ref, bvt_ref,
                    wif_ref, bif_ref, cw_ref, tab_ref, gq_ref, gk_ref,
                    proj_ref, q4_ref, kv_ref, vta_ref, qct_ref, kc_ref, vtm_ref, grow_ref, ccol_ref,
                    xn_s, *, tiles_per_seq):
    tm = x_ref.shape[0]
    tile_in_seq = pl.program_id(0) % tiles_per_seq

    xn_s[0:IN_HALO, :] = _rmsnorm_bf16(xprev_ref[...], g_ref[...])
    xn_s[IN_HALO:IN_HALO + tm, :] = _rmsnorm_bf16(x_ref[...], g_ref[...])
    xn_s[IN_HALO + tm:, :] = _rmsnorm_bf16(xnext_ref[...], g_ref[...])
    xn = xn_s[IN_HALO:IN_HALO + tm, :]
    xn_halo = xn_s[...]

    def chunk(c0, width, lhs=None):
        lhs = xn if lhs is None else lhs

        def piece(p0):
            cols = slice(p0, p0 + IN_PIECE)
            return lambda: jnp.dot(lhs, w_ref[:, cols], preferred_element_type=F32) + b_ref[:, cols]
        return [piece(p0) for p0 in range(c0, c0 + width, IN_PIECE)]

    stages = []

    def store_plain(c0):
        def store(acc, p0):
            def run():
                proj_ref[:, p0:p0 + IN_PIECE] = acc.astype(proj_ref.dtype)
            return run
        return lambda accs: [store(acc, c0 + i * IN_PIECE) for i, acc in enumerate(accs)]

    plain = [(chunk(c0, IN_TN), store_plain(c0)) for c0 in range(0, P_WIDTH, IN_TN)]

    has_before = tile_in_seq > 0
    has_after = tile_in_seq < tiles_per_seq - 1
    row_id = lax.broadcasted_iota(jnp.int32, (tm, LANES), 0)
    kscale = M_HEAD_DIM ** -0.5
    lane_tiles = range(0, IN_PIECE, LANES)

    def conv_silu(c0):
        def conv(acc, p0, l0):
            cols = slice(p0 + l0, p0 + l0 + LANES)

            def run():
                a = acc[IN_HALO:IN_HALO + tm, l0:l0 + LANES]
                before = jnp.where(has_before, acc[IN_HALO - 1:IN_HALO, l0:l0 + LANES], 0.0)
                after = jnp.where(has_after, acc[IN_HALO + tm:IN_HALO + tm + 1, l0:l0 + LANES], 0.0)
                xm = jnp.where(row_id == 0, before, pltpu.roll(a, 1, 0))
                xp = jnp.where(row_id == tm - 1, after, pltpu.roll(a, tm - 1, 0))
                y = _silu(xm * cw_ref[0:1, cols] + a * cw_ref[1:2, cols] + xp * cw_ref[2:3, cols])
                if p0 < M_WIDTH:
                    qct_ref[cols, :] = y.T.astype(BF16)
                else:
                    kc_ref[:, p0 + l0 - M_WIDTH:p0 + l0 - M_WIDTH + LANES] = (y * kscale).astype(BF16)
            return run
        return lambda accs: [conv(acc, c0 + i * IN_PIECE, l0) for i, acc in enumerate(accs) for l0 in lane_tiles]

    conv = [(chunk(W_MQ + c0, IN_TN, lhs=xn_halo), conv_silu(c0)) for c0 in range(0, 2 * M_WIDTH, IN_TN)]

    tab = tab_ref[...]

    def q_rope(c0):
        def rope(acc, p0, l0):
            def run():
                y = _norm_rope_tile(acc[:, l0:l0 + LANES], gq_ref[...], tab).astype(BF16)
                for j in range(tm // ATT_BLOCK):
                    q4_ref[j, (p0 + l0) // LANES] = y[j * ATT_BLOCK:(j + 1) * ATT_BLOCK, :]
            return run
        return lambda accs: [rope(acc, c0 + i * IN_PIECE, l0) for i, acc in enumerate(accs) for l0 in lane_tiles]

    att_q = [(chunk(W_AQ + c0, IN_TN), q_rope(c0)) for c0 in range(0, ATT_WIDTH, IN_TN)]

    left_half = lax.broadcasted_iota(jnp.int32, (tm, LANES), 1) < ATT_HEAD_DIM

    def k_norm_rope(accs):
        def norm_rope(acc, p0):
            def run():
                for t in range(IN_PIECE // LANES):
                    kt = _norm_rope_tile(acc[:, t * LANES:(t + 1) * LANES], gk_ref[...], tab)
                    sw = pltpu.roll(kt, ATT_HEAD_DIM, 1)
                    head = (p0 // LANES + t) * 2
                    variants = ((head, 0, jnp.where(left_half, kt, 0.0)), (head, 1, jnp.where(left_half, 0.0, sw)),
                                (head + 1, 0, jnp.where(left_half, sw, 0.0)), (head + 1, 1, jnp.where(left_half, 0.0, kt)))
                    for h, v, val in variants:
                        col = (2 * h + v) * LANES
                        kv_ref[:, col:col + LANES] = val.astype(BF16)
            return run
        return [norm_rope(acc, i * IN_PIECE) for i, acc in enumerate(accs)]

    att_k = (chunk(W_AK, ATT_KV_WIDTH), k_norm_rope)

    def vt_matmul():
        return lax.dot_general(wvt_ref[...], xn, _NT, preferred_element_type=F32) + bvt_ref[...]

    def store_vt(vt, r0):
        def run():
            if r0 < M_WIDTH:
                vtm_ref[r0:r0 + VT_TN, :] = vt[r0:r0 + VT_TN, :].astype(BF16)
            else:
                vta_ref[r0 - M_WIDTH:r0 - M_WIDTH + VT_TN, :] = vt[r0:r0 + VT_TN, :].astype(BF16)
        return run

    def gate_matmul():
        return jnp.dot(xn, wif_ref[...], preferred_element_type=F32) + bif_ref[...]

    def vt_gate_epilogues(results):
        vt, gc = results
        gr = vt[VT_ROWS:VT_ROWS + N_IFG, :]
        return [store_vt(vt, r0) for r0 in range(0, VT_ROWS, VT_TN)] + [lambda: gate_epilogue(gc, gr)]

    def gate_epilogue(gc, gr):
        ls_c = _log_sigmoid(gc)
        ls_r = _log_sigmoid(gr)
        r = lax.broadcasted_iota(jnp.int32, (M_CHUNK, M_CHUNK), 0)
        c = lax.broadcasted_iota(jnp.int32, (M_CHUNK, M_CHUNK), 1)
        tril = jnp.where(c <= r, 1.0, 0.0).astype(BF16)
        triu = jnp.where(c >= r, 1.0, 0.0).astype(BF16)
        lane = lax.broadcasted_iota(jnp.int32, (M_CHUNK, LANES), 1)
        fwd_lanes = lane < 2 * M_HEADS
        grow_ref[ROW_LI_F:ROW_LI_F + M_HEADS, :] = gr[0:M_HEADS, :]
        grow_ref[ROW_LI_B:ROW_LI_B + M_HEADS, :] = gr[2 * M_HEADS:3 * M_HEADS, :]
        for ci in range(tm // M_CHUNK):
            rows = slice(ci * M_CHUNK, (ci + 1) * M_CHUNK)
            lsc = ls_c[rows, :]
            b_f = _split_dot_left(tril, lsc)
            b_b = _split_dot_left(triu, lsc)
            cc = gc[rows, :] - pltpu.roll(jnp.where(fwd_lanes, b_f, b_b), LANES - M_HEADS, 1)
            for ti, term in enumerate(_bf16_terms(cc)):
                ccol_ref[ti, rows, :] = term
            grow_ref[ROW_B_F:ROW_B_F + M_HEADS, rows] = _split_dot_right(ls_r[M_HEADS:2 * M_HEADS, rows], triu)
            grow_ref[ROW_B_B:ROW_B_B + M_HEADS, rows] = _split_dot_right(ls_r[3 * M_HEADS:4 * M_HEADS, rows], tril)

    light = plain
    heavy = [([vt_matmul, gate_matmul], vt_gate_epilogues),
             att_q[0], conv[0], att_q[1], conv[1], att_k, conv[2], conv[3]]
    stages.append(heavy[0])
    for i, stage in enumerate(heavy[1:]):
        stages.extend([stage, light[i]])
    stages.extend(light[len(heavy) - 1:])

    n_matmuls = sum(len(matmuls) for matmuls, _ in stages)
    queue = []
    emitted = 0
    for matmuls, make_epilogues in stages:
        results = []
        for matmul in matmuls:
            results.append(matmul())
            emitted += 1
            take = max(IN_EPILOGUES_PER_MATMUL, -(-len(queue) // max(n_matmuls - emitted, 1))) if queue else 0
            for run in queue[:take]:
                run()
            queue = queue[take:]
        queue.extend(make_epilogues(results))
    for run in queue:
        run()


def _in_proj(x, lw, rope_tab, seq):
    t = x.shape[0]
    tps = seq // IN_TM
    halo_blocks = IN_TM // IN_HALO
    const = lambda i: (0, 0)
    tile = lambda i: (i, 0)
    cols = lambda i: (0, i)
    resident = dict(pipeline_mode=pl.Buffered(1))
    return pl.pallas_call(
        functools.partial(_in_proj_kernel, tiles_per_seq=tps),
        grid=(t // IN_TM,),
        in_specs=[
            pl.BlockSpec((IN_TM, D_MODEL), tile),
            pl.BlockSpec((IN_HALO, D_MODEL), lambda i: (jnp.maximum(i * halo_blocks - 1, 0), 0)),
            pl.BlockSpec((IN_HALO, D_MODEL), lambda i: (jnp.minimum((i + 1) * halo_blocks, t // IN_HALO - 1), 0)),
            pl.BlockSpec((1, D_MODEL), const),
            pl.BlockSpec((D_MODEL, W_WIDTH), const, **resident),
            pl.BlockSpec((1, W_WIDTH), const),
            pl.BlockSpec((VT_ROWS + N_IFG, D_MODEL), const, **resident),
            pl.BlockSpec((VT_ROWS + N_IFG, 1), const),
            pl.BlockSpec((D_MODEL, LANES), const),
            pl.BlockSpec((1, LANES), const),
            pl.BlockSpec((CONV_K, 2 * M_WIDTH), const),
            pl.BlockSpec((3, IN_TM, LANES), lambda i: (0, i % tps, 0)),
            pl.BlockSpec((1, LANES), const),
            pl.BlockSpec((1, LANES), const),
        ],
        out_specs=[
            pl.BlockSpec((IN_TM, P_WIDTH), tile),
            pl.BlockSpec((IN_TM // ATT_BLOCK, ATT_PAIRS, ATT_BLOCK, LANES), lambda i: (i, 0, 0, 0)),
            pl.BlockSpec((IN_TM, ATT_KVAR_WIDTH), tile),
            pl.BlockSpec((ATT_KV_WIDTH, IN_TM), cols),
            pl.BlockSpec((M_WIDTH, IN_TM), cols),
            pl.BlockSpec((IN_TM, M_WIDTH), tile),
            pl.BlockSpec((M_WIDTH, IN_TM), cols),
            pl.BlockSpec((N_IFG, IN_TM), cols),
            pl.BlockSpec((3, IN_TM, LANES), lambda i: (0, i, 0)),
        ],
        out_shape=[
            jax.ShapeDtypeStruct((t, P_WIDTH), PROJ_DTYPE),
            jax.ShapeDtypeStruct((t // ATT_BLOCK, ATT_PAIRS, ATT_BLOCK, LANES), BF16),
            jax.ShapeDtypeStruct((t, ATT_KVAR_WIDTH), BF16),
            jax.ShapeDtypeStruct((ATT_KV_WIDTH, t), BF16),
            jax.ShapeDtypeStruct((M_WIDTH, t), BF16),
            jax.ShapeDtypeStruct((t, M_WIDTH), BF16),
            jax.ShapeDtypeStruct((M_WIDTH, t), BF16),
            jax.ShapeDtypeStruct((N_IFG, t), F32),
            jax.ShapeDtypeStruct((3, t, LANES), BF16),
        ],
        scratch_shapes=[pltpu.VMEM((IN_TM + 2 * IN_HALO, D_MODEL), BF16)],
        compiler_params=pltpu.CompilerParams(
            dimension_semantics=("arbitrary",), vmem_limit_bytes=VMEM_LIMIT_BYTES),
        name="in_proj",
    )(x, x, x, lw["norm_g"], lw["w_main"], lw["b_main"], lw["w_vt"], lw["b_vt"], lw["w_if"],
      lw["b_if"], lw["conv_w"], rope_tab, lw["gq"], lw["gk"])


def _attention_kernel(q4_ref, z_ref, kp_ref, kc_ref, kn_ref, vp_ref, vc_ref, vn_ref, sink_ref, out_ref,
                      ks, vts, out_t, *, n_blocks):
    ti = pl.program_id(1)
    bq = ATT_BLOCK

    r0 = 0
    for k_ref, v_ref in ((kp_ref, vp_ref), (kc_ref, vc_ref), (kn_ref, vn_ref)):
        nr = k_ref.shape[0]
        ks[r0:r0 + nr, :] = k_ref[...]
        vts[:, r0:r0 + nr] = v_ref[...]
        r0 += nr

    def group_heads(h):
        return (ATT_GROUP * h, ATT_GROUP * h + 2, ATT_GROUP * h + 1, ATT_GROUP * h + 3)

    sink_rows = [jnp.concatenate([jnp.broadcast_to(sink_ref[:, c:c + 1] * LOG2E, (1, bq)) for c in group_heads(h)],
                                 axis=1) for h in range(ATT_KV_HEADS)]

    kk = lax.broadcasted_iota(jnp.int32, (bq, bq), 0)
    qq = lax.broadcasted_iota(jnp.int32, (bq, bq), 1)
    prev_in_band = kk >= qq
    next_in_band = kk <= qq

    def sub_block(j, carry):
        blk = ti * ATT_NSB + j
        bias_prev = jnp.where(prev_in_band, jnp.where(blk > 0, 0.0, NEG), NEG).astype(F32)
        bias_next = jnp.where(next_in_band, jnp.where(blk < n_blocks - 1, 0.0, NEG), NEG).astype(F32)
        bias_prev = jnp.concatenate([bias_prev] * ATT_GROUP, axis=1)
        bias_next = jnp.concatenate([bias_next] * ATT_GROUP, axis=1)
        row0 = pl.multiple_of(j * bq, bq)
        q_slabs = q4_ref[j]
        hs = range(ATT_KV_HEADS)
        q_pairs = [q_slabs[2 * h:2 * h + 2].reshape(2 * bq, LANES) for h in hs]
        ss = [jnp.concatenate(
            [lax.dot_general(ks[pl.ds(row0, 3 * bq), (2 * h + v) * LANES:(2 * h + v + 1) * LANES], q_pairs[h], _NT,
                             preferred_element_type=F32)
             for v in range(2)], axis=1) for h in hs]
        ss = [jnp.concatenate([s[:bq] + bias_prev, s[bq:2 * bq], s[2 * bq:] + bias_next], axis=0) for s in ss]
        ms = [jnp.maximum(jnp.max(s, axis=0, keepdims=True), sink_rows[h]) for h, s in zip(hs, ss)]
        ps = [jnp.exp2(s - m) for s, m in zip(ss, ms)]
        denoms = [jnp.sum(p, axis=0, keepdims=True) + jnp.exp2(sink_rows[h] - m) for h, p, m in zip(hs, ps, ms)]
        os_ = [jnp.dot(vts[h * ATT_HEAD_DIM:(h + 1) * ATT_HEAD_DIM, pl.ds(row0, 3 * bq)], p.astype(BF16),
                       preferred_element_type=F32) * (1.0 / denom)
               for h, p, denom in zip(hs, ps, denoms)]
        for h, o in zip(hs, os_):
            for i, c in enumerate(group_heads(h)):
                out_t[c * ATT_HEAD_DIM:(c + 1) * ATT_HEAD_DIM, :] = o[:, i * bq:(i + 1) * bq]
        att = out_t[...].T
        z = z_ref[pl.ds(row0, bq), :].astype(F32)
        out_ref[pl.ds(row0, bq), :] = (att * _silu(z)).astype(out_ref.dtype)
        return carry

    lax.fori_loop(0, ATT_NSB, sub_block, 0)


def _attention(proj, q4, kv, vta, sink, batch, seq):
    t = proj.shape[0]
    nb = seq // ATT_BLOCK
    nt = seq // ATT_TQ
    bq = ATT_BLOCK

    prev_blk = lambda b, i: b * nb + jnp.maximum(i * ATT_NSB - 1, 0)
    next_blk = lambda b, i: b * nb + jnp.minimum((i + 1) * ATT_NSB, nb - 1)
    return pl.pallas_call(
        functools.partial(_attention_kernel, n_blocks=nb),
        grid=(batch, nt),
        in_specs=[
            pl.BlockSpec((ATT_NSB, ATT_PAIRS, bq, LANES), lambda b, i: (b * nt + i, 0, 0, 0)),
            pl.BlockSpec((ATT_TQ, ATT_WIDTH), lambda b, i: (b * nt + i, P_AZ // ATT_WIDTH)),
            pl.BlockSpec((bq, ATT_KVAR_WIDTH), lambda b, i: (prev_blk(b, i), 0)),
            pl.BlockSpec((ATT_TQ, ATT_KVAR_WIDTH), lambda b, i: (b * nt + i, 0)),
            pl.BlockSpec((bq, ATT_KVAR_WIDTH), lambda b, i: (next_blk(b, i), 0)),
            pl.BlockSpec((ATT_KV_WIDTH, bq), lambda b, i: (0, prev_blk(b, i))),
            pl.BlockSpec((ATT_KV_WIDTH, ATT_TQ), lambda b, i: (0, b * nt + i)),
            pl.BlockSpec((ATT_KV_WIDTH, bq), lambda b, i: (0, next_blk(b, i))),
            pl.BlockSpec((1, ATT_HEADS), lambda b, i: (0, 0)),
        ],
        out_specs=pl.BlockSpec((ATT_TQ, ATT_WIDTH), lambda b, i: (b * nt + i, 0)),
        out_shape=jax.ShapeDtypeStruct((t, ATT_WIDTH), BF16),
        scratch_shapes=[
            pltpu.VMEM((ATT_TQ + 2 * bq, ATT_KVAR_WIDTH), BF16),
            pltpu.VMEM((ATT_KV_WIDTH, ATT_TQ + 2 * bq), BF16),
            pltpu.VMEM((ATT_WIDTH, bq), F32),
        ],
        compiler_params=pltpu.CompilerParams(
            dimension_semantics=("arbitrary", "arbitrary"), vmem_limit_bytes=VMEM_LIMIT_BYTES),
        name="attention",
    )(q4, proj, kv, kv, kv, vta, vta, vta, sink)


def _mlstm_kernel(qct_ref, kc_ref, vt_ref, o_ref, z_ref, grow_ref, ccol_ref, ng_ref,
                  out_ref, cct_s, nf_s, nb_s, mf_s, mb_s, *, seq):
    hd = pl.program_id(1)
    L = M_CHUNK
    dv = M_HEAD_DIM
    nc = seq // L

    head_row = lax.broadcasted_iota(jnp.int32, (M_HEADS, L), 0) == hd

    def gate_row(base, t0):
        rows = grow_ref[base:base + M_HEADS, pl.ds(t0, L)]
        return jnp.sum(jnp.where(head_row, rows, 0.0), axis=0, keepdims=True)

    U = MLSTM_SCAN_GROUP
    dirs = ((ROW_LI_F, ROW_B_F, L - 1, nf_s, mf_s, 0), (ROW_LI_B, ROW_B_B, 0, nb_s, mb_s, dv))

    def scan_body(gj, carry):
        cis = [[gj * U + u for u in range(U)], [nc - 1 - (gj * U + u) for u in range(U)]]
        t0s = [[pl.multiple_of(ci * L, L) for ci in row] for row in cis]
        lis = [[gate_row(dirs[d][0], t0) for t0 in t0s[d]] for d in range(2)]
        bs = [[gate_row(dirs[d][1], t0) for t0 in t0s[d]] for d in range(2)]
        b_lasts = [[b[:, dirs[d][2]:dirs[d][2] + 1] for b in bs[d]] for d in range(2)]
        gs = [[b_last - b + li for b_last, b, li in zip(b_lasts[d], bs[d], lis[d])] for d in range(2)]
        g_maxs = [[jnp.max(g, axis=-1, keepdims=True) for g in gs[d]] for d in range(2)]
        ks = [[kc_ref[pl.ds(t0, L), :] for t0 in t0s[d]] for d in range(2)]
        vts = [[vt_ref[:, pl.ds(t0, L)].astype(F32) for t0 in t0s[d]] for d in range(2)]

        ms = [[carry[d][2]] for d in range(2)]
        for d in range(2):
            for u in range(U):
                ms[d].append(jnp.maximum(b_lasts[d][u] + ms[d][u], g_maxs[d][u]))
        w_cs = [[jnp.exp(b_lasts[d][u] + ms[d][u] - ms[d][u + 1]) for u in range(U)] for d in range(2)]
        w_ks = [[jnp.exp(gs[d][u] - ms[d][u + 1]) for u in range(U)] for d in range(2)]
        c_upds = [[jnp.dot((vts[d][u] * w_ks[d][u]).astype(BF16), ks[d][u], preferred_element_type=F32)
                   for u in range(U)] for d in range(2)]
        n_upds = [[jnp.dot(jnp.broadcast_to(w_ks[d][u], (SUBLANES, L)).astype(BF16), ks[d][u],
                           preferred_element_type=F32)[0:1, :] for u in range(U)] for d in range(2)]

        out = []
        for d in range(2):
            _, _, _, n_s, m_s, row0 = dirs[d]
            ct_st, n_st, _ = carry[d]
            for u in range(U):
                ci = cis[d][u]
                cct_s[ci, row0:row0 + dv, :] = ct_st.astype(BF16)
                n_s[ci] = jnp.broadcast_to(n_st, (SUBLANES, LANES))
                m_s[ci] = jnp.broadcast_to(ms[d][u], (SUBLANES, LANES))
                ct_st = w_cs[d][u] * ct_st + c_upds[d][u]
                n_st = w_cs[d][u] * n_st + n_upds[d][u]
            out.append((ct_st, n_st, ms[d][U]))
        return tuple(out)

    init = (jnp.zeros((dv, M_HEAD_DIM), F32), jnp.zeros((1, M_HEAD_DIM), F32),
            jnp.full((1, 1), NEG, F32))
    lax.fori_loop(0, nc // U, scan_body, (init, init))

    rr = lax.broadcasted_iota(jnp.int32, (L, L), 0)
    cc = lax.broadcasted_iota(jnp.int32, (L, L), 1)
    row8 = lax.broadcasted_iota(jnp.int32, (SUBLANES, LANES), 0)
    sel_k = lax.broadcasted_iota(jnp.int32, (LANES, 2 * L), 0)
    sel_n = lax.broadcasted_iota(jnp.int32, (LANES, 2 * L), 1)
    sel_lane = jnp.where(sel_n < L, LANE_C_F + hd, LANE_C_B + hd)
    sel = jnp.where(sel_k == sel_lane, 1.0, 0.0).astype(BF16)
    keep_f = rr <= cc
    keep_b = rr >= cc
    G = MLSTM_OUT_GROUP

    def out_body(gi, carry):
        cis = [gi * G + g for g in range(G)]
        t0s = [pl.multiple_of(ci * L, L) for ci in cis]
        qts = [qct_ref[:, pl.ds(t0, L)] for t0 in t0s]
        qk_ts = [jnp.dot(kc_ref[pl.ds(t0, L), :], qt, preferred_element_type=F32)
                 for t0, qt in zip(t0s, qts)]
        cqs = [jnp.dot(cct_s[ci], qt, preferred_element_type=F32) for ci, qt in zip(cis, qts)]
        n8s = [jnp.where(row8 == 0, nf_s[ci], jnp.where(row8 == 1, nb_s[ci], 0.0)).astype(BF16) for ci in cis]
        qns = [jnp.dot(n8, qt, preferred_element_type=F32) for n8, qt in zip(n8s, qts)]
        c_bcs = [sum(jnp.dot(ccol_ref[ti, pl.ds(t0, L), :], sel, preferred_element_type=F32) for ti in range(3))
                 for t0 in t0s]

        items = []
        for g in range(G):
            items.append((g, c_bcs[g][:, :L], qns[g][0:1, :], ROW_B_F, keep_f, mf_s))
            items.append((g, c_bcs[g][:, L:], qns[g][1:2, :], ROW_B_B, keep_b, mb_s))
        b_rs = [gate_row(b_base, t0s[g]) for g, _, _, b_base, _, _ in items]
        m_prevs = [m_s[cis[g]][0:1, 0:1] for g, _, _, _, _, m_s in items]
        ds = [jnp.where(keep, b_r + c_s, NEG)
              for (_, c_s, _, _, keep, _), b_r in zip(items, b_rs)]
        m_inters = [b_r + m_prev for b_r, m_prev in zip(b_rs, m_prevs)]
        m_ts = [jnp.maximum(m_inter, jnp.max(d, axis=0, keepdims=True)) for m_inter, d in zip(m_inters, ds)]
        w_inters = [jnp.exp(m_inter - m_t) for m_inter, m_t in zip(m_inters, m_ts)]
        a_s = [jnp.exp(d - m_t) * qk_ts[it[0]] for d, m_t, it in zip(ds, m_ts, items)]
        dens = [w_inter * it[2] + jnp.sum(a, axis=0, keepdims=True) for w_inter, it, a in zip(w_inters, items, a_s)]
        invs = [1.0 / jnp.maximum(jnp.abs(den), jnp.exp(-m_t)) for den, m_t in zip(dens, m_ts)]
        ps = [a * inv for a, inv in zip(a_s, invs)]
        ws = [w_inter * inv for w_inter, inv in zip(w_inters, invs)]

        h_ts = [jnp.dot(vt_ref[:, pl.ds(t0s[g], L)], (ps[2 * g] + ps[2 * g + 1]).astype(BF16),
                        preferred_element_type=F32)
                + cqs[g][:dv, :] * ws[2 * g] + cqs[g][dv:, :] * ws[2 * g + 1] for g in range(G)]
        hs = [_sigmoid(o_ref[pl.ds(t0, L), :].astype(F32)) * h_t.T for t0, h_t in zip(t0s, h_ts)]
        hs = [h * lax.rsqrt(jnp.mean(h * h, axis=-1, keepdims=True) + NORM_EPS) * ng_ref[...] for h in hs]
        for t0, h in zip(t0s, hs):
            out_ref[pl.ds(t0, L), :] = (h * _silu(z_ref[pl.ds(t0, L), :].astype(F32))).astype(out_ref.dtype)
        return carry

    lax.fori_loop(0, nc // G, out_body, 0)


def _mlstm(proj, qct, kc, vtm, grow, ccol, ng, batch, seq):
    t = proj.shape[0]
    nc = seq // M_CHUNK
    head_rows = lambda b, h: (h, b)
    head_cols = lambda b, h: (b, h)

    def proj_head(col0):
        return pl.BlockSpec((seq, M_HEAD_DIM), lambda b, h: (b, col0 // M_HEAD_DIM + h))

    return pl.pallas_call(
        functools.partial(_mlstm_kernel, seq=seq),
        grid=(batch, M_HEADS),
        in_specs=[
            pl.BlockSpec((M_HEAD_DIM, seq), head_rows),
            pl.BlockSpec((seq, M_HEAD_DIM), head_cols),
            pl.BlockSpec((M_HEAD_DIM, seq), head_rows),
            proj_head(P_MO), proj_head(P_MZ),
            pl.BlockSpec((N_IFG, seq), lambda b, h: (0, b)),
            pl.BlockSpec((3, seq, LANES), lambda b, h: (0, b, 0)),
            pl.BlockSpec((1, M_HEAD_DIM), lambda b, h: (0, h)),
        ],
        out_specs=pl.BlockSpec((seq, M_HEAD_DIM), head_cols),
        out_shape=jax.ShapeDtypeStruct((t, M_WIDTH), BF16),
        scratch_shapes=[
            pltpu.VMEM((nc, 2 * M_HEAD_DIM, M_HEAD_DIM), BF16),
            pltpu.VMEM((nc, SUBLANES, M_HEAD_DIM), F32),
            pltpu.VMEM((nc, SUBLANES, M_HEAD_DIM), F32),
            pltpu.VMEM((nc, SUBLANES, LANES), F32),
            pltpu.VMEM((nc, SUBLANES, LANES), F32),
        ],
        compiler_params=pltpu.CompilerParams(
            dimension_semantics=("arbitrary", "arbitrary"), vmem_limit_bytes=VMEM_LIMIT_BYTES),
        name="mlstm",
    )(qct, kc, vtm, proj, proj, grow, ccol, ng)


def _out_proj_kernel(x_ref, a_ref, m_ref, gates_ref, wa_ref, wm_ref, wo_ref, out_ref):
    branch_a = jnp.dot(a_ref[...], wa_ref[...], preferred_element_type=F32)
    branch_m = jnp.dot(m_ref[...], wm_ref[...], preferred_element_type=F32)
    gates = _sigmoid(gates_ref[...].astype(F32))
    merged = gates[:, :D_MODEL] * branch_a + gates[:, D_MODEL:] * branch_m
    out_ref[...] = x_ref[...] + jnp.dot(merged.astype(BF16), wo_ref[...], preferred_element_type=F32)


def _out_proj(x, a, m, proj, wa, wm, wo):
    t = x.shape[0]
    tile = lambda i: (i, 0)
    const = lambda i: (0, 0)
    return pl.pallas_call(
        _out_proj_kernel,
        grid=(t // OUT_TM,),
        in_specs=[
            pl.BlockSpec((OUT_TM, D_MODEL), tile),
            pl.BlockSpec((OUT_TM, ATT_WIDTH), tile),
            pl.BlockSpec((OUT_TM, M_WIDTH), tile),
            pl.BlockSpec((OUT_TM, 2 * D_MODEL), lambda i: (i, P_GATES // (2 * D_MODEL))),
            pl.BlockSpec((ATT_WIDTH, D_MODEL), const),
            pl.BlockSpec((M_WIDTH, D_MODEL), const),
            pl.BlockSpec((D_MODEL, D_MODEL), const),
        ],
        out_specs=pl.BlockSpec((OUT_TM, D_MODEL), tile),
        out_shape=jax.ShapeDtypeStruct((t, D_MODEL), F32),
        compiler_params=pltpu.CompilerParams(
            dimension_semantics=("arbitrary",), vmem_limit_bytes=VMEM_LIMIT_BYTES),
        name="out_proj",
    )(x, a, m, proj, wa, wm, wo)


def _rope_tables(seq):
    inv = jnp.power(jnp.float32(ROPE_THETA), -jnp.arange(ROPE_HALF, dtype=F32) * 2.0 / ROPE_DIM)
    ang = jnp.arange(seq, dtype=F32)[:, None] * inv[None, :]
    cos, sin = jnp.cos(ang), jnp.sin(ang)
    pad = ATT_HEAD_DIM - ROPE_DIM
    ones = jnp.ones((seq, pad), F32)
    zeros = jnp.zeros((seq, pad), F32)
    zh = jnp.zeros((seq, ROPE_HALF), F32)
    c = jnp.concatenate([cos, cos, ones], axis=-1)
    s1 = jnp.concatenate([-sin, zh, zeros], axis=-1)
    s2 = jnp.concatenate([zh, sin, zeros], axis=-1)
    tab = jnp.stack([c, s1, s2])
    return jnp.concatenate([tab, tab], axis=-1)


def _prep_weights(w_in, b_in):
    order = [(_R_GATES, 2 * D_MODEL), (_R_AZ, ATT_WIDTH), (_R_MO, M_WIDTH), (_R_MZ, M_WIDTH),
             (_R_MQ, M_WIDTH), (_R_MK, M_WIDTH), (_R_AQ, ATT_WIDTH), (_R_AK, ATT_KV_WIDTH)]
    w_main = jnp.concatenate([w_in[:, :, o:o + n] for o, n in order], axis=-1).astype(BF16)
    b_main = jnp.concatenate([b_in[:, o:o + n] for o, n in order], axis=-1)[:, None, :]
    vt_order = [(_R_MV, M_WIDTH), (_R_AV, ATT_KV_WIDTH), (_R_IF, N_IFG)]
    w_vt = jnp.swapaxes(jnp.concatenate([w_in[:, :, o:o + n] for o, n in vt_order], axis=-1), 1, 2).astype(BF16)
    b_vt = jnp.concatenate([b_in[:, o:o + n] for o, n in vt_order], axis=-1)[:, :, None]
    w_if = w_in[:, :, _R_IF:_R_IF + N_IFG]
    b_if = b_in[:, _R_IF:_R_IF + N_IFG]
    w_if_pad = jnp.pad(w_if, ((0, 0), (0, 0), (0, LANES - N_IFG))).astype(BF16)
    b_if_pad = jnp.pad(b_if, ((0, 0), (0, LANES - N_IFG)))[:, None, :]
    return w_main, b_main, w_vt, b_vt, w_if_pad, b_if_pad


def _trunk(x, rope_tab, layers):
    batch, seq, _ = x.shape
    xf = x.reshape(batch * seq, D_MODEL)
    for lw in layers:
        proj, q4, kv, vta, qct, kc, vtm, grow, ccol = _in_proj(xf, lw, rope_tab, seq)
        a = _attention(proj, q4, kv, vta, lw["sink"], batch, seq)
        m = _mlstm(proj, qct, kc, vtm, grow, ccol, lw["m_norm_g"], batch, seq)
        xf = _out_proj(xf, a, m, proj, lw["w_att_out"], lw["w_m_out"], lw["w_out"])
    return xf.reshape(batch, seq, D_MODEL)


def kernel(x_prompt, x_sample, norm_g, w_in, b_in, q_norm_g, k_norm_g, sink, conv_w, m_norm_g,
           w_att_out, w_m_out, w_out):
    w_main, b_main, w_vt, b_vt, w_if, b_if = _prep_weights(w_in, b_in)
    wa, wm, wo = w_att_out.astype(BF16), w_m_out.astype(BF16), w_out.astype(BF16)
    layers = []
    for l in range(DEPTH):
        layers.append(dict(
            norm_g=norm_g[l][None, :], w_main=w_main[l], b_main=b_main[l], w_vt=w_vt[l], b_vt=b_vt[l],
            w_if=w_if[l], b_if=b_if[l],
            gq=jnp.tile(q_norm_g[l] * (ATT_HEAD_DIM ** -0.5 * LOG2E), 2)[None, :],
            gk=jnp.tile(k_norm_g[l], 2)[None, :],
            sink=sink[l][None, :], conv_w=conv_w[l], m_norm_g=m_norm_g[l][None, :],
            w_att_out=wa[l], w_m_out=wm[l], w_out=wo[l]))
    outs = []
    for x in (x_prompt, x_sample):
        outs.append(_trunk(x, _rope_tables(x.shape[1]), layers))
    return tuple(outs)
```

```python
import functools
import math

import jax
import jax.numpy as jnp
from jax import lax
from jax.experimental import pallas as pl
from jax.experimental.pallas import tpu as pltpu

F32 = jnp.float32
BF16 = jnp.bfloat16

D_MODEL = 1024
DEPTH = 4
ATT_HEADS = 16
ATT_KV_HEADS = 4
ATT_GROUP = ATT_HEADS // ATT_KV_HEADS
ATT_HEAD_DIM = 64
ATT_WIDTH = ATT_HEADS * ATT_HEAD_DIM
ATT_KV_WIDTH = ATT_KV_HEADS * ATT_HEAD_DIM
WINDOW = 128
ATT_BLOCK = 128
ROPE_THETA = 500000.0
ROPE_DIM = ATT_HEAD_DIM // 4
ROPE_HALF = ROPE_DIM // 2
M_HEADS = 8
M_HEAD_DIM = 128
M_WIDTH = M_HEADS * M_HEAD_DIM
M_CHUNK = 128
CONV_K = 3
NORM_EPS = 1e-6
NEG = -1e30
LOG2E = math.log2(math.e)

LANES = 128
SUBLANES = 8
VMEM_LIMIT_BYTES = 56 * 1024 * 1024
_REF_SPLITS = (ATT_WIDTH, ATT_KV_WIDTH, ATT_KV_WIDTH, ATT_WIDTH,
               M_WIDTH, M_WIDTH, M_WIDTH, M_WIDTH, M_WIDTH,
               M_HEADS, M_HEADS, M_HEADS, M_HEADS, 2 * D_MODEL)
_REF_OFF = [0]
for _w in _REF_SPLITS:
    _REF_OFF.append(_REF_OFF[-1] + _w)
(_R_AQ, _R_AK, _R_AV, _R_AZ, _R_MQ, _R_MK, _R_MV, _R_MO, _R_MZ,
 _R_IF, _R_FF, _R_IB, _R_FB, _R_GATES) = _REF_OFF[:-1]

P_GATES = 0
P_AZ = P_GATES + 2 * D_MODEL
P_MO = P_AZ + ATT_WIDTH
P_MZ = P_MO + M_WIDTH
P_WIDTH = P_MZ + M_WIDTH
W_MQ = P_WIDTH
W_MK = W_MQ + M_WIDTH
W_AQ = W_MK + M_WIDTH
W_AK = W_AQ + ATT_WIDTH
W_WIDTH = W_AK + ATT_KV_WIDTH
VT_ROWS = M_WIDTH + ATT_KV_WIDTH
N_IFG = 4 * M_HEADS

ROW_LI_F, ROW_B_F, ROW_LI_B, ROW_B_B = 0, M_HEADS, 2 * M_HEADS, 3 * M_HEADS
LANE_C_F, LANE_C_B = 0, 2 * M_HEADS

PROJ_DTYPE = BF16
IN_TM = 512
IN_TN = 512
IN_EPILOGUES_PER_MATMUL = 2
IN_HALO = 16
IN_PIECE = 256
VT_TN = 256
OUT_TM = 512
ATT_TQ = 512
ATT_NSB = ATT_TQ // ATT_BLOCK
ATT_SUB_GROUP = 4
ATT_PAIRS = ATT_WIDTH // LANES
ATT_KVAR_WIDTH = ATT_KV_HEADS * 2 * LANES
MLSTM_SCAN_GROUP = 16
MLSTM_OUT_GROUP = 16

_NT = (((1,), (1,)), ((), ()))


def _sigmoid(x):
    return 1.0 / (1.0 + jnp.exp(-x))


def _silu(x):
    return x * _sigmoid(x)


def _log_sigmoid(x):
    return jnp.minimum(x, 0.0) - jnp.log1p(jnp.exp(-jnp.abs(x)))


def _bf16_terms(x):
    hi = x.astype(BF16)
    r1 = x - hi.astype(F32)
    mid = r1.astype(BF16)
    lo = (r1 - mid.astype(F32)).astype(BF16)
    return hi, mid, lo


def _split_dot_left(sel, x):
    return sum(jnp.dot(sel, t, preferred_element_type=F32) for t in _bf16_terms(x))


def _split_dot_right(x, sel):
    return sum(jnp.dot(t, sel, preferred_element_type=F32) for t in _bf16_terms(x))


def _split_dot_nt(sel, x):
    return sum(lax.dot_general(sel, t, _NT, preferred_element_type=F32) for t in _bf16_terms(x))


def _rope(y, tab):
    return (y * tab[0]
            + pltpu.roll(y, LANES - ROPE_HALF, 1) * tab[1]
            + pltpu.roll(y, ROPE_HALF, 1) * tab[2])


def _norm_rope_tile(x, gain, tab):
    lane = lax.broadcasted_iota(jnp.int32, x.shape, 1)
    left = lane < ATT_HEAD_DIM
    sq = x * x
    ss_l = jnp.sum(jnp.where(left, sq, 0.0), axis=-1, keepdims=True)
    ss_r = jnp.sum(jnp.where(left, 0.0, sq), axis=-1, keepdims=True)
    inv = jnp.where(left, lax.rsqrt(ss_l / ATT_HEAD_DIM + NORM_EPS),
                    lax.rsqrt(ss_r / ATT_HEAD_DIM + NORM_EPS))
    return _rope(x * inv * gain, tab)


def _rmsnorm_bf16(x, g):
    return (x * lax.rsqrt(jnp.mean(x * x, axis=-1, keepdims=True) + NORM_EPS) * g).astype(BF16)


def _in_proj_kernel(x_ref, xprev_ref, xnext_ref, g_ref, w_ref, b_ref, wvt_ref, bvt_ref,
                    wif_ref, bif_ref, cw_ref, tab_ref, gq_ref, gk_ref,
                    proj_ref, q4_ref, kv_ref, vta_ref, qct_ref, kc_ref, vtm_ref, grow_ref, ccol_ref,
                    xn_s, *, tiles_per_seq):
    tm = x_ref.shape[0]
    tile_in_seq = pl.program_id(0) % tiles_per_seq

    xn_s[0:IN_HALO, :] = _rmsnorm_bf16(xprev_ref[...], g_ref[...])
    xn_s[IN_HALO:IN_HALO + tm, :] = _rmsnorm_bf16(x_ref[...], g_ref[...])
    xn_s[IN_HALO + tm:, :] = _rmsnorm_bf16(xnext_ref[...], g_ref[...])
    xn = xn_s[IN_HALO:IN_HALO + tm, :]
    xn_halo = xn_s[...]

    def chunk(c0, width, lhs=None):
        lhs = xn if lhs is None else lhs

        def piece(p0):
            cols = slice(p0, p0 + IN_PIECE)
            return lambda: jnp.dot(lhs, w_ref[:, cols], preferred_element_type=F32) + b_ref[:, cols]
        return [piece(p0) for p0 in range(c0, c0 + width, IN_PIECE)]

    stages = []

    def store_plain(c0):
        def store(acc, p0):
            def run():
                proj_ref[:, p0:p0 + IN_PIECE] = acc.astype(proj_ref.dtype)
            return run
        return lambda accs: [store(acc, c0 + i * IN_PIECE) for i, acc in enumerate(accs)]

    plain = [(chunk(c0, IN_TN), store_plain(c0)) for c0 in range(0, P_WIDTH, IN_TN)]

    has_before = tile_in_seq > 0
    has_after = tile_in_seq < tiles_per_seq - 1
    row_id = lax.broadcasted_iota(jnp.int32, (tm, LANES), 0)
    kscale = M_HEAD_DIM ** -0.5
    lane_tiles = range(0, IN_PIECE, LANES)

    def conv_silu(c0):
        def conv(acc, p0, l0):
            cols = slice(p0 + l0, p0 + l0 + LANES)

            def run():
                a = acc[IN_HALO:IN_HALO + tm, l0:l0 + LANES]
                before = jnp.where(has_before, acc[IN_HALO - 1:IN_HALO, l0:l0 + LANES], 0.0)
                after = jnp.where(has_after, acc[IN_HALO + tm:IN_HALO + tm + 1, l0:l0 + LANES], 0.0)
                xm = jnp.where(row_id == 0, before, pltpu.roll(a, 1, 0))
                xp = jnp.where(row_id == tm - 1, after, pltpu.roll(a, tm - 1, 0))
                y = _silu(xm * cw_ref[0:1, cols] + a * cw_ref[1:2, cols] + xp * cw_ref[2:3, cols])
                if p0 < M_WIDTH:
                    qct_ref[cols, :] = y.T.astype(BF16)
                else:
                    kc_ref[:, p0 + l0 - M_WIDTH:p0 + l0 - M_WIDTH + LANES] = (y * kscale).astype(BF16)
            return run
        return lambda accs: [conv(acc, c0 + i * IN_PIECE, l0) for i, acc in enumerate(accs) for l0 in lane_tiles]

    conv = [(chunk(W_MQ + c0, IN_TN, lhs=xn_halo), conv_silu(c0)) for c0 in range(0, 2 * M_WIDTH, IN_TN)]

    tab = tab_ref[...]

    def q_rope(c0):
        def rope(acc, p0, l0):
            def run():
                y = _norm_rope_tile(acc[:, l0:l0 + LANES], gq_ref[...], tab).astype(BF16)
                for j in range(tm // ATT_BLOCK):
                    q4_ref[j, (p0 + l0) // LANES] = y[j * ATT_BLOCK:(j + 1) * ATT_BLOCK, :]
            return run
        return lambda accs: [rope(acc, c0 + i * IN_PIECE, l0) for i, acc in enumerate(accs) for l0 in lane_tiles]

    att_q = [(chunk(W_AQ + c0, IN_TN), q_rope(c0)) for c0 in range(0, ATT_WIDTH, IN_TN)]

    left_half = lax.broadcasted_iota(jnp.int32, (tm, LANES), 1) < ATT_HEAD_DIM

    def k_norm_rope(accs):
        def norm_rope(acc, p0):
            def run():
                for t in range(IN_PIECE // LANES):
                    kt = _norm_rope_tile(acc[:, t * LANES:(t + 1) * LANES], gk_ref[...], tab)
                    sw = pltpu.roll(kt, ATT_HEAD_DIM, 1)
                    head = (p0 // LANES + t) * 2
                    variants = ((head, 0, jnp.where(left_half, kt, 0.0)), (head, 1, jnp.where(left_half, 0.0, sw)),
                                (head + 1, 0, jnp.where(left_half, sw, 0.0)), (head + 1, 1, jnp.where(left_half, 0.0, kt)))
                    for h, v, val in variants:
                        col = (2 * h + v) * LANES
                        kv_ref[:, col:col + LANES] = val.astype(BF16)
            return run
        return [norm_rope(acc, i * IN_PIECE) for i, acc in enumerate(accs)]

    att_k = (chunk(W_AK, ATT_KV_WIDTH), k_norm_rope)

    def vt_matmul():
        return lax.dot_general(wvt_ref[...], xn, _NT, preferred_element_type=F32) + bvt_ref[...]

    def store_vt(vt, r0):
        def run():
            if r0 < M_WIDTH:
                vtm_ref[r0:r0 + VT_TN, :] = vt[r0:r0 + VT_TN, :].astype(BF16)
            else:
                vta_ref[r0 - M_WIDTH:r0 - M_WIDTH + VT_TN, :] = vt[r0:r0 + VT_TN, :].astype(BF16)
        return run

    def gate_matmul():
        return jnp.dot(xn, wif_ref[...], preferred_element_type=F32) + bif_ref[...]

    def vt_gate_epilogues(results):
        vt, gc = results
        gr = vt[VT_ROWS:VT_ROWS + N_IFG, :]
        return [store_vt(vt, r0) for r0 in range(0, VT_ROWS, VT_TN)] + [lambda: gate_epilogue(gc, gr)]

    def gate_epilogue(gc, gr):
        ls_c = _log_sigmoid(gc)
        ls_r = _log_sigmoid(gr)
        r = lax.broadcasted_iota(jnp.int32, (M_CHUNK, M_CHUNK), 0)
        c = lax.broadcasted_iota(jnp.int32, (M_CHUNK, M_CHUNK), 1)
        tril = jnp.where(c <= r, 1.0, 0.0).astype(BF16)
        triu = jnp.where(c >= r, 1.0, 0.0).astype(BF16)
        lane = lax.broadcasted_iota(jnp.int32, (M_CHUNK, LANES), 1)
        fwd_lanes = lane < 2 * M_HEADS
        grow_ref[ROW_LI_F:ROW_LI_F + M_HEADS, :] = gr[0:M_HEADS, :]
        grow_ref[ROW_LI_B:ROW_LI_B + M_HEADS, :] = gr[2 * M_HEADS:3 * M_HEADS, :]
        for ci in range(tm // M_CHUNK):
            rows = slice(ci * M_CHUNK, (ci + 1) * M_CHUNK)
            lsc = ls_c[rows, :]
            b_f = _split_dot_left(tril, lsc)
            b_b = _split_dot_left(triu, lsc)
            cc = gc[rows, :] - pltpu.roll(jnp.where(fwd_lanes, b_f, b_b), LANES - M_HEADS, 1)
            for ti, term in enumerate(_bf16_terms(cc)):
                ccol_ref[ti, rows, :] = term
            grow_ref[ROW_B_F:ROW_B_F + M_HEADS, rows] = _split_dot_right(ls_r[M_HEADS:2 * M_HEADS, rows], triu)
            grow_ref[ROW_B_B:ROW_B_B + M_HEADS, rows] = _split_dot_right(ls_r[3 * M_HEADS:4 * M_HEADS, rows], tril)

    light = plain
    heavy = [([vt_matmul, gate_matmul], vt_gate_epilogues),
             att_q[0], conv[0], att_q[1], conv[1], att_k, conv[2], conv[3]]
    stages.append(heavy[0])
    for i, stage in enumerate(heavy[1:]):
        stages.extend([stage, light[i]])
    stages.extend(light[len(heavy) - 1:])

    n_matmuls = sum(len(matmuls) for matmuls, _ in stages)
    queue = []
    emitted = 0
    for matmuls, make_epilogues in stages:
        results = []
        for matmul in matmuls:
            results.append(matmul())
            emitted += 1
            take = max(IN_EPILOGUES_PER_MATMUL, -(-len(queue) // max(n_matmuls - emitted, 1))) if queue else 0
            for run in queue[:take]:
                run()
            queue = queue[take:]
        queue.extend(make_epilogues(results))
    for run in queue:
        run()


def _in_proj(x, lw, rope_tab, seq):
    t = x.shape[0]
    tps = seq // IN_TM
    halo_blocks = IN_TM // IN_HALO
    const = lambda i: (0, 0)
    tile = lambda i: (i, 0)
    cols = lambda i: (0, i)
    resident = dict(pipeline_mode=pl.Buffered(1))
    return pl.pallas_call(
        functools.partial(_in_proj_kernel, tiles_per_seq=tps),
        grid=(t // IN_TM,),
        in_specs=[
            pl.BlockSpec((IN_TM, D_MODEL), tile),
            pl.BlockSpec((IN_HALO, D_MODEL), lambda i: (jnp.maximum(i * halo_blocks - 1, 0), 0)),
            pl.BlockSpec((IN_HALO, D_MODEL), lambda i: (jnp.minimum((i + 1) * halo_blocks, t // IN_HALO - 1), 0)),
            pl.BlockSpec((1, D_MODEL), const),
            pl.BlockSpec((D_MODEL, W_WIDTH), const, **resident),
            pl.BlockSpec((1, W_WIDTH), const),
            pl.BlockSpec((VT_ROWS + N_IFG, D_MODEL), const, **resident),
            pl.BlockSpec((VT_ROWS + N_IFG, 1), const),
            pl.BlockSpec((D_MODEL, LANES), const),
            pl.BlockSpec((1, LANES), const),
            pl.BlockSpec((CONV_K, 2 * M_WIDTH), const),
            pl.BlockSpec((3, IN_TM, LANES), lambda i: (0, i % tps, 0)),
            pl.BlockSpec((1, LANES), const),
            pl.BlockSpec((1, LANES), const),
        ],
        out_specs=[
            pl.BlockSpec((IN_TM, P_WIDTH), tile),
            pl.BlockSpec((IN_TM // ATT_BLOCK, ATT_PAIRS, ATT_BLOCK, LANES), lambda i: (i, 0, 0, 0)),
            pl.BlockSpec((IN_TM, ATT_KVAR_WIDTH), tile),
            pl.BlockSpec((ATT_KV_WIDTH, IN_TM), cols),
            pl.BlockSpec((M_WIDTH, IN_TM), cols),
            pl.BlockSpec((IN_TM, M_WIDTH), tile),
            pl.BlockSpec((M_WIDTH, IN_TM), cols),
            pl.BlockSpec((N_IFG, IN_TM), cols),
            pl.BlockSpec((3, IN_TM, LANES), lambda i: (0, i, 0)),
        ],
        out_shape=[
            jax.ShapeDtypeStruct((t, P_WIDTH), PROJ_DTYPE),
            jax.ShapeDtypeStruct((t // ATT_BLOCK, ATT_PAIRS, ATT_BLOCK, LANES), BF16),
            jax.ShapeDtypeStruct((t, ATT_KVAR_WIDTH), BF16),
            jax.ShapeDtypeStruct((ATT_KV_WIDTH, t), BF16),
            jax.ShapeDtypeStruct((M_WIDTH, t), BF16),
            jax.ShapeDtypeStruct((t, M_WIDTH), BF16),
            jax.ShapeDtypeStruct((M_WIDTH, t), BF16),
            jax.ShapeDtypeStruct((N_IFG, t), F32),
            jax.ShapeDtypeStruct((3, t, LANES), BF16),
        ],
        scratch_shapes=[pltpu.VMEM((IN_TM + 2 * IN_HALO, D_MODEL), BF16)],
        compiler_params=pltpu.CompilerParams(
            dimension_semantics=("arbitrary",), vmem_limit_bytes=VMEM_LIMIT_BYTES),
        name="in_proj",
    )(x, x, x, lw["norm_g"], lw["w_main"], lw["b_main"], lw["w_vt"], lw["b_vt"], lw["w_if"],
      lw["b_if"], lw["conv_w"], rope_tab, lw["gq"], lw["gk"])


def _attention_kernel(q4_ref, z_ref, kp_ref, kc_ref, kn_ref, vp_ref, vc_ref, vn_ref, sink_ref, out_ref,
                      ks, vts, out_t, *, n_blocks):
    ti = pl.program_id(1)
    bq = ATT_BLOCK

    r0 = 0
    for k_ref, v_ref in ((kp_ref, vp_ref), (kc_ref, vc_ref), (kn_ref, vn_ref)):
        nr = k_ref.shape[0]
        ks[r0:r0 + nr, :] = k_ref[...]
        vts[:, r0:r0 + nr] = v_ref[...]
        r0 += nr

    def group_heads(h):
        return (ATT_GROUP * h, ATT_GROUP * h + 2, ATT_GROUP * h + 1, ATT_GROUP * h + 3)

    sink_rows = [jnp.concatenate([jnp.broadcast_to(sink_ref[:, c:c + 1] * LOG2E, (1, bq)) for c in group_heads(h)],
                                 axis=1) for h in range(ATT_KV_HEADS)]

    kk = lax.broadcasted_iota(jnp.int32, (bq, bq), 0)
    qq = lax.broadcasted_iota(jnp.int32, (bq, bq), 1)
    prev_in_band = kk >= qq
    next_in_band = kk <= qq

    JG = ATT_SUB_GROUP

    def sub_blocks(jg, carry):
        js = [jg * JG + u for u in range(JG)]
        row0s = [pl.multiple_of(j * bq, bq) for j in js]
        biases = []
        for j in js:
            blk = ti * ATT_NSB + j
            bias_prev = jnp.where(prev_in_band, jnp.where(blk > 0, 0.0, NEG), NEG).astype(F32)
            bias_next = jnp.where(next_in_band, jnp.where(blk < n_blocks - 1, 0.0, NEG), NEG).astype(F32)
            biases.append((jnp.concatenate([bias_prev] * ATT_GROUP, axis=1),
                           jnp.concatenate([bias_next] * ATT_GROUP, axis=1)))
        q_slabs = [q4_ref[j] for j in js]
        pairs = [(u, h) for u in range(JG) for h in range(ATT_KV_HEADS)]
        q_pairs = [q_slabs[u][2 * h:2 * h + 2].reshape(2 * bq, LANES) for u, h in pairs]
        ss = [jnp.concatenate(
            [lax.dot_general(ks[pl.ds(row0s[u], 3 * bq), (2 * h + v) * LANES:(2 * h + v + 1) * LANES], q_pair, _NT,
                             preferred_element_type=F32)
             for v in range(2)], axis=1) for (u, h), q_pair in zip(pairs, q_pairs)]
        ss = [jnp.concatenate([s[:bq] + biases[u][0], s[bq:2 * bq], s[2 * bq:] + biases[u][1]], axis=0)
              for (u, h), s in zip(pairs, ss)]
        ms = [jnp.maximum(jnp.max(s, axis=0, keepdims=True), sink_rows[h]) for (u, h), s in zip(pairs, ss)]
        ps = [jnp.exp2(s - m) for s, m in zip(ss, ms)]
        denoms = [jnp.sum(p, axis=0, keepdims=True) + jnp.exp2(sink_rows[h] - m)
                  for (u, h), p, m in zip(pairs, ps, ms)]
        os_ = [jnp.dot(vts[h * ATT_HEAD_DIM:(h + 1) * ATT_HEAD_DIM, pl.ds(row0s[u], 3 * bq)], p.astype(BF16),
                       preferred_element_type=F32) * (1.0 / denom)
               for (u, h), p, denom in zip(pairs, ps, denoms)]
        for (u, h), o in zip(pairs, os_):
            for i, c in enumerate(group_heads(h)):
                out_t[u, c * ATT_HEAD_DIM:(c + 1) * ATT_HEAD_DIM, :] = o[:, i * bq:(i + 1) * bq]
        for u in range(JG):
            att = out_t[u].T
            z = z_ref[pl.ds(row0s[u], bq), :].astype(F32)
            out_ref[pl.ds(row0s[u], bq), :] = (att * _silu(z)).astype(out_ref.dtype)
        return carry

    lax.fori_loop(0, ATT_NSB // JG, sub_blocks, 0)


def _attention(proj, q4, kv, vta, sink, batch, seq):
    t = proj.shape[0]
    nb = seq // ATT_BLOCK
    nt = seq // ATT_TQ
    bq = ATT_BLOCK

    prev_blk = lambda b, i: b * nb + jnp.maximum(i * ATT_NSB - 1, 0)
    next_blk = lambda b, i: b * nb + jnp.minimum((i + 1) * ATT_NSB, nb - 1)
    return pl.pallas_call(
        functools.partial(_attention_kernel, n_blocks=nb),
        grid=(batch, nt),
        in_specs=[
            pl.BlockSpec((ATT_NSB, ATT_PAIRS, bq, LANES), lambda b, i: (b * nt + i, 0, 0, 0)),
            pl.BlockSpec((ATT_TQ, ATT_WIDTH), lambda b, i: (b * nt + i, P_AZ // ATT_WIDTH)),
            pl.BlockSpec((bq, ATT_KVAR_WIDTH), lambda b, i: (prev_blk(b, i), 0)),
            pl.BlockSpec((ATT_TQ, ATT_KVAR_WIDTH), lambda b, i: (b * nt + i, 0)),
            pl.BlockSpec((bq, ATT_KVAR_WIDTH), lambda b, i: (next_blk(b, i), 0)),
            pl.BlockSpec((ATT_KV_WIDTH, bq), lambda b, i: (0, prev_blk(b, i))),
            pl.BlockSpec((ATT_KV_WIDTH, ATT_TQ), lambda b, i: (0, b * nt + i)),
            pl.BlockSpec((ATT_KV_WIDTH, bq), lambda b, i: (0, next_blk(b, i))),
            pl.BlockSpec((1, ATT_HEADS), lambda b, i: (0, 0)),
        ],
        out_specs=pl.BlockSpec((ATT_TQ, ATT_WIDTH), lambda b, i: (b * nt + i, 0)),
        out_shape=jax.ShapeDtypeStruct((t, ATT_WIDTH), BF16),
        scratch_shapes=[
            pltpu.VMEM((ATT_TQ + 2 * bq, ATT_KVAR_WIDTH), BF16),
            pltpu.VMEM((ATT_KV_WIDTH, ATT_TQ + 2 * bq), BF16),
            pltpu.VMEM((ATT_SUB_GROUP, ATT_WIDTH, bq), F32),
        ],
        compiler_params=pltpu.CompilerParams(
            dimension_semantics=("arbitrary", "arbitrary"), vmem_limit_bytes=VMEM_LIMIT_BYTES),
        name="attention",
    )(q4, proj, kv, kv, kv, vta, vta, vta, sink)


def _mlstm_kernel(qct_ref, kc_ref, vt_ref, o_ref, z_ref, grow_ref, ccol_ref, ng_ref,
                  out_ref, cct_s, nf_s, nb_s, mf_s, mb_s, *, seq):
    hd = pl.program_id(1)
    L = M_CHUNK
    dv = M_HEAD_DIM
    nc = seq // L

    head_row = lax.broadcasted_iota(jnp.int32, (M_HEADS, L), 0) == hd

    def gate_row(base, t0):
        rows = grow_ref[base:base + M_HEADS, pl.ds(t0, L)]
        return jnp.sum(jnp.where(head_row, rows, 0.0), axis=0, keepdims=True)

    U = MLSTM_SCAN_GROUP
    dirs = ((ROW_LI_F, ROW_B_F, L - 1, nf_s, mf_s, 0), (ROW_LI_B, ROW_B_B, 0, nb_s, mb_s, dv))

    def scan_body(gj, carry):
        cis = [[gj * U + u for u in range(U)], [nc - 1 - (gj * U + u) for u in range(U)]]
        t0s = [[pl.multiple_of(ci * L, L) for ci in row] for row in cis]
        lis = [[gate_row(dirs[d][0], t0) for t0 in t0s[d]] for d in range(2)]
        bs = [[gate_row(dirs[d][1], t0) for t0 in t0s[d]] for d in range(2)]
        b_lasts = [[b[:, dirs[d][2]:dirs[d][2] + 1] for b in bs[d]] for d in range(2)]
        gs = [[b_last - b + li for b_last, b, li in zip(b_lasts[d], bs[d], lis[d])] for d in range(2)]
        g_maxs = [[jnp.max(g, axis=-1, keepdims=True) for g in gs[d]] for d in range(2)]
        ks = [[kc_ref[pl.ds(t0, L), :] for t0 in t0s[d]] for d in range(2)]
        vts = [[vt_ref[:, pl.ds(t0, L)].astype(F32) for t0 in t0s[d]] for d in range(2)]

        ms = [[carry[d][2]] for d in range(2)]
        for d in range(2):
            for u in range(U):
                ms[d].append(jnp.maximum(b_lasts[d][u] + ms[d][u], g_maxs[d][u]))
        w_cs = [[jnp.exp(b_lasts[d][u] + ms[d][u] - ms[d][u + 1]) for u in range(U)] for d in range(2)]
        w_ks = [[jnp.exp(gs[d][u] - ms[d][u + 1]) for u in range(U)] for d in range(2)]
        c_upds = [[jnp.dot((vts[d][u] * w_ks[d][u]).astype(BF16), ks[d][u], preferred_element_type=F32)
                   for u in range(U)] for d in range(2)]
        n_upds = [[jnp.dot(jnp.broadcast_to(w_ks[d][u], (SUBLANES, L)).astype(BF16), ks[d][u],
                           preferred_element_type=F32)[0:1, :] for u in range(U)] for d in range(2)]

        out = []
        for d in range(2):
            _, _, _, n_s, m_s, row0 = dirs[d]
            ct_st, n_st, _ = carry[d]
            for u in range(U):
                ci = cis[d][u]
                cct_s[ci, row0:row0 + dv, :] = ct_st.astype(BF16)
                n_s[ci] = jnp.broadcast_to(n_st, (SUBLANES, LANES))
                m_s[ci] = jnp.broadcast_to(ms[d][u], (SUBLANES, LANES))
                ct_st = w_cs[d][u] * ct_st + c_upds[d][u]
                n_st = w_cs[d][u] * n_st + n_upds[d][u]
            out.append((ct_st, n_st, ms[d][U]))
        return tuple(out)

    init = (jnp.zeros((dv, M_HEAD_DIM), F32), jnp.zeros((1, M_HEAD_DIM), F32),
            jnp.full((1, 1), NEG, F32))
    lax.fori_loop(0, nc // U, scan_body, (init, init))

    rr = lax.broadcasted_iota(jnp.int32, (L, L), 0)
    cc = lax.broadcasted_iota(jnp.int32, (L, L), 1)
    row8 = lax.broadcasted_iota(jnp.int32, (SUBLANES, LANES), 0)
    sel_k = lax.broadcasted_iota(jnp.int32, (LANES, 2 * L), 0)
    sel_n = lax.broadcasted_iota(jnp.int32, (LANES, 2 * L), 1)
    sel_lane = jnp.where(sel_n < L, LANE_C_F + hd, LANE_C_B + hd)
    sel = jnp.where(sel_k == sel_lane, 1.0, 0.0).astype(BF16)
    keep_f = rr <= cc
    keep_b = rr >= cc
    G = MLSTM_OUT_GROUP

    def out_body(gi, carry):
        cis = [gi * G + g for g in range(G)]
        t0s = [pl.multiple_of(ci * L, L) for ci in cis]
        qts = [qct_ref[:, pl.ds(t0, L)] for t0 in t0s]
        qk_ts = [jnp.dot(kc_ref[pl.ds(t0, L), :], qt, preferred_element_type=F32)
                 for t0, qt in zip(t0s, qts)]
        cqs = [jnp.dot(cct_s[ci], qt, preferred_element_type=F32) for ci, qt in zip(cis, qts)]
        n8s = [jnp.where(row8 == 0, nf_s[ci], jnp.where(row8 == 1, nb_s[ci], 0.0)).astype(BF16) for ci in cis]
        qns = [jnp.dot(n8, qt, preferred_element_type=F32) for n8, qt in zip(n8s, qts)]
        c_bcs = [sum(jnp.dot(ccol_ref[ti, pl.ds(t0, L), :], sel, preferred_element_type=F32) for ti in range(3))
                 for t0 in t0s]

        items = []
        for g in range(G):
            items.append((g, c_bcs[g][:, :L], qns[g][0:1, :], ROW_B_F, keep_f, mf_s))
            items.append((g, c_bcs[g][:, L:], qns[g][1:2, :], ROW_B_B, keep_b, mb_s))
        b_rs = [gate_row(b_base, t0s[g]) for g, _, _, b_base, _, _ in items]
        m_prevs = [m_s[cis[g]][0:1, 0:1] for g, _, _, _, _, m_s in items]
        ds = [jnp.where(keep, b_r + c_s, NEG)
              for (_, c_s, _, _, keep, _), b_r in zip(items, b_rs)]
        m_inters = [b_r + m_prev for b_r, m_prev in zip(b_rs, m_prevs)]
        m_ts = [jnp.maximum(m_inter, jnp.max(d, axis=0, keepdims=True)) for m_inter, d in zip(m_inters, ds)]
        w_inters = [jnp.exp(m_inter - m_t) for m_inter, m_t in zip(m_inters, m_ts)]
        a_s = [jnp.exp(d - m_t) * qk_ts[it[0]] for d, m_t, it in zip(ds, m_ts, items)]
        dens = [w_inter * it[2] + jnp.sum(a, axis=0, keepdims=True) for w_inter, it, a in zip(w_inters, items, a_s)]
        invs = [1.0 / jnp.maximum(jnp.abs(den), jnp.exp(-m_t)) for den, m_t in zip(dens, m_ts)]
        ps = [a * inv for a, inv in zip(a_s, invs)]
        ws = [w_inter * inv for w_inter, inv in zip(w_inters, invs)]

        h_ts = [jnp.dot(vt_ref[:, pl.ds(t0s[g], L)], (ps[2 * g] + ps[2 * g + 1]).astype(BF16),
                        preferred_element_type=F32)
                + cqs[g][:dv, :] * ws[2 * g] + cqs[g][dv:, :] * ws[2 * g + 1] for g in range(G)]
        hs = [_sigmoid(o_ref[pl.ds(t0, L), :].astype(F32)) * h_t.T for t0, h_t in zip(t0s, h_ts)]
        hs = [h * lax.rsqrt(jnp.mean(h * h, axis=-1, keepdims=True) + NORM_EPS) * ng_ref[...] for h in hs]
        for t0, h in zip(t0s, hs):
            out_ref[pl.ds(t0, L), :] = (h * _silu(z_ref[pl.ds(t0, L), :].astype(F32))).astype(out_ref.dtype)
        return carry

    lax.fori_loop(0, nc // G, out_body, 0)


def _mlstm(proj, qct, kc, vtm, grow, ccol, ng, batch, seq):
    t = proj.shape[0]
    nc = seq // M_CHUNK
    head_rows = lambda b, h: (h, b)
    head_cols = lambda b, h: (b, h)

    def proj_head(col0):
        return pl.BlockSpec((seq, M_HEAD_DIM), lambda b, h: (b, col0 // M_HEAD_DIM + h))

    return pl.pallas_call(
        functools.partial(_mlstm_kernel, seq=seq),
        grid=(batch, M_HEADS),
        in_specs=[
            pl.BlockSpec((M_HEAD_DIM, seq), head_rows),
            pl.BlockSpec((seq, M_HEAD_DIM), head_cols),
            pl.BlockSpec((M_HEAD_DIM, seq), head_rows),
            proj_head(P_MO), proj_head(P_MZ),
            pl.BlockSpec((N_IFG, seq), lambda b, h: (0, b)),
            pl.BlockSpec((3, seq, LANES), lambda b, h: (0, b, 0)),
            pl.BlockSpec((1, M_HEAD_DIM), lambda b, h: (0, h)),
        ],
        out_specs=pl.BlockSpec((seq, M_HEAD_DIM), head_cols),
        out_shape=jax.ShapeDtypeStruct((t, M_WIDTH), BF16),
        scratch_shapes=[
            pltpu.VMEM((nc, 2 * M_HEAD_DIM, M_HEAD_DIM), BF16),
            pltpu.VMEM((nc, SUBLANES, M_HEAD_DIM), F32),
            pltpu.VMEM((nc, SUBLANES, M_HEAD_DIM), F32),
            pltpu.VMEM((nc, SUBLANES, LANES), F32),
            pltpu.VMEM((nc, SUBLANES, LANES), F32),
        ],
        compiler_params=pltpu.CompilerParams(
            dimension_semantics=("arbitrary", "arbitrary"), vmem_limit_bytes=VMEM_LIMIT_BYTES),
        name="mlstm",
    )(qct, kc, vtm, proj, proj, grow, ccol, ng)


def _out_proj_kernel(x_ref, a_ref, m_ref, gates_ref, wa_ref, wm_ref, wo_ref, out_ref):
    branch_a = jnp.dot(a_ref[...], wa_ref[...], preferred_element_type=F32)
    branch_m = jnp.dot(m_ref[...], wm_ref[...], preferred_element_type=F32)
    gates = _sigmoid(gates_ref[...].astype(F32))
    merged = gates[:, :D_MODEL] * branch_a + gates[:, D_MODEL:] * branch_m
    out_ref[...] = x_ref[...] + jnp.dot(merged.astype(BF16), wo_ref[...], preferred_element_type=F32)


def _out_proj(x, a, m, proj, wa, wm, wo):
    t = x.shape[0]
    tile = lambda i: (i, 0)
    const = lambda i: (0, 0)
    return pl.pallas_call(
        _out_proj_kernel,
        grid=(t // OUT_TM,),
        in_specs=[
            pl.BlockSpec((OUT_TM, D_MODEL), tile),
            pl.BlockSpec((OUT_TM, ATT_WIDTH), tile),
            pl.BlockSpec((OUT_TM, M_WIDTH), tile),
            pl.BlockSpec((OUT_TM, 2 * D_MODEL), lambda i: (i, P_GATES // (2 * D_MODEL))),
            pl.BlockSpec((ATT_WIDTH, D_MODEL), const),
            pl.BlockSpec((M_WIDTH, D_MODEL), const),
            pl.BlockSpec((D_MODEL, D_MODEL), const),
        ],
        out_specs=pl.BlockSpec((OUT_TM, D_MODEL), tile),
        out_shape=jax.ShapeDtypeStruct((t, D_MODEL), F32),
        compiler_params=pltpu.CompilerParams(
            dimension_semantics=("arbitrary",), vmem_limit_bytes=VMEM_LIMIT_BYTES),
        name="out_proj",
    )(x, a, m, proj, wa, wm, wo)


def _rope_tables(seq):
    inv = jnp.power(jnp.float32(ROPE_THETA), -jnp.arange(ROPE_HALF, dtype=F32) * 2.0 / ROPE_DIM)
    ang = jnp.arange(seq, dtype=F32)[:, None] * inv[None, :]
    cos, sin = jnp.cos(ang), jnp.sin(ang)
    pad = ATT_HEAD_DIM - ROPE_DIM
    ones = jnp.ones((seq, pad), F32)
    zeros = jnp.zeros((seq, pad), F32)
    zh = jnp.zeros((seq, ROPE_HALF), F32)
    c = jnp.concatenate([cos, cos, ones], axis=-1)
    s1 = jnp.concatenate([-sin, zh, zeros], axis=-1)
    s2 = jnp.concatenate([zh, sin, zeros], axis=-1)
    tab = jnp.stack([c, s1, s2])
    return jnp.concatenate([tab, tab], axis=-1)


def _prep_weights(w_in, b_in):
    order = [(_R_GATES, 2 * D_MODEL), (_R_AZ, ATT_WIDTH), (_R_MO, M_WIDTH), (_R_MZ, M_WIDTH),
             (_R_MQ, M_WIDTH), (_R_MK, M_WIDTH), (_R_AQ, ATT_WIDTH), (_R_AK, ATT_KV_WIDTH)]
    w_main = jnp.concatenate([w_in[:, :, o:o + n] for o, n in order], axis=-1).astype(BF16)
    b_main = jnp.concatenate([b_in[:, o:o + n] for o, n in order], axis=-1)[:, None, :]
    vt_order = [(_R_MV, M_WIDTH), (_R_AV, ATT_KV_WIDTH), (_R_IF, N_IFG)]
    w_vt = jnp.swapaxes(jnp.concatenate([w_in[:, :, o:o + n] for o, n in vt_order], axis=-1), 1, 2).astype(BF16)
    b_vt = jnp.concatenate([b_in[:, o:o + n] for o, n in vt_order], axis=-1)[:, :, None]
    w_if = w_in[:, :, _R_IF:_R_IF + N_IFG]
    b_if = b_in[:, _R_IF:_R_IF + N_IFG]
    w_if_pad = jnp.pad(w_if, ((0, 0), (0, 0), (0, LANES - N_IFG))).astype(BF16)
    b_if_pad = jnp.pad(b_if, ((0, 0), (0, LANES - N_IFG)))[:, None, :]
    return w_main, b_main, w_vt, b_vt, w_if_pad, b_if_pad


def _trunk(x, rope_tab, layers):
    batch, seq, _ = x.shape
    xf = x.reshape(batch * seq, D_MODEL)
    for lw in layers:
        proj, q4, kv, vta, qct, kc, vtm, grow, ccol = _in_proj(xf, lw, rope_tab, seq)
        a = _attention(proj, q4, kv, vta, lw["sink"], batch, seq)
        m = _mlstm(proj, qct, kc, vtm, grow, ccol, lw["m_norm_g"], batch, seq)
        xf = _out_proj(xf, a, m, proj, lw["w_att_out"], lw["w_m_out"], lw["w_out"])
    return xf.reshape(batch, seq, D_MODEL)


def kernel(x_prompt, x_sample, norm_g, w_in, b_in, q_norm_g, k_norm_g, sink, conv_w, m_norm_g,
           w_att_out, w_m_out, w_out):
    w_main, b_main, w_vt, b_vt, w_if, b_if = _prep_weights(w_in, b_in)
    wa, wm, wo = w_att_out.astype(BF16), w_m_out.astype(BF16), w_out.astype(BF16)
    layers = []
    for l in range(DEPTH):
        layers.append(dict(
            norm_g=norm_g[l][None, :], w_main=w_main[l], b_main=b_main[l], w_vt=w_vt[l], b_vt=b_vt[l],
            w_if=w_if[l], b_if=b_if[l],
            gq=jnp.tile(q_norm_g[l] * (ATT_HEAD_DIM ** -0.5 * LOG2E), 2)[None, :],
            gk=jnp.tile(k_norm_g[l], 2)[None, :],
            sink=sink[l][None, :], conv_w=conv_w[l], m_norm_g=m_norm_g[l][None, :],
            w_att_out=wa[l], w_m_out=wm[l], w_out=wo[l]))
    outs = []
    for x in (x_prompt, x_sample):
        outs.append(_trunk(x, _rope_tables(x.shape[1]), layers))
    return tuple(outs)
```

```python
import functools
import math

import jax
import jax.numpy as jnp
from jax import lax
from jax.experimental import pallas as pl
from jax.experimental.pallas import tpu as pltpu

F32 = jnp.float32
BF16 = jnp.bfloat16

D_MODEL = 1024
DEPTH = 4
ATT_HEADS = 16
ATT_KV_HEADS = 4
ATT_GROUP = ATT_HEADS // ATT_KV_HEADS
ATT_HEAD_DIM = 64
ATT_WIDTH = ATT_HEADS * ATT_HEAD_DIM
ATT_KV_WIDTH = ATT_KV_HEADS * ATT_HEAD_DIM
WINDOW = 128
ATT_BLOCK = 128
ROPE_THETA = 500000.0
ROPE_DIM = ATT_HEAD_DIM // 4
ROPE_HALF = ROPE_DIM // 2
M_HEADS = 8
M_HEAD_DIM = 128
M_WIDTH = M_HEADS * M_HEAD_DIM
M_CHUNK = 128
CONV_K = 3
NORM_EPS = 1e-6
NEG = -1e30
LOG2E = math.log2(math.e)

LANES = 128
SUBLANES = 8
VMEM_LIMIT_BYTES = 56 * 1024 * 1024
_REF_SPLITS = (ATT_WIDTH, ATT_KV_WIDTH, ATT_KV_WIDTH, ATT_WIDTH,
               M_WIDTH, M_WIDTH, M_WIDTH, M_WIDTH, M_WIDTH,
               M_HEADS, M_HEADS, M_HEADS, M_HEADS, 2 * D_MODEL)
_REF_OFF = [0]
for _w in _REF_SPLITS:
    _REF_OFF.append(_REF_OFF[-1] + _w)
(_R_AQ, _R_AK, _R_AV, _R_AZ, _R_MQ, _R_MK, _R_MV, _R_MO, _R_MZ,
 _R_IF, _R_FF, _R_IB, _R_FB, _R_GATES) = _REF_OFF[:-1]

P_GATES = 0
P_AZ = P_GATES + 2 * D_MODEL
P_MO = P_AZ + ATT_WIDTH
P_MZ = P_MO + M_WIDTH
P_WIDTH = P_MZ + M_WIDTH
W_MQ = P_WIDTH
W_MK = W_MQ + M_WIDTH
W_AQ = W_MK + M_WIDTH
W_AK = W_AQ + ATT_WIDTH
W_WIDTH = W_AK + ATT_KV_WIDTH
VT_ROWS = M_WIDTH + ATT_KV_WIDTH
N_IFG = 4 * M_HEADS

ROW_LI_F, ROW_B_F, ROW_LI_B, ROW_B_B = 0, M_HEADS, 2 * M_HEADS, 3 * M_HEADS
LANE_C_F, LANE_C_B = 0, 2 * M_HEADS

PROJ_DTYPE = BF16
IN_TM = 512
IN_TN = 512
IN_EPILOGUES_PER_MATMUL = 2
IN_HALO = 16
IN_PIECE = 256
VT_TN = 256
OUT_TM = 1024
ATT_TQ = 512
ATT_NSB = ATT_TQ // ATT_BLOCK
ATT_SUB_GROUP = 4
ATT_PAIRS = ATT_WIDTH // LANES
ATT_KVAR_WIDTH = ATT_KV_HEADS * 2 * LANES
MLSTM_SCAN_GROUP = 16
MLSTM_OUT_GROUP = 16

_NT = (((1,), (1,)), ((), ()))


def _sigmoid(x):
    return 1.0 / (1.0 + jnp.exp(-x))


def _silu(x):
    return x * _sigmoid(x)


def _log_sigmoid(x):
    return jnp.minimum(x, 0.0) - jnp.log1p(jnp.exp(-jnp.abs(x)))


def _bf16_terms(x):
    hi = x.astype(BF16)
    r1 = x - hi.astype(F32)
    mid = r1.astype(BF16)
    lo = (r1 - mid.astype(F32)).astype(BF16)
    return hi, mid, lo


def _split_dot_left(sel, x):
    return sum(jnp.dot(sel, t, preferred_element_type=F32) for t in _bf16_terms(x))


def _split_dot_right(x, sel):
    return sum(jnp.dot(t, sel, preferred_element_type=F32) for t in _bf16_terms(x))


def _rope(y, tab):
    return (y * tab[0]
            + pltpu.roll(y, LANES - ROPE_HALF, 1) * tab[1]
            + pltpu.roll(y, ROPE_HALF, 1) * tab[2])


def _norm_rope_tile(x, gain, tab):
    lane = lax.broadcasted_iota(jnp.int32, x.shape, 1)
    left = lane < ATT_HEAD_DIM
    sq = x * x
    ss_l = jnp.sum(jnp.where(left, sq, 0.0), axis=-1, keepdims=True)
    ss_r = jnp.sum(jnp.where(left, 0.0, sq), axis=-1, keepdims=True)
    inv = jnp.where(left, lax.rsqrt(ss_l / ATT_HEAD_DIM + NORM_EPS),
                    lax.rsqrt(ss_r / ATT_HEAD_DIM + NORM_EPS))
    return _rope(x * inv * gain, tab)


def _rmsnorm_bf16(x, g):
    return (x * lax.rsqrt(jnp.mean(x * x, axis=-1, keepdims=True) + NORM_EPS) * g).astype(BF16)


def _in_proj_kernel(x_ref, xprev_ref, xnext_ref, g_ref, w_ref, b_ref, wvt_ref, bvt_ref,
                    wif_ref, bif_ref, cw_ref, tab_ref, gq_ref, gk_ref,
                    proj_ref, q4_ref, kv_ref, vta_ref, qct_ref, kc_ref, vtm_ref, grow_ref, ccol_ref,
                    xn_s, conv_s, *, tiles_per_seq):
    tm = x_ref.shape[0]
    tile_in_seq = pl.program_id(0) % tiles_per_seq

    xn_s[0:IN_HALO, :] = _rmsnorm_bf16(xprev_ref[...], g_ref[...])
    xn_s[IN_HALO:IN_HALO + tm, :] = _rmsnorm_bf16(x_ref[...], g_ref[...])
    xn_s[IN_HALO + tm:, :] = _rmsnorm_bf16(xnext_ref[...], g_ref[...])
    xn = xn_s[IN_HALO:IN_HALO + tm, :]
    xn_halo = xn_s[...]

    def chunk(c0, width, lhs=None):
        lhs = xn if lhs is None else lhs

        def piece(p0):
            cols = slice(p0, p0 + IN_PIECE)
            return lambda: jnp.dot(lhs, w_ref[:, cols], preferred_element_type=F32) + b_ref[:, cols]
        return [piece(p0) for p0 in range(c0, c0 + width, IN_PIECE)]

    stages = []

    def store_plain(c0):
        def store(acc, p0):
            def run():
                proj_ref[:, p0:p0 + IN_PIECE] = acc.astype(proj_ref.dtype)
            return run
        return lambda accs: [store(acc, c0 + i * IN_PIECE) for i, acc in enumerate(accs)]

    plain = [(chunk(c0, IN_TN), store_plain(c0)) for c0 in range(0, P_WIDTH, IN_TN)]

    has_before = tile_in_seq > 0
    has_after = tile_in_seq < tiles_per_seq - 1
    row_id = lax.broadcasted_iota(jnp.int32, (tm, LANES), 0)
    kscale = M_HEAD_DIM ** -0.5
    lane_tiles = range(0, IN_PIECE, LANES)

    def conv_silu(c0):
        def conv(acc, p0, l0):
            cols = slice(p0 + l0, p0 + l0 + LANES)

            def run():
                buf = conv_s.at[((p0 + l0) // LANES) % 2]
                buf[...] = acc[:, l0:l0 + LANES]
                lo, hi = IN_HALO - 1, IN_HALO + tm
                buf[lo:lo + 1, :] = jnp.where(has_before, acc[lo:lo + 1, l0:l0 + LANES], 0.0)
                buf[hi:hi + 1, :] = jnp.where(has_after, acc[hi:hi + 1, l0:l0 + LANES], 0.0)
                a = acc[IN_HALO:IN_HALO + tm, l0:l0 + LANES]
                xm = buf[IN_HALO - 1:IN_HALO - 1 + tm, :]
                xp = buf[IN_HALO + 1:IN_HALO + 1 + tm, :]
                y = _silu(xm * cw_ref[0:1, cols] + a * cw_ref[1:2, cols] + xp * cw_ref[2:3, cols])
                if p0 < M_WIDTH:
                    qct_ref[cols, :] = y.T.astype(BF16)
                else:
                    kc_ref[:, p0 + l0 - M_WIDTH:p0 + l0 - M_WIDTH + LANES] = (y * kscale).astype(BF16)
            return run
        return lambda accs: [conv(acc, c0 + i * IN_PIECE, l0) for i, acc in enumerate(accs) for l0 in lane_tiles]

    conv = [(chunk(W_MQ + c0, IN_TN, lhs=xn_halo), conv_silu(c0)) for c0 in range(0, 2 * M_WIDTH, IN_TN)]

    tab = tab_ref[...]

    def q_rope(c0):
        def rope(acc, p0, l0):
            def run():
                y = _norm_rope_tile(acc[:, l0:l0 + LANES], gq_ref[...], tab).astype(BF16)
                for j in range(tm // ATT_BLOCK):
                    q4_ref[j, (p0 + l0) // LANES] = y[j * ATT_BLOCK:(j + 1) * ATT_BLOCK, :]
            return run
        return lambda accs: [rope(acc, c0 + i * IN_PIECE, l0) for i, acc in enumerate(accs) for l0 in lane_tiles]

    att_q = [(chunk(W_AQ + c0, IN_TN), q_rope(c0)) for c0 in range(0, ATT_WIDTH, IN_TN)]

    left_half = lax.broadcasted_iota(jnp.int32, (tm, LANES), 1) < ATT_HEAD_DIM

    def k_norm_rope(accs):
        def norm_rope(acc, p0):
            def run():
                for t in range(IN_PIECE // LANES):
                    kt = _norm_rope_tile(acc[:, t * LANES:(t + 1) * LANES], gk_ref[...], tab)
                    sw = pltpu.roll(kt, ATT_HEAD_DIM, 1)
                    head = (p0 // LANES + t) * 2
                    variants = ((head, 0, jnp.where(left_half, kt, 0.0)), (head, 1, jnp.where(left_half, 0.0, sw)),
                                (head + 1, 0, jnp.where(left_half, sw, 0.0)), (head + 1, 1, jnp.where(left_half, 0.0, kt)))
                    for h, v, val in variants:
                        col = (2 * h + v) * LANES
                        kv_ref[:, col:col + LANES] = val.astype(BF16)
            return run
        return [norm_rope(acc, i * IN_PIECE) for i, acc in enumerate(accs)]

    att_k = (chunk(W_AK, ATT_KV_WIDTH), k_norm_rope)

    def vt_matmul():
        return lax.dot_general(wvt_ref[...], xn, _NT, preferred_element_type=F32) + bvt_ref[...]

    def store_vt(vt, r0):
        def run():
            if r0 < M_WIDTH:
                vtm_ref[r0:r0 + VT_TN, :] = vt[r0:r0 + VT_TN, :].astype(BF16)
            else:
                vta_ref[r0 - M_WIDTH:r0 - M_WIDTH + VT_TN, :] = vt[r0:r0 + VT_TN, :].astype(BF16)
        return run

    def gate_matmul():
        return jnp.dot(xn, wif_ref[...], preferred_element_type=F32) + bif_ref[...]

    def vt_gate_epilogues(results):
        vt, gc = results
        gr = vt[VT_ROWS:VT_ROWS + N_IFG, :]
        return [store_vt(vt, r0) for r0 in range(0, VT_ROWS, VT_TN)] + [lambda: gate_epilogue(gc, gr)]

    def gate_epilogue(gc, gr):
        ls_c = _log_sigmoid(gc)
        ls_r = _log_sigmoid(gr)
        r = lax.broadcasted_iota(jnp.int32, (M_CHUNK, M_CHUNK), 0)
        c = lax.broadcasted_iota(jnp.int32, (M_CHUNK, M_CHUNK), 1)
        tril = jnp.where(c <= r, 1.0, 0.0).astype(BF16)
        triu = jnp.where(c >= r, 1.0, 0.0).astype(BF16)
        lane = lax.broadcasted_iota(jnp.int32, (M_CHUNK, LANES), 1)
        fwd_lanes = lane < 2 * M_HEADS
        grow_ref[ROW_LI_F:ROW_LI_F + M_HEADS, :] = gr[0:M_HEADS, :]
        grow_ref[ROW_LI_B:ROW_LI_B + M_HEADS, :] = gr[2 * M_HEADS:3 * M_HEADS, :]
        for ci in range(tm // M_CHUNK):
            rows = slice(ci * M_CHUNK, (ci + 1) * M_CHUNK)
            lsc = ls_c[rows, :]
            b_f = _split_dot_left(tril, lsc)
            b_b = _split_dot_left(triu, lsc)
            cc = gc[rows, :] - pltpu.roll(jnp.where(fwd_lanes, b_f, b_b), LANES - M_HEADS, 1)
            for ti, term in enumerate(_bf16_terms(cc)):
                ccol_ref[ti, rows, :] = term
            grow_ref[ROW_B_F:ROW_B_F + M_HEADS, rows] = _split_dot_right(ls_r[M_HEADS:2 * M_HEADS, rows], triu)
            grow_ref[ROW_B_B:ROW_B_B + M_HEADS, rows] = _split_dot_right(ls_r[3 * M_HEADS:4 * M_HEADS, rows], tril)

    light = plain
    heavy = [([vt_matmul, gate_matmul], vt_gate_epilogues),
             conv[0], att_q[0], conv[1], att_q[1], conv[2], att_k, conv[3]]
    stages.append(heavy[0])
    for i, stage in enumerate(heavy[1:]):
        stages.extend([stage, light[i]])
    stages.extend(light[len(heavy) - 1:])

    n_matmuls = sum(len(matmuls) for matmuls, _ in stages)
    queue = []
    emitted = 0
    for matmuls, make_epilogues in stages:
        results = []
        for matmul in matmuls:
            results.append(matmul())
            emitted += 1
            take = max(IN_EPILOGUES_PER_MATMUL, -(-len(queue) // max(n_matmuls - emitted, 1))) if queue else 0
            for run in queue[:take]:
                run()
            queue = queue[take:]
        queue.extend(make_epilogues(results))
    for run in queue:
        run()


def _in_proj(x, lw, rope_tab, seq):
    t = x.shape[0]
    tps = seq // IN_TM
    halo_blocks = IN_TM // IN_HALO
    const = lambda i: (0, 0)
    tile = lambda i: (i, 0)
    cols = lambda i: (0, i)
    resident = dict(pipeline_mode=pl.Buffered(1))
    return pl.pallas_call(
        functools.partial(_in_proj_kernel, tiles_per_seq=tps),
        grid=(t // IN_TM,),
        in_specs=[
            pl.BlockSpec((IN_TM, D_MODEL), tile),
            pl.BlockSpec((IN_HALO, D_MODEL), lambda i: (jnp.maximum(i * halo_blocks - 1, 0), 0)),
            pl.BlockSpec((IN_HALO, D_MODEL), lambda i: (jnp.minimum((i + 1) * halo_blocks, t // IN_HALO - 1), 0)),
            pl.BlockSpec((1, D_MODEL), const),
            pl.BlockSpec((D_MODEL, W_WIDTH), const, **resident),
            pl.BlockSpec((1, W_WIDTH), const),
            pl.BlockSpec((VT_ROWS + N_IFG, D_MODEL), const, **resident),
            pl.BlockSpec((VT_ROWS + N_IFG, 1), const),
            pl.BlockSpec((D_MODEL, LANES), const),
            pl.BlockSpec((1, LANES), const),
            pl.BlockSpec((CONV_K, 2 * M_WIDTH), const),
            pl.BlockSpec((3, IN_TM, LANES), lambda i: (0, i % tps, 0)),
            pl.BlockSpec((1, LANES), const),
            pl.BlockSpec((1, LANES), const),
        ],
        out_specs=[
            pl.BlockSpec((IN_TM, P_WIDTH), tile),
            pl.BlockSpec((IN_TM // ATT_BLOCK, ATT_PAIRS, ATT_BLOCK, LANES), lambda i: (i, 0, 0, 0)),
            pl.BlockSpec((IN_TM, ATT_KVAR_WIDTH), tile),
            pl.BlockSpec((ATT_KV_WIDTH, IN_TM), cols),
            pl.BlockSpec((M_WIDTH, IN_TM), cols),
            pl.BlockSpec((IN_TM, M_WIDTH), tile),
            pl.BlockSpec((M_WIDTH, IN_TM), cols),
            pl.BlockSpec((N_IFG, IN_TM), cols),
            pl.BlockSpec((3, IN_TM, LANES), lambda i: (0, i, 0)),
        ],
        out_shape=[
            jax.ShapeDtypeStruct((t, P_WIDTH), PROJ_DTYPE),
            jax.ShapeDtypeStruct((t // ATT_BLOCK, ATT_PAIRS, ATT_BLOCK, LANES), BF16),
            jax.ShapeDtypeStruct((t, ATT_KVAR_WIDTH), BF16),
            jax.ShapeDtypeStruct((ATT_KV_WIDTH, t), BF16),
            jax.ShapeDtypeStruct((M_WIDTH, t), BF16),
            jax.ShapeDtypeStruct((t, M_WIDTH), BF16),
            jax.ShapeDtypeStruct((M_WIDTH, t), BF16),
            jax.ShapeDtypeStruct((N_IFG, t), F32),
            jax.ShapeDtypeStruct((3, t, LANES), BF16),
        ],
        scratch_shapes=[pltpu.VMEM((IN_TM + 2 * IN_HALO, D_MODEL), BF16),
                        pltpu.VMEM((2, IN_TM + 2 * IN_HALO, LANES), F32)],
        compiler_params=pltpu.CompilerParams(
            dimension_semantics=("arbitrary",), vmem_limit_bytes=VMEM_LIMIT_BYTES),
        name="in_proj",
    )(x, x, x, lw["norm_g"], lw["w_main"], lw["b_main"], lw["w_vt"], lw["b_vt"], lw["w_if"],
      lw["b_if"], lw["conv_w"], rope_tab, lw["gq"], lw["gk"])


def _attention_kernel(q4_ref, z_ref, kp_ref, kc_ref, kn_ref, vp_ref, vc_ref, vn_ref, sink_ref, out_ref,
                      ks, vts, out_t, *, n_blocks):
    ti = pl.program_id(1)
    bq = ATT_BLOCK

    r0 = 0
    for k_ref, v_ref in ((kp_ref, vp_ref), (kc_ref, vc_ref), (kn_ref, vn_ref)):
        nr = k_ref.shape[0]
        ks[r0:r0 + nr, :] = k_ref[...]
        vts[:, r0:r0 + nr] = v_ref[...]
        r0 += nr

    def group_heads(h):
        return (ATT_GROUP * h, ATT_GROUP * h + 2, ATT_GROUP * h + 1, ATT_GROUP * h + 3)

    sink_rows = [jnp.concatenate([jnp.broadcast_to(sink_ref[:, c:c + 1] * LOG2E, (1, bq)) for c in group_heads(h)],
                                 axis=1) for h in range(ATT_KV_HEADS)]

    kk = lax.broadcasted_iota(jnp.int32, (bq, bq), 0)
    qq = lax.broadcasted_iota(jnp.int32, (bq, bq), 1)
    prev_in_band = kk >= qq
    next_in_band = kk <= qq

    JG = ATT_SUB_GROUP

    def sub_blocks(jg, carry):
        js = [jg * JG + u for u in range(JG)]
        row0s = [pl.multiple_of(j * bq, bq) for j in js]
        biases = []
        for j in js:
            blk = ti * ATT_NSB + j
            bias_prev = jnp.where(prev_in_band, jnp.where(blk > 0, 0.0, NEG), NEG).astype(F32)
            bias_next = jnp.where(next_in_band, jnp.where(blk < n_blocks - 1, 0.0, NEG), NEG).astype(F32)
            biases.append((jnp.concatenate([bias_prev] * ATT_GROUP, axis=1),
                           jnp.concatenate([bias_next] * ATT_GROUP, axis=1)))
        q_slabs = [q4_ref[j] for j in js]
        pairs = [(u, h) for u in range(JG) for h in range(ATT_KV_HEADS)]
        q_pairs = [q_slabs[u][2 * h:2 * h + 2].reshape(2 * bq, LANES) for u, h in pairs]
        ss = [jnp.concatenate(
            [lax.dot_general(ks[pl.ds(row0s[u], 3 * bq), (2 * h + v) * LANES:(2 * h + v + 1) * LANES], q_pair, _NT,
                             preferred_element_type=F32)
             for v in range(2)], axis=1) for (u, h), q_pair in zip(pairs, q_pairs)]
        ss = [jnp.concatenate([s[:bq] + biases[u][0], s[bq:2 * bq], s[2 * bq:] + biases[u][1]], axis=0)
              for (u, h), s in zip(pairs, ss)]
        ms = [jnp.maximum(jnp.max(s, axis=0, keepdims=True), sink_rows[h]) for (u, h), s in zip(pairs, ss)]
        ps = [jnp.exp2(s - m) for s, m in zip(ss, ms)]
        denoms = [jnp.sum(p, axis=0, keepdims=True) + jnp.exp2(sink_rows[h] - m)
                  for (u, h), p, m in zip(pairs, ps, ms)]
        os_ = [jnp.dot(vts[h * ATT_HEAD_DIM:(h + 1) * ATT_HEAD_DIM, pl.ds(row0s[u], 3 * bq)], p.astype(BF16),
                       preferred_element_type=F32) * (1.0 / denom)
               for (u, h), p, denom in zip(pairs, ps, denoms)]
        for (u, h), o in zip(pairs, os_):
            for i, c in enumerate(group_heads(h)):
                out_t[u, c * ATT_HEAD_DIM:(c + 1) * ATT_HEAD_DIM, :] = o[:, i * bq:(i + 1) * bq]
        for u in range(JG):
            att = out_t[u].T
            z = z_ref[pl.ds(row0s[u], bq), :].astype(F32)
            out_ref[pl.ds(row0s[u], bq), :] = (att * _silu(z)).astype(out_ref.dtype)
        return carry

    lax.fori_loop(0, ATT_NSB // JG, sub_blocks, 0)


def _attention(proj, q4, kv, vta, sink, batch, seq):
    t = proj.shape[0]
    nb = seq // ATT_BLOCK
    nt = seq // ATT_TQ
    bq = ATT_BLOCK

    prev_blk = lambda b, i: b * nb + jnp.maximum(i * ATT_NSB - 1, 0)
    next_blk = lambda b, i: b * nb + jnp.minimum((i + 1) * ATT_NSB, nb - 1)
    return pl.pallas_call(
        functools.partial(_attention_kernel, n_blocks=nb),
        grid=(batch, nt),
        in_specs=[
            pl.BlockSpec((ATT_NSB, ATT_PAIRS, bq, LANES), lambda b, i: (b * nt + i, 0, 0, 0)),
            pl.BlockSpec((ATT_TQ, ATT_WIDTH), lambda b, i: (b * nt + i, P_AZ // ATT_WIDTH)),
            pl.BlockSpec((bq, ATT_KVAR_WIDTH), lambda b, i: (prev_blk(b, i), 0)),
            pl.BlockSpec((ATT_TQ, ATT_KVAR_WIDTH), lambda b, i: (b * nt + i, 0)),
            pl.BlockSpec((bq, ATT_KVAR_WIDTH), lambda b, i: (next_blk(b, i), 0)),
            pl.BlockSpec((ATT_KV_WIDTH, bq), lambda b, i: (0, prev_blk(b, i))),
            pl.BlockSpec((ATT_KV_WIDTH, ATT_TQ), lambda b, i: (0, b * nt + i)),
            pl.BlockSpec((ATT_KV_WIDTH, bq), lambda b, i: (0, next_blk(b, i))),
            pl.BlockSpec((1, ATT_HEADS), lambda b, i: (0, 0)),
        ],
        out_specs=pl.BlockSpec((ATT_TQ, ATT_WIDTH), lambda b, i: (b * nt + i, 0)),
        out_shape=jax.ShapeDtypeStruct((t, ATT_WIDTH), BF16),
        scratch_shapes=[
            pltpu.VMEM((ATT_TQ + 2 * bq, ATT_KVAR_WIDTH), BF16),
            pltpu.VMEM((ATT_KV_WIDTH, ATT_TQ + 2 * bq), BF16),
            pltpu.VMEM((ATT_SUB_GROUP, ATT_WIDTH, bq), F32),
        ],
        compiler_params=pltpu.CompilerParams(
            dimension_semantics=("arbitrary", "arbitrary"), vmem_limit_bytes=VMEM_LIMIT_BYTES),
        name="attention",
    )(q4, proj, kv, kv, kv, vta, vta, vta, sink)


def _mlstm_kernel(qct_ref, kc_ref, vt_ref, o_ref, z_ref, grow_ref, ccol_ref, ng_ref,
                  out_ref, cct_s, nf_s, nb_s, mf_s, mb_s, *, seq):
    hd = pl.program_id(1)
    L = M_CHUNK
    dv = M_HEAD_DIM
    nc = seq // L

    head_row = lax.broadcasted_iota(jnp.int32, (M_HEADS, L), 0) == hd

    def gate_row(base, t0):
        rows = grow_ref[base:base + M_HEADS, pl.ds(t0, L)]
        return jnp.sum(jnp.where(head_row, rows, 0.0), axis=0, keepdims=True)

    U = MLSTM_SCAN_GROUP
    dirs = ((ROW_LI_F, ROW_B_F, L - 1, nf_s, mf_s, 0), (ROW_LI_B, ROW_B_B, 0, nb_s, mb_s, dv))

    def scan_body(gj, carry):
        cis = [[gj * U + u for u in range(U)], [nc - 1 - (gj * U + u) for u in range(U)]]
        t0s = [[pl.multiple_of(ci * L, L) for ci in row] for row in cis]
        lis = [[gate_row(dirs[d][0], t0) for t0 in t0s[d]] for d in range(2)]
        bs = [[gate_row(dirs[d][1], t0) for t0 in t0s[d]] for d in range(2)]
        b_lasts = [[b[:, dirs[d][2]:dirs[d][2] + 1] for b in bs[d]] for d in range(2)]
        gs = [[b_last - b + li for b_last, b, li in zip(b_lasts[d], bs[d], lis[d])] for d in range(2)]
        g_maxs = [[jnp.max(g, axis=-1, keepdims=True) for g in gs[d]] for d in range(2)]
        ks = [[kc_ref[pl.ds(t0, L), :] for t0 in t0s[d]] for d in range(2)]
        vts = [[vt_ref[:, pl.ds(t0, L)].astype(F32) for t0 in t0s[d]] for d in range(2)]

        ms = [[carry[d][2]] for d in range(2)]
        for d in range(2):
            for u in range(U):
                ms[d].append(jnp.maximum(b_lasts[d][u] + ms[d][u], g_maxs[d][u]))
        w_cs = [[jnp.exp(b_lasts[d][u] + ms[d][u] - ms[d][u + 1]) for u in range(U)] for d in range(2)]
        w_ks = [[jnp.exp(gs[d][u] - ms[d][u + 1]) for u in range(U)] for d in range(2)]
        c_upds = [[jnp.dot((vts[d][u] * w_ks[d][u]).astype(BF16), ks[d][u], preferred_element_type=F32)
                   for u in range(U)] for d in range(2)]
        n_upds = [[jnp.dot(jnp.broadcast_to(w_ks[d][u], (SUBLANES, L)).astype(BF16), ks[d][u],
                           preferred_element_type=F32)[0:1, :] for u in range(U)] for d in range(2)]

        out = []
        for d in range(2):
            _, _, _, n_s, m_s, row0 = dirs[d]
            ct_st, n_st, _ = carry[d]
            for u in range(U):
                ci = cis[d][u]
                cct_s[ci, row0:row0 + dv, :] = ct_st.astype(BF16)
                n_s[ci] = jnp.broadcast_to(n_st, (SUBLANES, LANES))
                m_s[ci] = jnp.broadcast_to(ms[d][u], (SUBLANES, LANES))
                ct_st = w_cs[d][u] * ct_st + c_upds[d][u]
                n_st = w_cs[d][u] * n_st + n_upds[d][u]
            out.append((ct_st, n_st, ms[d][U]))
        return tuple(out)

    init = (jnp.zeros((dv, M_HEAD_DIM), F32), jnp.zeros((1, M_HEAD_DIM), F32),
            jnp.full((1, 1), NEG, F32))
    lax.fori_loop(0, nc // U, scan_body, (init, init))

    rr = lax.broadcasted_iota(jnp.int32, (L, L), 0)
    cc = lax.broadcasted_iota(jnp.int32, (L, L), 1)
    row8 = lax.broadcasted_iota(jnp.int32, (SUBLANES, LANES), 0)
    sel_k = lax.broadcasted_iota(jnp.int32, (LANES, 2 * L), 0)
    sel_n = lax.broadcasted_iota(jnp.int32, (LANES, 2 * L), 1)
    sel_lane = jnp.where(sel_n < L, LANE_C_F + hd, LANE_C_B + hd)
    sel = jnp.where(sel_k == sel_lane, 1.0, 0.0).astype(BF16)
    keep_f = rr <= cc
    keep_b = rr >= cc
    G = MLSTM_OUT_GROUP

    def out_body(gi, carry):
        cis = [gi * G + g for g in range(G)]
        t0s = [pl.multiple_of(ci * L, L) for ci in cis]
        qts = [qct_ref[:, pl.ds(t0, L)] for t0 in t0s]
        qk_ts = [jnp.dot(kc_ref[pl.ds(t0, L), :], qt, preferred_element_type=F32)
                 for t0, qt in zip(t0s, qts)]
        cqs = [jnp.dot(cct_s[ci], qt, preferred_element_type=F32) for ci, qt in zip(cis, qts)]
        n8s = [jnp.where(row8 == 0, nf_s[ci], jnp.where(row8 == 1, nb_s[ci], 0.0)).astype(BF16) for ci in cis]
        qns = [jnp.dot(n8, qt, preferred_element_type=F32) for n8, qt in zip(n8s, qts)]
        c_bcs = [sum(jnp.dot(ccol_ref[ti, pl.ds(t0, L), :], sel, preferred_element_type=F32) for ti in range(3))
                 for t0 in t0s]

        items = []
        for g in range(G):
            items.append((g, c_bcs[g][:, :L], qns[g][0:1, :], ROW_B_F, keep_f, mf_s))
            items.append((g, c_bcs[g][:, L:], qns[g][1:2, :], ROW_B_B, keep_b, mb_s))
        b_rs = [gate_row(b_base, t0s[g]) for g, _, _, b_base, _, _ in items]
        m_prevs = [m_s[cis[g]][0:1, 0:1] for g, _, _, _, _, m_s in items]
        ds = [jnp.where(keep, b_r + c_s, NEG)
              for (_, c_s, _, _, keep, _), b_r in zip(items, b_rs)]
        m_inters = [b_r + m_prev for b_r, m_prev in zip(b_rs, m_prevs)]
        m_ts = [jnp.maximum(m_inter, jnp.max(d, axis=0, keepdims=True)) for m_inter, d in zip(m_inters, ds)]
        w_inters = [jnp.exp(m_inter - m_t) for m_inter, m_t in zip(m_inters, m_ts)]
        a_s = [jnp.exp(d - m_t) * qk_ts[it[0]] for d, m_t, it in zip(ds, m_ts, items)]
        dens = [w_inter * it[2] + jnp.sum(a, axis=0, keepdims=True) for w_inter, it, a in zip(w_inters, items, a_s)]
        invs = [1.0 / jnp.maximum(jnp.abs(den), jnp.exp(-m_t)) for den, m_t in zip(dens, m_ts)]
        ps = [a * inv for a, inv in zip(a_s, invs)]
        ws = [w_inter * inv for w_inter, inv in zip(w_inters, invs)]

        h_ts = [jnp.dot(vt_ref[:, pl.ds(t0s[g], L)], (ps[2 * g] + ps[2 * g + 1]).astype(BF16),
                        preferred_element_type=F32)
                + cqs[g][:dv, :] * ws[2 * g] + cqs[g][dv:, :] * ws[2 * g + 1] for g in range(G)]
        hs = [_sigmoid(o_ref[pl.ds(t0, L), :].astype(F32)) * h_t.T for t0, h_t in zip(t0s, h_ts)]
        hs = [h * lax.rsqrt(jnp.mean(h * h, axis=-1, keepdims=True) + NORM_EPS) * ng_ref[...] for h in hs]
        for t0, h in zip(t0s, hs):
            out_ref[pl.ds(t0, L), :] = (h * _silu(z_ref[pl.ds(t0, L), :].astype(F32))).astype(out_ref.dtype)
        return carry

    lax.fori_loop(0, nc // G, out_body, 0)


def _mlstm(proj, qct, kc, vtm, grow, ccol, ng, batch, seq):
    t = proj.shape[0]
    nc = seq // M_CHUNK
    head_rows = lambda b, h: (h, b)
    head_cols = lambda b, h: (b, h)

    def proj_head(col0):
        return pl.BlockSpec((seq, M_HEAD_DIM), lambda b, h: (b, col0 // M_HEAD_DIM + h))

    return pl.pallas_call(
        functools.partial(_mlstm_kernel, seq=seq),
        grid=(batch, M_HEADS),
        in_specs=[
            pl.BlockSpec((M_HEAD_DIM, seq), head_rows),
            pl.BlockSpec((seq, M_HEAD_DIM), head_cols),
            pl.BlockSpec((M_HEAD_DIM, seq), head_rows),
            proj_head(P_MO), proj_head(P_MZ),
            pl.BlockSpec((N_IFG, seq), lambda b, h: (0, b)),
            pl.BlockSpec((3, seq, LANES), lambda b, h: (0, b, 0)),
            pl.BlockSpec((1, M_HEAD_DIM), lambda b, h: (0, h)),
        ],
        out_specs=pl.BlockSpec((seq, M_HEAD_DIM), head_cols),
        out_shape=jax.ShapeDtypeStruct((t, M_WIDTH), BF16),
        scratch_shapes=[
            pltpu.VMEM((nc, 2 * M_HEAD_DIM, M_HEAD_DIM), BF16),
            pltpu.VMEM((nc, SUBLANES, M_HEAD_DIM), F32),
            pltpu.VMEM((nc, SUBLANES, M_HEAD_DIM), F32),
            pltpu.VMEM((nc, SUBLANES, LANES), F32),
            pltpu.VMEM((nc, SUBLANES, LANES), F32),
        ],
        compiler_params=pltpu.CompilerParams(
            dimension_semantics=("arbitrary", "arbitrary"), vmem_limit_bytes=VMEM_LIMIT_BYTES),
        name="mlstm",
    )(qct, kc, vtm, proj, proj, grow, ccol, ng)


def _out_proj_kernel(x_ref, a_ref, m_ref, gates_ref, wa_ref, wm_ref, wo_ref, out_ref):
    branch_a = jnp.dot(a_ref[...], wa_ref[...], preferred_element_type=F32)
    branch_m = jnp.dot(m_ref[...], wm_ref[...], preferred_element_type=F32)
    gates = _sigmoid(gates_ref[...].astype(F32))
    merged = gates[:, :D_MODEL] * branch_a + gates[:, D_MODEL:] * branch_m
    out_ref[...] = x_ref[...] + jnp.dot(merged.astype(BF16), wo_ref[...], preferred_element_type=F32)


def _out_proj(x, a, m, proj, wa, wm, wo):
    t = x.shape[0]
    tile = lambda i: (i, 0)
    const = lambda i: (0, 0)
    return pl.pallas_call(
        _out_proj_kernel,
        grid=(t // OUT_TM,),
        in_specs=[
            pl.BlockSpec((OUT_TM, D_MODEL), tile),
            pl.BlockSpec((OUT_TM, ATT_WIDTH), tile),
            pl.BlockSpec((OUT_TM, M_WIDTH), tile),
            pl.BlockSpec((OUT_TM, 2 * D_MODEL), lambda i: (i, P_GATES // (2 * D_MODEL))),
            pl.BlockSpec((ATT_WIDTH, D_MODEL), const),
            pl.BlockSpec((M_WIDTH, D_MODEL), const),
            pl.BlockSpec((D_MODEL, D_MODEL), const),
        ],
        out_specs=pl.BlockSpec((OUT_TM, D_MODEL), tile),
        out_shape=jax.ShapeDtypeStruct((t, D_MODEL), F32),
        compiler_params=pltpu.CompilerParams(
            dimension_semantics=("arbitrary",), vmem_limit_bytes=VMEM_LIMIT_BYTES),
        name="out_proj",
    )(x, a, m, proj, wa, wm, wo)


def _rope_tables(seq):
    inv = jnp.power(jnp.float32(ROPE_THETA), -jnp.arange(ROPE_HALF, dtype=F32) * 2.0 / ROPE_DIM)
    ang = jnp.arange(seq, dtype=F32)[:, None] * inv[None, :]
    cos, sin = jnp.cos(ang), jnp.sin(ang)
    pad = ATT_HEAD_DIM - ROPE_DIM
    ones = jnp.ones((seq, pad), F32)
    zeros = jnp.zeros((seq, pad), F32)
    zh = jnp.zeros((seq, ROPE_HALF), F32)
    c = jnp.concatenate([cos, cos, ones], axis=-1)
    s1 = jnp.concatenate([-sin, zh, zeros], axis=-1)
    s2 = jnp.concatenate([zh, sin, zeros], axis=-1)
    tab = jnp.stack([c, s1, s2])
    return jnp.concatenate([tab, tab], axis=-1)


def _prep_weights(w_in, b_in):
    order = [(_R_GATES, 2 * D_MODEL), (_R_AZ, ATT_WIDTH), (_R_MO, M_WIDTH), (_R_MZ, M_WIDTH),
             (_R_MQ, M_WIDTH), (_R_MK, M_WIDTH), (_R_AQ, ATT_WIDTH), (_R_AK, ATT_KV_WIDTH)]
    w_main = jnp.concatenate([w_in[:, :, o:o + n] for o, n in order], axis=-1).astype(BF16)
    b_main = jnp.concatenate([b_in[:, o:o + n] for o, n in order], axis=-1)[:, None, :]
    vt_order = [(_R_MV, M_WIDTH), (_R_AV, ATT_KV_WIDTH), (_R_IF, N_IFG)]
    w_vt = jnp.swapaxes(jnp.concatenate([w_in[:, :, o:o + n] for o, n in vt_order], axis=-1), 1, 2).astype(BF16)
    b_vt = jnp.concatenate([b_in[:, o:o + n] for o, n in vt_order], axis=-1)[:, :, None]
    w_if = w_in[:, :, _R_IF:_R_IF + N_IFG]
    b_if = b_in[:, _R_IF:_R_IF + N_IFG]
    w_if_pad = jnp.pad(w_if, ((0, 0), (0, 0), (0, LANES - N_IFG))).astype(BF16)
    b_if_pad = jnp.pad(b_if, ((0, 0), (0, LANES - N_IFG)))[:, None, :]
    return w_main, b_main, w_vt, b_vt, w_if_pad, b_if_pad


def _trunk(x, rope_tab, layers):
    batch, seq, _ = x.shape
    xf = x.reshape(batch * seq, D_MODEL)
    for lw in layers:
        proj, q4, kv, vta, qct, kc, vtm, grow, ccol = _in_proj(xf, lw, rope_tab, seq)
        a = _attention(proj, q4, kv, vta, lw["sink"], batch, seq)
        m = _mlstm(proj, qct, kc, vtm, grow, ccol, lw["m_norm_g"], batch, seq)
        xf = _out_proj(xf, a, m, proj, lw["w_att_out"], lw["w_m_out"], lw["w_out"])
    return xf.reshape(batch, seq, D_MODEL)


def kernel(x_prompt, x_sample, norm_g, w_in, b_in, q_norm_g, k_norm_g, sink, conv_w, m_norm_g,
           w_att_out, w_m_out, w_out):
    w_main, b_main, w_vt, b_vt, w_if, b_if = _prep_weights(w_in, b_in)
    wa, wm, wo = w_att_out.astype(BF16), w_m_out.astype(BF16), w_out.astype(BF16)
    layers = []
    for l in range(DEPTH):
        layers.append(dict(
            norm_g=norm_g[l][None, :], w_main=w_main[l], b_main=b_main[l], w_vt=w_vt[l], b_vt=b_vt[l],
            w_if=w_if[l], b_if=b_if[l],
            gq=jnp.tile(q_norm_g[l] * (ATT_HEAD_DIM ** -0.5 * LOG2E), 2)[None, :],
            gk=jnp.tile(k_norm_g[l], 2)[None, :],
            sink=sink[l][None, :], conv_w=conv_w[l], m_norm_g=m_norm_g[l][None, :],
            w_att_out=wa[l], w_m_out=wm[l], w_out=wo[l]))
    outs = []
    for x in (x_prompt, x_sample):
        outs.append(_trunk(x, _rope_tables(x.shape[1]), layers))
    return tuple(outs)
```

```python
import functools
import math

import jax
import jax.numpy as jnp
from jax import lax
from jax.experimental import pallas as pl
from jax.experimental.pallas import tpu as pltpu

F32 = jnp.float32
BF16 = jnp.bfloat16

D_MODEL = 1024
DEPTH = 4
ATT_HEADS = 16
ATT_KV_HEADS = 4
ATT_GROUP = ATT_HEADS // ATT_KV_HEADS
ATT_HEAD_DIM = 64
ATT_WIDTH = ATT_HEADS * ATT_HEAD_DIM
ATT_KV_WIDTH = ATT_KV_HEADS * ATT_HEAD_DIM
WINDOW = 128
ATT_BLOCK = 128
ROPE_THETA = 500000.0
ROPE_DIM = ATT_HEAD_DIM // 4
ROPE_HALF = ROPE_DIM // 2
M_HEADS = 8
M_HEAD_DIM = 128
M_WIDTH = M_HEADS * M_HEAD_DIM
M_CHUNK = 128
CONV_K = 3
NORM_EPS = 1e-6
NEG = -1e30
LOG2E = math.log2(math.e)

LANES = 128
SUBLANES = 8
VMEM_LIMIT_BYTES = 56 * 1024 * 1024
_REF_SPLITS = (ATT_WIDTH, ATT_KV_WIDTH, ATT_KV_WIDTH, ATT_WIDTH,
               M_WIDTH, M_WIDTH, M_WIDTH, M_WIDTH, M_WIDTH,
               M_HEADS, M_HEADS, M_HEADS, M_HEADS, 2 * D_MODEL)
_REF_OFF = [0]
for _w in _REF_SPLITS:
    _REF_OFF.append(_REF_OFF[-1] + _w)
(_R_AQ, _R_AK, _R_AV, _R_AZ, _R_MQ, _R_MK, _R_MV, _R_MO, _R_MZ,
 _R_IF, _R_FF, _R_IB, _R_FB, _R_GATES) = _REF_OFF[:-1]

P_GATES = 0
P_AZ = P_GATES + 2 * D_MODEL
P_MO = P_AZ + ATT_WIDTH
P_MZ = P_MO + M_WIDTH
P_WIDTH = P_MZ + M_WIDTH
W_MQ = P_WIDTH
W_MK = W_MQ + M_WIDTH
W_AQ = W_MK + M_WIDTH
W_AK = W_AQ + ATT_WIDTH
W_WIDTH = W_AK + ATT_KV_WIDTH
VT_ROWS = M_WIDTH + ATT_KV_WIDTH
N_IFG = 4 * M_HEADS

ROW_LI_F, ROW_B_F, ROW_LI_B, ROW_B_B = 0, M_HEADS, 2 * M_HEADS, 3 * M_HEADS
LANE_C_F, LANE_C_B = 0, 2 * M_HEADS

PROJ_DTYPE = BF16
IN_TM = 512
IN_TN = 512
IN_EPILOGUES_PER_MATMUL = 2
IN_HALO = 16
IN_PIECE = 256
VT_TN = 256
OUT_TM = 1024
ATT_TQ = 512
ATT_NSB = ATT_TQ // ATT_BLOCK
ATT_SUB_GROUP = 4
ATT_PAIRS = ATT_WIDTH // LANES
ATT_KVAR_WIDTH = ATT_KV_HEADS * 2 * LANES
MLSTM_SCAN_GROUP = 16
MLSTM_OUT_GROUP = 16

_NT = (((1,), (1,)), ((), ()))


def _sigmoid(x):
    return 1.0 / (1.0 + jnp.exp(-x))


def _silu(x):
    return x * _sigmoid(x)


def _log_sigmoid(x):
    return jnp.minimum(x, 0.0) - jnp.log1p(jnp.exp(-jnp.abs(x)))


def _bf16_terms(x):
    hi = x.astype(BF16)
    r1 = x - hi.astype(F32)
    mid = r1.astype(BF16)
    lo = (r1 - mid.astype(F32)).astype(BF16)
    return hi, mid, lo


def _split_dot_left(sel, x):
    return sum(jnp.dot(sel, t, preferred_element_type=F32) for t in _bf16_terms(x))


def _split_dot_right(x, sel):
    return sum(jnp.dot(t, sel, preferred_element_type=F32) for t in _bf16_terms(x))


def _rope(y, tab):
    return (y * tab[0]
            + pltpu.roll(y, LANES - ROPE_HALF, 1) * tab[1]
            + pltpu.roll(y, ROPE_HALF, 1) * tab[2])


def _norm_rope_tile(x, gain, tab):
    lane = lax.broadcasted_iota(jnp.int32, x.shape, 1)
    left = lane < ATT_HEAD_DIM
    sq = x * x
    ss_l = jnp.sum(jnp.where(left, sq, 0.0), axis=-1, keepdims=True)
    ss_r = jnp.sum(jnp.where(left, 0.0, sq), axis=-1, keepdims=True)
    inv = jnp.where(left, lax.rsqrt(ss_l / ATT_HEAD_DIM + NORM_EPS),
                    lax.rsqrt(ss_r / ATT_HEAD_DIM + NORM_EPS))
    return _rope(x * inv * gain, tab)


def _rmsnorm_bf16(x, g):
    return (x * lax.rsqrt(jnp.mean(x * x, axis=-1, keepdims=True) + NORM_EPS) * g).astype(BF16)


def _in_proj_kernel(x_ref, xprev_ref, xnext_ref, g_ref, w_ref, b_ref, wvt_ref, bvt_ref,
                    wif_ref, bif_ref, cw_ref, tab_ref, gq_ref, gk_ref,
                    proj_ref, q4_ref, kv_ref, vta_ref, qct_ref, kc_ref, vtm_ref, grow_ref, ccol_ref,
                    xn_s, conv_s, *, tiles_per_seq):
    tm = x_ref.shape[0]
    tile_in_seq = pl.program_id(0) % tiles_per_seq

    xn_s[0:IN_HALO, :] = _rmsnorm_bf16(xprev_ref[...], g_ref[...])
    xn_s[IN_HALO:IN_HALO + tm, :] = _rmsnorm_bf16(x_ref[...], g_ref[...])
    xn_s[IN_HALO + tm:, :] = _rmsnorm_bf16(xnext_ref[...], g_ref[...])
    xn = xn_s[IN_HALO:IN_HALO + tm, :]
    xn_halo = xn_s[...]

    def chunk(c0, width, lhs=None):
        lhs = xn if lhs is None else lhs

        def piece(p0):
            cols = slice(p0, p0 + IN_PIECE)
            return lambda: jnp.dot(lhs, w_ref[:, cols], preferred_element_type=F32) + b_ref[:, cols]
        return [piece(p0) for p0 in range(c0, c0 + width, IN_PIECE)]

    stages = []

    def store_plain(c0):
        def store(acc, p0):
            def run():
                proj_ref[:, p0:p0 + IN_PIECE] = acc.astype(proj_ref.dtype)
            return run
        return lambda accs: [store(acc, c0 + i * IN_PIECE) for i, acc in enumerate(accs)]

    plain = [(chunk(c0, IN_TN), store_plain(c0)) for c0 in range(0, P_WIDTH, IN_TN)]

    has_before = tile_in_seq > 0
    has_after = tile_in_seq < tiles_per_seq - 1
    row_id = lax.broadcasted_iota(jnp.int32, (tm, LANES), 0)
    kscale = M_HEAD_DIM ** -0.5
    lane_tiles = range(0, IN_PIECE, LANES)

    def conv_silu(c0):
        def conv(acc, p0, l0):
            cols = slice(p0 + l0, p0 + l0 + LANES)

            def run():
                buf = conv_s.at[((p0 + l0) // LANES) % 2]
                buf[...] = acc[:, l0:l0 + LANES]
                lo, hi = IN_HALO - 1, IN_HALO + tm
                buf[lo:lo + 1, :] = jnp.where(has_before, acc[lo:lo + 1, l0:l0 + LANES], 0.0)
                buf[hi:hi + 1, :] = jnp.where(has_after, acc[hi:hi + 1, l0:l0 + LANES], 0.0)
                a = acc[IN_HALO:IN_HALO + tm, l0:l0 + LANES]
                xm = buf[IN_HALO - 1:IN_HALO - 1 + tm, :]
                xp = buf[IN_HALO + 1:IN_HALO + 1 + tm, :]
                y = _silu(xm * cw_ref[0:1, cols] + a * cw_ref[1:2, cols] + xp * cw_ref[2:3, cols])
                if p0 < M_WIDTH:
                    qct_ref[cols, :] = y.T.astype(BF16)
                else:
                    kc_ref[:, p0 + l0 - M_WIDTH:p0 + l0 - M_WIDTH + LANES] = (y * kscale).astype(BF16)
            return run
        return lambda accs: [conv(acc, c0 + i * IN_PIECE, l0) for i, acc in enumerate(accs) for l0 in lane_tiles]

    conv = [(chunk(W_MQ + c0, IN_TN, lhs=xn_halo), conv_silu(c0)) for c0 in range(0, 2 * M_WIDTH, IN_TN)]

    tab = tab_ref[...]

    def q_rope(c0):
        def rope(acc, p0, l0):
            def run():
                y = _norm_rope_tile(acc[:, l0:l0 + LANES], gq_ref[...], tab).astype(BF16)
                for j in range(tm // ATT_BLOCK):
                    q4_ref[j, (p0 + l0) // LANES] = y[j * ATT_BLOCK:(j + 1) * ATT_BLOCK, :]
            return run
        return lambda accs: [rope(acc, c0 + i * IN_PIECE, l0) for i, acc in enumerate(accs) for l0 in lane_tiles]

    att_q = [(chunk(W_AQ + c0, IN_TN), q_rope(c0)) for c0 in range(0, ATT_WIDTH, IN_TN)]

    left_half = lax.broadcasted_iota(jnp.int32, (tm, LANES), 1) < ATT_HEAD_DIM

    def k_norm_rope(accs):
        def norm_rope(acc, p0):
            def run():
                for t in range(IN_PIECE // LANES):
                    kt = _norm_rope_tile(acc[:, t * LANES:(t + 1) * LANES], gk_ref[...], tab)
                    sw = pltpu.roll(kt, ATT_HEAD_DIM, 1)
                    head = (p0 // LANES + t) * 2
                    variants = ((head, 0, jnp.where(left_half, kt, 0.0)), (head, 1, jnp.where(left_half, 0.0, sw)),
                                (head + 1, 0, jnp.where(left_half, sw, 0.0)), (head + 1, 1, jnp.where(left_half, 0.0, kt)))
                    for h, v, val in variants:
                        col = (2 * h + v) * LANES
                        kv_ref[:, col:col + LANES] = val.astype(BF16)
            return run
        return [norm_rope(acc, i * IN_PIECE) for i, acc in enumerate(accs)]

    att_k = (chunk(W_AK, ATT_KV_WIDTH), k_norm_rope)

    def vt_matmul():
        return lax.dot_general(wvt_ref[...], xn, _NT, preferred_element_type=F32) + bvt_ref[...]

    def store_vt(vt, r0):
        def run():
            if r0 < M_WIDTH:
                vtm_ref[r0:r0 + VT_TN, :] = vt[r0:r0 + VT_TN, :].astype(BF16)
            else:
                vta_ref[r0 - M_WIDTH:r0 - M_WIDTH + VT_TN, :] = vt[r0:r0 + VT_TN, :].astype(BF16)
        return run

    def gate_matmul():
        return jnp.dot(xn, wif_ref[...], preferred_element_type=F32) + bif_ref[...]

    def vt_gate_epilogues(results):
        vt, gc = results
        gr = vt[VT_ROWS:VT_ROWS + N_IFG, :]
        return [store_vt(vt, r0) for r0 in range(0, VT_ROWS, VT_TN)] + [lambda: gate_epilogue(gc, gr)]

    def gate_epilogue(gc, gr):
        ls_c = _log_sigmoid(gc)
        ls_r = _log_sigmoid(gr)
        r = lax.broadcasted_iota(jnp.int32, (M_CHUNK, M_CHUNK), 0)
        c = lax.broadcasted_iota(jnp.int32, (M_CHUNK, M_CHUNK), 1)
        tril = jnp.where(c <= r, 1.0, 0.0).astype(BF16)
        triu = jnp.where(c >= r, 1.0, 0.0).astype(BF16)
        lane = lax.broadcasted_iota(jnp.int32, (M_CHUNK, LANES), 1)
        fwd_lanes = lane < 2 * M_HEADS
        grow_ref[ROW_LI_F:ROW_LI_F + M_HEADS, :] = gr[0:M_HEADS, :]
        grow_ref[ROW_LI_B:ROW_LI_B + M_HEADS, :] = gr[2 * M_HEADS:3 * M_HEADS, :]
        for ci in range(tm // M_CHUNK):
            rows = slice(ci * M_CHUNK, (ci + 1) * M_CHUNK)
            lsc = ls_c[rows, :]
            b_f = _split_dot_left(tril, lsc)
            b_b = _split_dot_left(triu, lsc)
            cc = gc[rows, :] - pltpu.roll(jnp.where(fwd_lanes, b_f, b_b), LANES - M_HEADS, 1)
            for ti, term in enumerate(_bf16_terms(cc)):
                ccol_ref[ti, rows, :] = term
            grow_ref[ROW_B_F:ROW_B_F + M_HEADS, rows] = _split_dot_right(ls_r[M_HEADS:2 * M_HEADS, rows], triu)
            grow_ref[ROW_B_B:ROW_B_B + M_HEADS, rows] = _split_dot_right(ls_r[3 * M_HEADS:4 * M_HEADS, rows], tril)

    light = plain
    heavy = [([vt_matmul, gate_matmul], vt_gate_epilogues),
             conv[0], att_q[0], conv[1], att_q[1], conv[2], att_k, conv[3]]
    stages.append(heavy[0])
    for i, stage in enumerate(heavy[1:]):
        stages.extend([stage, light[i]])
    stages.extend(light[len(heavy) - 1:])

    n_matmuls = sum(len(matmuls) for matmuls, _ in stages)
    queue = []
    emitted = 0
    for matmuls, make_epilogues in stages:
        results = []
        for matmul in matmuls:
            results.append(matmul())
            emitted += 1
            take = max(IN_EPILOGUES_PER_MATMUL, -(-len(queue) // max(n_matmuls - emitted, 1))) if queue else 0
            for run in queue[:take]:
                run()
            queue = queue[take:]
        queue.extend(make_epilogues(results))
    for run in queue:
        run()


def _in_proj(x, lw, rope_tab, seq):
    t = x.shape[0]
    tps = seq // IN_TM
    halo_blocks = IN_TM // IN_HALO
    const = lambda i: (0, 0)
    tile = lambda i: (i, 0)
    cols = lambda i: (0, i)
    resident = dict(pipeline_mode=pl.Buffered(1))
    return pl.pallas_call(
        functools.partial(_in_proj_kernel, tiles_per_seq=tps),
        grid=(t // IN_TM,),
        in_specs=[
            pl.BlockSpec((IN_TM, D_MODEL), tile),
            pl.BlockSpec((IN_HALO, D_MODEL), lambda i: (jnp.maximum(i * halo_blocks - 1, 0), 0)),
            pl.BlockSpec((IN_HALO, D_MODEL), lambda i: (jnp.minimum((i + 1) * halo_blocks, t // IN_HALO - 1), 0)),
            pl.BlockSpec((1, D_MODEL), const),
            pl.BlockSpec((D_MODEL, W_WIDTH), const, **resident),
            pl.BlockSpec((1, W_WIDTH), const),
            pl.BlockSpec((VT_ROWS + N_IFG, D_MODEL), const, **resident),
            pl.BlockSpec((VT_ROWS + N_IFG, 1), const),
            pl.BlockSpec((D_MODEL, LANES), const),
            pl.BlockSpec((1, LANES), const),
            pl.BlockSpec((CONV_K, 2 * M_WIDTH), const),
            pl.BlockSpec((3, IN_TM, LANES), lambda i: (0, i % tps, 0)),
            pl.BlockSpec((1, LANES), const),
            pl.BlockSpec((1, LANES), const),
        ],
        out_specs=[
            pl.BlockSpec((IN_TM, P_WIDTH), tile),
            pl.BlockSpec((IN_TM // ATT_BLOCK, ATT_PAIRS, ATT_BLOCK, LANES), lambda i: (i, 0, 0, 0)),
            pl.BlockSpec((IN_TM, ATT_KVAR_WIDTH), tile),
            pl.BlockSpec((ATT_KV_WIDTH, IN_TM), cols),
            pl.BlockSpec((M_WIDTH, IN_TM), cols),
            pl.BlockSpec((IN_TM, M_WIDTH), tile),
            pl.BlockSpec((M_WIDTH, IN_TM), cols),
            pl.BlockSpec((N_IFG, IN_TM), cols),
            pl.BlockSpec((3, IN_TM, LANES), lambda i: (0, i, 0)),
        ],
        out_shape=[
            jax.ShapeDtypeStruct((t, P_WIDTH), PROJ_DTYPE),
            jax.ShapeDtypeStruct((t // ATT_BLOCK, ATT_PAIRS, ATT_BLOCK, LANES), BF16),
            jax.ShapeDtypeStruct((t, ATT_KVAR_WIDTH), BF16),
            jax.ShapeDtypeStruct((ATT_KV_WIDTH, t), BF16),
            jax.ShapeDtypeStruct((M_WIDTH, t), BF16),
            jax.ShapeDtypeStruct((t, M_WIDTH), BF16),
            jax.ShapeDtypeStruct((M_WIDTH, t), BF16),
            jax.ShapeDtypeStruct((N_IFG, t), F32),
            jax.ShapeDtypeStruct((3, t, LANES), BF16),
        ],
        scratch_shapes=[pltpu.VMEM((IN_TM + 2 * IN_HALO, D_MODEL), BF16),
                        pltpu.VMEM((2, IN_TM + 2 * IN_HALO, LANES), F32)],
        compiler_params=pltpu.CompilerParams(
            dimension_semantics=("arbitrary",), vmem_limit_bytes=VMEM_LIMIT_BYTES),
        name="in_proj",
    )(x, x, x, lw["norm_g"], lw["w_main"], lw["b_main"], lw["w_vt"], lw["b_vt"], lw["w_if"],
      lw["b_if"], lw["conv_w"], rope_tab, lw["gq"], lw["gk"])


def _attention_kernel(q4_ref, z_ref, kp_ref, kc_ref, kn_ref, vp_ref, vc_ref, vn_ref, sink_ref, out_ref,
                      ks, vts, out_t, *, n_blocks):
    ti = pl.program_id(1)
    bq = ATT_BLOCK

    r0 = 0
    for k_ref, v_ref in ((kp_ref, vp_ref), (kc_ref, vc_ref), (kn_ref, vn_ref)):
        nr = k_ref.shape[0]
        ks[r0:r0 + nr, :] = k_ref[...]
        vts[:, r0:r0 + nr] = v_ref[...]
        r0 += nr

    def group_heads(h):
        return (ATT_GROUP * h, ATT_GROUP * h + 2, ATT_GROUP * h + 1, ATT_GROUP * h + 3)

    sink_rows = [jnp.concatenate([jnp.broadcast_to(sink_ref[:, c:c + 1] * LOG2E, (1, bq)) for c in group_heads(h)],
                                 axis=1) for h in range(ATT_KV_HEADS)]

    kk = lax.broadcasted_iota(jnp.int32, (bq, bq), 0)
    qq = lax.broadcasted_iota(jnp.int32, (bq, bq), 1)
    prev_in_band = kk >= qq
    next_in_band = kk <= qq

    JG = ATT_SUB_GROUP

    def sub_blocks(jg, carry):
        js = [jg * JG + u for u in range(JG)]
        row0s = [pl.multiple_of(j * bq, bq) for j in js]
        biases = []
        for j in js:
            blk = ti * ATT_NSB + j
            bias_prev = jnp.where(prev_in_band, jnp.where(blk > 0, 0.0, NEG), NEG).astype(F32)
            bias_next = jnp.where(next_in_band, jnp.where(blk < n_blocks - 1, 0.0, NEG), NEG).astype(F32)
            biases.append((jnp.concatenate([bias_prev] * ATT_GROUP, axis=1),
                           jnp.concatenate([bias_next] * ATT_GROUP, axis=1)))
        q_slabs = [q4_ref[j] for j in js]
        pairs = [(u, h) for u in range(JG) for h in range(ATT_KV_HEADS)]
        q_pairs = [q_slabs[u][2 * h:2 * h + 2].reshape(2 * bq, LANES) for u, h in pairs]
        ss = [jnp.concatenate(
            [lax.dot_general(ks[pl.ds(row0s[u], 3 * bq), (2 * h + v) * LANES:(2 * h + v + 1) * LANES], q_pair, _NT,
                             preferred_element_type=F32)
             for v in range(2)], axis=1) for (u, h), q_pair in zip(pairs, q_pairs)]
        ss = [jnp.concatenate([s[:bq] + biases[u][0], s[bq:2 * bq], s[2 * bq:] + biases[u][1]], axis=0)
              for (u, h), s in zip(pairs, ss)]
        ms = [jnp.maximum(jnp.max(s, axis=0, keepdims=True), sink_rows[h]) for (u, h), s in zip(pairs, ss)]
        ps = [jnp.exp2(s - m) for s, m in zip(ss, ms)]
        denoms = [jnp.sum(p, axis=0, keepdims=True) + jnp.exp2(sink_rows[h] - m)
                  for (u, h), p, m in zip(pairs, ps, ms)]
        os_ = [jnp.dot(vts[h * ATT_HEAD_DIM:(h + 1) * ATT_HEAD_DIM, pl.ds(row0s[u], 3 * bq)], p.astype(BF16),
                       preferred_element_type=F32) * (1.0 / denom)
               for (u, h), p, denom in zip(pairs, ps, denoms)]
        for (u, h), o in zip(pairs, os_):
            for i, c in enumerate(group_heads(h)):
                out_t[u, c * ATT_HEAD_DIM:(c + 1) * ATT_HEAD_DIM, :] = o[:, i * bq:(i + 1) * bq]
        for u in range(JG):
            att = out_t[u].T
            z = z_ref[pl.ds(row0s[u], bq), :].astype(F32)
            out_ref[pl.ds(row0s[u], bq), :] = (att * _silu(z)).astype(out_ref.dtype)
        return carry

    lax.fori_loop(0, ATT_NSB // JG, sub_blocks, 0)


def _attention(proj, q4, kv, vta, sink, batch, seq):
    t = proj.shape[0]
    nb = seq // ATT_BLOCK
    nt = seq // ATT_TQ
    bq = ATT_BLOCK

    prev_blk = lambda b, i: b * nb + jnp.maximum(i * ATT_NSB - 1, 0)
    next_blk = lambda b, i: b * nb + jnp.minimum((i + 1) * ATT_NSB, nb - 1)
    return pl.pallas_call(
        functools.partial(_attention_kernel, n_blocks=nb),
        grid=(batch, nt),
        in_specs=[
            pl.BlockSpec((ATT_NSB, ATT_PAIRS, bq, LANES), lambda b, i: (b * nt + i, 0, 0, 0)),
            pl.BlockSpec((ATT_TQ, ATT_WIDTH), lambda b, i: (b * nt + i, P_AZ // ATT_WIDTH)),
            pl.BlockSpec((bq, ATT_KVAR_WIDTH), lambda b, i: (prev_blk(b, i), 0)),
            pl.BlockSpec((ATT_TQ, ATT_KVAR_WIDTH), lambda b, i: (b * nt + i, 0)),
            pl.BlockSpec((bq, ATT_KVAR_WIDTH), lambda b, i: (next_blk(b, i), 0)),
            pl.BlockSpec((ATT_KV_WIDTH, bq), lambda b, i: (0, prev_blk(b, i))),
            pl.BlockSpec((ATT_KV_WIDTH, ATT_TQ), lambda b, i: (0, b * nt + i)),
            pl.BlockSpec((ATT_KV_WIDTH, bq), lambda b, i: (0, next_blk(b, i))),
            pl.BlockSpec((1, ATT_HEADS), lambda b, i: (0, 0)),
        ],
        out_specs=pl.BlockSpec((ATT_TQ, ATT_WIDTH), lambda b, i: (b * nt + i, 0)),
        out_shape=jax.ShapeDtypeStruct((t, ATT_WIDTH), BF16),
        scratch_shapes=[
            pltpu.VMEM((ATT_TQ + 2 * bq, ATT_KVAR_WIDTH), BF16),
            pltpu.VMEM((ATT_KV_WIDTH, ATT_TQ + 2 * bq), BF16),
            pltpu.VMEM((ATT_SUB_GROUP, ATT_WIDTH, bq), F32),
        ],
        compiler_params=pltpu.CompilerParams(
            dimension_semantics=("arbitrary", "arbitrary"), vmem_limit_bytes=VMEM_LIMIT_BYTES),
        name="attention",
    )(q4, proj, kv, kv, kv, vta, vta, vta, sink)


def _mlstm_kernel(qct_ref, kc_ref, vt_ref, o_ref, z_ref, grow_ref, ccol_ref, ng_ref,
                  out_ref, cct_s, nf_s, nb_s, mf_s, mb_s, *, seq):
    hd = pl.program_id(1)
    L = M_CHUNK
    dv = M_HEAD_DIM
    nc = seq // L

    head_row = lax.broadcasted_iota(jnp.int32, (M_HEADS, L), 0) == hd

    def gate_row(base, t0):
        rows = grow_ref[base:base + M_HEADS, pl.ds(t0, L)]
        return jnp.sum(jnp.where(head_row, rows, 0.0), axis=0, keepdims=True)

    U = MLSTM_SCAN_GROUP
    dirs = ((ROW_LI_F, ROW_B_F, L - 1, nf_s, mf_s, 0), (ROW_LI_B, ROW_B_B, 0, nb_s, mb_s, dv))

    def scan_body(gj, carry):
        cis = [[gj * U + u for u in range(U)], [nc - 1 - (gj * U + u) for u in range(U)]]
        t0s = [[pl.multiple_of(ci * L, L) for ci in row] for row in cis]
        lis = [[gate_row(dirs[d][0], t0) for t0 in t0s[d]] for d in range(2)]
        bs = [[gate_row(dirs[d][1], t0) for t0 in t0s[d]] for d in range(2)]
        b_lasts = [[b[:, dirs[d][2]:dirs[d][2] + 1] for b in bs[d]] for d in range(2)]
        gs = [[b_last - b + li for b_last, b, li in zip(b_lasts[d], bs[d], lis[d])] for d in range(2)]
        g_maxs = [[jnp.max(g, axis=-1, keepdims=True) for g in gs[d]] for d in range(2)]
        ks = [[kc_ref[pl.ds(t0, L), :] for t0 in t0s[d]] for d in range(2)]
        vts = [[vt_ref[:, pl.ds(t0, L)].astype(F32) for t0 in t0s[d]] for d in range(2)]

        ms = [[carry[d][2]] for d in range(2)]
        for d in range(2):
            for u in range(U):
                ms[d].append(jnp.maximum(b_lasts[d][u] + ms[d][u], g_maxs[d][u]))
        w_cs = [[jnp.exp(b_lasts[d][u] + ms[d][u] - ms[d][u + 1]) for u in range(U)] for d in range(2)]
        w_ks = [[jnp.exp(gs[d][u] - ms[d][u + 1]) for u in range(U)] for d in range(2)]
        c_upds = [[jnp.dot((vts[d][u] * w_ks[d][u]).astype(BF16), ks[d][u], preferred_element_type=F32)
                   for u in range(U)] for d in range(2)]
        n_upds = [[jnp.dot(jnp.broadcast_to(w_ks[d][u], (SUBLANES, L)).astype(BF16), ks[d][u],
                           preferred_element_type=F32)[0:1, :] for u in range(U)] for d in range(2)]

        out = []
        for d in range(2):
            _, _, _, n_s, m_s, row0 = dirs[d]
            ct_st, n_st, _ = carry[d]
            for u in range(U):
                ci = cis[d][u]
                cct_s[ci, row0:row0 + dv, :] = ct_st.astype(BF16)
                n_s[ci] = jnp.broadcast_to(n_st, (SUBLANES, LANES))
                m_s[ci] = jnp.broadcast_to(ms[d][u], (SUBLANES, LANES))
                ct_st = w_cs[d][u] * ct_st + c_upds[d][u]
                n_st = w_cs[d][u] * n_st + n_upds[d][u]
            out.append((ct_st, n_st, ms[d][U]))
        return tuple(out)

    rr = lax.broadcasted_iota(jnp.int32, (L, L), 0)
    cc = lax.broadcasted_iota(jnp.int32, (L, L), 1)
    row8 = lax.broadcasted_iota(jnp.int32, (SUBLANES, LANES), 0)
    sel_k = lax.broadcasted_iota(jnp.int32, (LANES, 2 * L), 0)
    sel_n = lax.broadcasted_iota(jnp.int32, (LANES, 2 * L), 1)
    sel_lane = jnp.where(sel_n < L, LANE_C_F + hd, LANE_C_B + hd)
    sel = jnp.where(sel_k == sel_lane, 1.0, 0.0).astype(BF16)
    keep_f = rr <= cc
    keep_b = rr >= cc
    G = MLSTM_OUT_GROUP

    def out_pre(gi):
        cis = [gi * G + g for g in range(G)]
        t0s = [ci * L for ci in cis]
        qts = [qct_ref[:, t0:t0 + L] for t0 in t0s]
        qk_ts = [jnp.dot(kc_ref[t0:t0 + L, :], qt, preferred_element_type=F32)
                 for t0, qt in zip(t0s, qts)]
        c_bcs = [sum(jnp.dot(ccol_ref[ti, t0:t0 + L, :], sel, preferred_element_type=F32) for ti in range(3))
                 for t0 in t0s]
        items = []
        for g in range(G):
            items.append((g, c_bcs[g][:, :L], ROW_B_F, keep_f, mf_s))
            items.append((g, c_bcs[g][:, L:], ROW_B_B, keep_b, mb_s))
        b_rs = [gate_row(b_base, t0s[g]) for g, _, b_base, _, _ in items]
        ds = [jnp.where(keep, b_r + c_s, NEG)
              for (_, c_s, _, keep, _), b_r in zip(items, b_rs)]
        d_maxs = [jnp.max(d, axis=0, keepdims=True) for d in ds]
        return cis, t0s, qts, qk_ts, items, b_rs, ds, d_maxs

    def out_post(pre):
        cis, t0s, qts, qk_ts, items, b_rs, ds, d_maxs = pre
        cqs = [jnp.dot(cct_s[ci], qt, preferred_element_type=F32) for ci, qt in zip(cis, qts)]
        n8s = [jnp.where(row8 == 0, nf_s[ci], jnp.where(row8 == 1, nb_s[ci], 0.0)).astype(BF16) for ci in cis]
        qns = [jnp.dot(n8, qt, preferred_element_type=F32) for n8, qt in zip(n8s, qts)]
        qn_rows = [qns[it[0]][i % 2:i % 2 + 1, :] for i, it in enumerate(items)]
        m_prevs = [it[4][cis[it[0]]][0:1, 0:1] for it in items]
        m_inters = [b_r + m_prev for b_r, m_prev in zip(b_rs, m_prevs)]
        m_ts = [jnp.maximum(m_inter, d_max) for m_inter, d_max in zip(m_inters, d_maxs)]
        w_inters = [jnp.exp(m_inter - m_t) for m_inter, m_t in zip(m_inters, m_ts)]
        a_s = [jnp.exp(d - m_t) * qk_ts[it[0]] for d, m_t, it in zip(ds, m_ts, items)]
        dens = [w_inter * qn + jnp.sum(a, axis=0, keepdims=True) for w_inter, qn, a in zip(w_inters, qn_rows, a_s)]
        invs = [1.0 / jnp.maximum(jnp.abs(den), jnp.exp(-m_t)) for den, m_t in zip(dens, m_ts)]
        ps = [a * inv for a, inv in zip(a_s, invs)]
        ws = [w_inter * inv for w_inter, inv in zip(w_inters, invs)]

        h_ts = [jnp.dot(vt_ref[:, t0s[g]:t0s[g] + L], (ps[2 * g] + ps[2 * g + 1]).astype(BF16),
                        preferred_element_type=F32)
                + cqs[g][:dv, :] * ws[2 * g] + cqs[g][dv:, :] * ws[2 * g + 1] for g in range(G)]
        hs = [_sigmoid(o_ref[t0:t0 + L, :].astype(F32)) * h_t.T for t0, h_t in zip(t0s, h_ts)]
        hs = [h * lax.rsqrt(jnp.mean(h * h, axis=-1, keepdims=True) + NORM_EPS) * ng_ref[...] for h in hs]
        for t0, h in zip(t0s, hs):
            out_ref[t0:t0 + L, :] = (h * _silu(z_ref[t0:t0 + L, :].astype(F32))).astype(out_ref.dtype)

    pre = out_pre(0)
    init = (jnp.zeros((dv, M_HEAD_DIM), F32), jnp.zeros((1, M_HEAD_DIM), F32),
            jnp.full((1, 1), NEG, F32))
    lax.fori_loop(0, nc // U, scan_body, (init, init))
    for gi in range(nc // G):
        nxt = out_pre(gi + 1) if gi + 1 < nc // G else None
        out_post(pre)
        pre = nxt


def _mlstm(proj, qct, kc, vtm, grow, ccol, ng, batch, seq):
    t = proj.shape[0]
    nc = seq // M_CHUNK
    head_rows = lambda b, h: (h, b)
    head_cols = lambda b, h: (b, h)

    def proj_head(col0):
        return pl.BlockSpec((seq, M_HEAD_DIM), lambda b, h: (b, col0 // M_HEAD_DIM + h))

    return pl.pallas_call(
        functools.partial(_mlstm_kernel, seq=seq),
        grid=(batch, M_HEADS),
        in_specs=[
            pl.BlockSpec((M_HEAD_DIM, seq), head_rows),
            pl.BlockSpec((seq, M_HEAD_DIM), head_cols),
            pl.BlockSpec((M_HEAD_DIM, seq), head_rows),
            proj_head(P_MO), proj_head(P_MZ),
            pl.BlockSpec((N_IFG, seq), lambda b, h: (0, b)),
            pl.BlockSpec((3, seq, LANES), lambda b, h: (0, b, 0)),
            pl.BlockSpec((1, M_HEAD_DIM), lambda b, h: (0, h)),
        ],
        out_specs=pl.BlockSpec((seq, M_HEAD_DIM), head_cols),
        out_shape=jax.ShapeDtypeStruct((t, M_WIDTH), BF16),
        scratch_shapes=[
            pltpu.VMEM((nc, 2 * M_HEAD_DIM, M_HEAD_DIM), BF16),
            pltpu.VMEM((nc, SUBLANES, M_HEAD_DIM), F32),
            pltpu.VMEM((nc, SUBLANES, M_HEAD_DIM), F32),
            pltpu.VMEM((nc, SUBLANES, LANES), F32),
            pltpu.VMEM((nc, SUBLANES, LANES), F32),
        ],
        compiler_params=pltpu.CompilerParams(
            dimension_semantics=("arbitrary", "arbitrary"), vmem_limit_bytes=VMEM_LIMIT_BYTES),
        name="mlstm",
    )(qct, kc, vtm, proj, proj, grow, ccol, ng)


def _out_proj_kernel(x_ref, a_ref, m_ref, gates_ref, wa_ref, wm_ref, wo_ref, out_ref):
    branch_a = jnp.dot(a_ref[...], wa_ref[...], preferred_element_type=F32)
    branch_m = jnp.dot(m_ref[...], wm_ref[...], preferred_element_type=F32)
    gates = _sigmoid(gates_ref[...].astype(F32))
    merged = gates[:, :D_MODEL] * branch_a + gates[:, D_MODEL:] * branch_m
    out_ref[...] = x_ref[...] + jnp.dot(merged.astype(BF16), wo_ref[...], preferred_element_type=F32)


def _out_proj(x, a, m, proj, wa, wm, wo):
    t = x.shape[0]
    tile = lambda i: (i, 0)
    const = lambda i: (0, 0)
    return pl.pallas_call(
        _out_proj_kernel,
        grid=(t // OUT_TM,),
        in_specs=[
            pl.BlockSpec((OUT_TM, D_MODEL), tile),
            pl.BlockSpec((OUT_TM, ATT_WIDTH), tile),
            pl.BlockSpec((OUT_TM, M_WIDTH), tile),
            pl.BlockSpec((OUT_TM, 2 * D_MODEL), lambda i: (i, P_GATES // (2 * D_MODEL))),
            pl.BlockSpec((ATT_WIDTH, D_MODEL), const),
            pl.BlockSpec((M_WIDTH, D_MODEL), const),
            pl.BlockSpec((D_MODEL, D_MODEL), const),
        ],
        out_specs=pl.BlockSpec((OUT_TM, D_MODEL), tile),
        out_shape=jax.ShapeDtypeStruct((t, D_MODEL), F32),
        compiler_params=pltpu.CompilerParams(
            dimension_semantics=("arbitrary",), vmem_limit_bytes=VMEM_LIMIT_BYTES),
        name="out_proj",
    )(x, a, m, proj, wa, wm, wo)


def _rope_tables(seq):
    inv = jnp.power(jnp.float32(ROPE_THETA), -jnp.arange(ROPE_HALF, dtype=F32) * 2.0 / ROPE_DIM)
    ang = jnp.arange(seq, dtype=F32)[:, None] * inv[None, :]
    cos, sin = jnp.cos(ang), jnp.sin(ang)
    pad = ATT_HEAD_DIM - ROPE_DIM
    ones = jnp.ones((seq, pad), F32)
    zeros = jnp.zeros((seq, pad), F32)
    zh = jnp.zeros((seq, ROPE_HALF), F32)
    c = jnp.concatenate([cos, cos, ones], axis=-1)
    s1 = jnp.concatenate([-sin, zh, zeros], axis=-1)
    s2 = jnp.concatenate([zh, sin, zeros], axis=-1)
    tab = jnp.stack([c, s1, s2])
    return jnp.concatenate([tab, tab], axis=-1)


def _prep_weights(w_in, b_in):
    order = [(_R_GATES, 2 * D_MODEL), (_R_AZ, ATT_WIDTH), (_R_MO, M_WIDTH), (_R_MZ, M_WIDTH),
             (_R_MQ, M_WIDTH), (_R_MK, M_WIDTH), (_R_AQ, ATT_WIDTH), (_R_AK, ATT_KV_WIDTH)]
    w_main = jnp.concatenate([w_in[:, :, o:o + n] for o, n in order], axis=-1).astype(BF16)
    b_main = jnp.concatenate([b_in[:, o:o + n] for o, n in order], axis=-1)[:, None, :]
    vt_order = [(_R_MV, M_WIDTH), (_R_AV, ATT_KV_WIDTH), (_R_IF, N_IFG)]
    w_vt = jnp.swapaxes(jnp.concatenate([w_in[:, :, o:o + n] for o, n in vt_order], axis=-1), 1, 2).astype(BF16)
    b_vt = jnp.concatenate([b_in[:, o:o + n] for o, n in vt_order], axis=-1)[:, :, None]
    w_if = w_in[:, :, _R_IF:_R_IF + N_IFG]
    b_if = b_in[:, _R_IF:_R_IF + N_IFG]
    w_if_pad = jnp.pad(w_if, ((0, 0), (0, 0), (0, LANES - N_IFG))).astype(BF16)
    b_if_pad = jnp.pad(b_if, ((0, 0), (0, LANES - N_IFG)))[:, None, :]
    return w_main, b_main, w_vt, b_vt, w_if_pad, b_if_pad


def _trunk(x, rope_tab, layers):
    batch, seq, _ = x.shape
    xf = x.reshape(batch * seq, D_MODEL)
    for lw in layers:
        proj, q4, kv, vta, qct, kc, vtm, grow, ccol = _in_proj(xf, lw, rope_tab, seq)
        a = _attention(proj, q4, kv, vta, lw["sink"], batch, seq)
        m = _mlstm(proj, qct, kc, vtm, grow, ccol, lw["m_norm_g"], batch, seq)
        xf = _out_proj(xf, a, m, proj, lw["w_att_out"], lw["w_m_out"], lw["w_out"])
    return xf.reshape(batch, seq, D_MODEL)


def kernel(x_prompt, x_sample, norm_g, w_in, b_in, q_norm_g, k_norm_g, sink, conv_w, m_norm_g,
           w_att_out, w_m_out, w_out):
    w_main, b_main, w_vt, b_vt, w_if, b_if = _prep_weights(w_in, b_in)
    wa, wm, wo = w_att_out.astype(BF16), w_m_out.astype(BF16), w_out.astype(BF16)
    layers = []
    for l in range(DEPTH):
        layers.append(dict(
            norm_g=norm_g[l][None, :], w_main=w_main[l], b_main=b_main[l], w_vt=w_vt[l], b_vt=b_vt[l],
            w_if=w_if[l], b_if=b_if[l],
            gq=jnp.tile(q_norm_g[l] * (ATT_HEAD_DIM ** -0.5 * LOG2E), 2)[None, :],
            gk=jnp.tile(k_norm_g[l], 2)[None, :],
            sink=sink[l][None, :], conv_w=conv_w[l], m_norm_g=m_norm_g[l][None, :],
            w_att_out=wa[l], w_m_out=wm[l], w_out=wo[l]))
    outs = []
    for x in (x_prompt, x_sample):
        outs.append(_trunk(x, _rope_tables(x.shape[1]), layers))
    return tuple(outs)
```

```python
import functools
import math

import jax
import jax.numpy as jnp
from jax import lax
from jax.experimental import pallas as pl
from jax.experimental.pallas import tpu as pltpu

F32 = jnp.float32
BF16 = jnp.bfloat16

D_MODEL = 1024
DEPTH = 4
ATT_HEADS = 16
ATT_KV_HEADS = 4
ATT_GROUP = ATT_HEADS // ATT_KV_HEADS
ATT_HEAD_DIM = 64
ATT_WIDTH = ATT_HEADS * ATT_HEAD_DIM
ATT_KV_WIDTH = ATT_KV_HEADS * ATT_HEAD_DIM
WINDOW = 128
ATT_BLOCK = 128
ROPE_THETA = 500000.0
ROPE_DIM = ATT_HEAD_DIM // 4
ROPE_HALF = ROPE_DIM // 2
M_HEADS = 8
M_HEAD_DIM = 128
M_WIDTH = M_HEADS * M_HEAD_DIM
M_CHUNK = 128
CONV_K = 3
NORM_EPS = 1e-6
NEG = -1e30
LOG2E = math.log2(math.e)

LANES = 128
SUBLANES = 8
VMEM_LIMIT_BYTES = 56 * 1024 * 1024
_REF_SPLITS = (ATT_WIDTH, ATT_KV_WIDTH, ATT_KV_WIDTH, ATT_WIDTH,
               M_WIDTH, M_WIDTH, M_WIDTH, M_WIDTH, M_WIDTH,
               M_HEADS, M_HEADS, M_HEADS, M_HEADS, 2 * D_MODEL)
_REF_OFF = [0]
for _w in _REF_SPLITS:
    _REF_OFF.append(_REF_OFF[-1] + _w)
(_R_AQ, _R_AK, _R_AV, _R_AZ, _R_MQ, _R_MK, _R_MV, _R_MO, _R_MZ,
 _R_IF, _R_FF, _R_IB, _R_FB, _R_GATES) = _REF_OFF[:-1]

P_GATES = 0
P_AZ = P_GATES + 2 * D_MODEL
P_MO = P_AZ + ATT_WIDTH
P_MZ = P_MO + M_WIDTH
P_WIDTH = P_MZ + M_WIDTH
W_MQ = P_WIDTH
W_MK = W_MQ + M_WIDTH
W_AQ = W_MK + M_WIDTH
W_AK = W_AQ + ATT_WIDTH
W_WIDTH = W_AK + ATT_KV_WIDTH
VT_ROWS = M_WIDTH + ATT_KV_WIDTH
N_IFG = 4 * M_HEADS

ROW_LI_F, ROW_B_F, ROW_LI_B, ROW_B_B = 0, M_HEADS, 2 * M_HEADS, 3 * M_HEADS
LANE_C_F, LANE_C_B = 0, 2 * M_HEADS

PROJ_DTYPE = BF16
IN_TM = 512
IN_TN = 512
IN_EPILOGUES_PER_MATMUL = 2
IN_HALO = 16
IN_PIECE = 256
VT_TN = 256
OUT_TM = 1024
ATT_TQ = 512
ATT_NSB = ATT_TQ // ATT_BLOCK
ATT_SUB_GROUP = 4
ATT_PAIRS = ATT_WIDTH // LANES
ATT_KVAR_WIDTH = ATT_KV_HEADS * 2 * LANES
MLSTM_SCAN_GROUP = 32
MLSTM_OUT_GROUP = 16

_NT = (((1,), (1,)), ((), ()))


def _sigmoid(x):
    return 1.0 / (1.0 + jnp.exp(-x))


def _silu(x):
    return x * _sigmoid(x)


def _log_sigmoid(x):
    return jnp.minimum(x, 0.0) - jnp.log1p(jnp.exp(-jnp.abs(x)))


def _bf16_terms(x):
    hi = x.astype(BF16)
    r1 = x - hi.astype(F32)
    mid = r1.astype(BF16)
    lo = (r1 - mid.astype(F32)).astype(BF16)
    return hi, mid, lo


def _split_dot_left(sel, x):
    return sum(jnp.dot(sel, t, preferred_element_type=F32) for t in _bf16_terms(x))


def _split_dot_right(x, sel):
    return sum(jnp.dot(t, sel, preferred_element_type=F32) for t in _bf16_terms(x))


def _rope(y, tab):
    return (y * tab[0]
            + pltpu.roll(y, LANES - ROPE_HALF, 1) * tab[1]
            + pltpu.roll(y, ROPE_HALF, 1) * tab[2])


def _norm_rope_tile(x, gain, tab):
    lane = lax.broadcasted_iota(jnp.int32, x.shape, 1)
    left = lane < ATT_HEAD_DIM
    sq = x * x
    ss_l = jnp.sum(jnp.where(left, sq, 0.0), axis=-1, keepdims=True)
    ss_r = jnp.sum(jnp.where(left, 0.0, sq), axis=-1, keepdims=True)
    inv = jnp.where(left, lax.rsqrt(ss_l / ATT_HEAD_DIM + NORM_EPS),
                    lax.rsqrt(ss_r / ATT_HEAD_DIM + NORM_EPS))
    return _rope(x * inv * gain, tab)


def _rmsnorm_bf16(x, g):
    return (x * lax.rsqrt(jnp.mean(x * x, axis=-1, keepdims=True) + NORM_EPS) * g).astype(BF16)


def _in_proj_kernel(x_ref, xprev_ref, xnext_ref, g_ref, w_ref, b_ref, wvt_ref, bvt_ref,
                    wif_ref, bif_ref, cw_ref, tab_ref, gq_ref, gk_ref,
                    proj_ref, q4_ref, kv_ref, vta_ref, qct_ref, kc_ref, vtm_ref, grow_ref, ccol_ref,
                    xn_s, conv_s, *, tiles_per_seq):
    tm = x_ref.shape[0]
    tile_in_seq = pl.program_id(0) % tiles_per_seq

    xn_s[0:IN_HALO, :] = _rmsnorm_bf16(xprev_ref[...], g_ref[...])
    xn_s[IN_HALO:IN_HALO + tm, :] = _rmsnorm_bf16(x_ref[...], g_ref[...])
    xn_s[IN_HALO + tm:, :] = _rmsnorm_bf16(xnext_ref[...], g_ref[...])
    xn = xn_s[IN_HALO:IN_HALO + tm, :]
    xn_halo = xn_s[...]

    def chunk(c0, width, lhs=None):
        lhs = xn if lhs is None else lhs

        def piece(p0):
            cols = slice(p0, p0 + IN_PIECE)
            return lambda: jnp.dot(lhs, w_ref[:, cols], preferred_element_type=F32) + b_ref[:, cols]
        return [piece(p0) for p0 in range(c0, c0 + width, IN_PIECE)]

    stages = []

    def store_plain(c0):
        def store(acc, p0):
            def run():
                proj_ref[:, p0:p0 + IN_PIECE] = acc.astype(proj_ref.dtype)
            return run
        return lambda accs: [store(acc, c0 + i * IN_PIECE) for i, acc in enumerate(accs)]

    plain = [(chunk(c0, IN_TN), store_plain(c0)) for c0 in range(0, P_WIDTH, IN_TN)]

    has_before = tile_in_seq > 0
    has_after = tile_in_seq < tiles_per_seq - 1
    row_id = lax.broadcasted_iota(jnp.int32, (tm, LANES), 0)
    kscale = M_HEAD_DIM ** -0.5
    lane_tiles = range(0, IN_PIECE, LANES)

    def conv_silu(c0):
        def conv(acc, p0, l0):
            cols = slice(p0 + l0, p0 + l0 + LANES)

            def run():
                buf = conv_s.at[((p0 + l0) // LANES) % 2]
                buf[...] = acc[:, l0:l0 + LANES]
                lo, hi = IN_HALO - 1, IN_HALO + tm
                buf[lo:lo + 1, :] = jnp.where(has_before, acc[lo:lo + 1, l0:l0 + LANES], 0.0)
                buf[hi:hi + 1, :] = jnp.where(has_after, acc[hi:hi + 1, l0:l0 + LANES], 0.0)
                a = acc[IN_HALO:IN_HALO + tm, l0:l0 + LANES]
                xm = buf[IN_HALO - 1:IN_HALO - 1 + tm, :]
                xp = buf[IN_HALO + 1:IN_HALO + 1 + tm, :]
                y = _silu(xm * cw_ref[0:1, cols] + a * cw_ref[1:2, cols] + xp * cw_ref[2:3, cols])
                if p0 < M_WIDTH:
                    qct_ref[cols, :] = y.T.astype(BF16)
                else:
                    kc_ref[:, p0 + l0 - M_WIDTH:p0 + l0 - M_WIDTH + LANES] = (y * kscale).astype(BF16)
            return run
        return lambda accs: [conv(acc, c0 + i * IN_PIECE, l0) for i, acc in enumerate(accs) for l0 in lane_tiles]

    conv = [(chunk(W_MQ + c0, IN_TN, lhs=xn_halo), conv_silu(c0)) for c0 in range(0, 2 * M_WIDTH, IN_TN)]

    tab = tab_ref[...]

    def q_rope(c0):
        def rope(acc, p0, l0):
            def run():
                y = _norm_rope_tile(acc[:, l0:l0 + LANES], gq_ref[...], tab).astype(BF16)
                for j in range(tm // ATT_BLOCK):
                    q4_ref[j, (p0 + l0) // LANES] = y[j * ATT_BLOCK:(j + 1) * ATT_BLOCK, :]
            return run
        return lambda accs: [rope(acc, c0 + i * IN_PIECE, l0) for i, acc in enumerate(accs) for l0 in lane_tiles]

    att_q = [(chunk(W_AQ + c0, IN_TN), q_rope(c0)) for c0 in range(0, ATT_WIDTH, IN_TN)]

    left_half = lax.broadcasted_iota(jnp.int32, (tm, LANES), 1) < ATT_HEAD_DIM

    def k_norm_rope(accs):
        def norm_rope(acc, p0):
            def run():
                for t in range(IN_PIECE // LANES):
                    kt = _norm_rope_tile(acc[:, t * LANES:(t + 1) * LANES], gk_ref[...], tab)
                    sw = pltpu.roll(kt, ATT_HEAD_DIM, 1)
                    head = (p0 // LANES + t) * 2
                    variants = ((head, 0, jnp.where(left_half, kt, 0.0)), (head, 1, jnp.where(left_half, 0.0, sw)),
                                (head + 1, 0, jnp.where(left_half, sw, 0.0)), (head + 1, 1, jnp.where(left_half, 0.0, kt)))
                    for h, v, val in variants:
                        col = (2 * h + v) * LANES
                        kv_ref[:, col:col + LANES] = val.astype(BF16)
            return run
        return [norm_rope(acc, i * IN_PIECE) for i, acc in enumerate(accs)]

    att_k = (chunk(W_AK, ATT_KV_WIDTH), k_norm_rope)

    def vt_matmul():
        return lax.dot_general(wvt_ref[...], xn, _NT, preferred_element_type=F32) + bvt_ref[...]

    def store_vt(vt, r0):
        def run():
            if r0 < M_WIDTH:
                vtm_ref[r0:r0 + VT_TN, :] = vt[r0:r0 + VT_TN, :].astype(BF16)
            else:
                vta_ref[r0 - M_WIDTH:r0 - M_WIDTH + VT_TN, :] = vt[r0:r0 + VT_TN, :].astype(BF16)
        return run

    def gate_matmul():
        return jnp.dot(xn, wif_ref[...], preferred_element_type=F32) + bif_ref[...]

    def vt_gate_epilogues(results):
        vt, gc = results
        gr = vt[VT_ROWS:VT_ROWS + N_IFG, :]
        return [store_vt(vt, r0) for r0 in range(0, VT_ROWS, VT_TN)] + [lambda: gate_epilogue(gc, gr)]

    def gate_epilogue(gc, gr):
        ls_c = _log_sigmoid(gc)
        ls_r = _log_sigmoid(gr)
        r = lax.broadcasted_iota(jnp.int32, (M_CHUNK, M_CHUNK), 0)
        c = lax.broadcasted_iota(jnp.int32, (M_CHUNK, M_CHUNK), 1)
        tril = jnp.where(c <= r, 1.0, 0.0).astype(BF16)
        triu = jnp.where(c >= r, 1.0, 0.0).astype(BF16)
        lane = lax.broadcasted_iota(jnp.int32, (M_CHUNK, LANES), 1)
        fwd_lanes = lane < 2 * M_HEADS
        grow_ref[ROW_LI_F:ROW_LI_F + M_HEADS, :] = gr[0:M_HEADS, :]
        grow_ref[ROW_LI_B:ROW_LI_B + M_HEADS, :] = gr[2 * M_HEADS:3 * M_HEADS, :]
        for ci in range(tm // M_CHUNK):
            rows = slice(ci * M_CHUNK, (ci + 1) * M_CHUNK)
            lsc = ls_c[rows, :]
            b_f = _split_dot_left(tril, lsc)
            b_b = _split_dot_left(triu, lsc)
            cc = gc[rows, :] - pltpu.roll(jnp.where(fwd_lanes, b_f, b_b), LANES - M_HEADS, 1)
            for ti, term in enumerate(_bf16_terms(cc)):
                ccol_ref[ti, rows, :] = term
            grow_ref[ROW_B_F:ROW_B_F + M_HEADS, rows] = _split_dot_right(ls_r[M_HEADS:2 * M_HEADS, rows], triu)
            grow_ref[ROW_B_B:ROW_B_B + M_HEADS, rows] = _split_dot_right(ls_r[3 * M_HEADS:4 * M_HEADS, rows], tril)

    light = plain
    heavy = [([vt_matmul, gate_matmul], vt_gate_epilogues),
             conv[0], att_q[0], conv[1], att_q[1], conv[2], att_k, conv[3]]
    stages.append(heavy[0])
    for i, stage in enumerate(heavy[1:]):
        stages.extend([stage, light[i]])
    stages.extend(light[len(heavy) - 1:])

    n_matmuls = sum(len(matmuls) for matmuls, _ in stages)
    queue = []
    emitted = 0
    for matmuls, make_epilogues in stages:
        results = []
        for matmul in matmuls:
            results.append(matmul())
            emitted += 1
            take = max(IN_EPILOGUES_PER_MATMUL, -(-len(queue) // max(n_matmuls - emitted, 1))) if queue else 0
            for run in queue[:take]:
                run()
            queue = queue[take:]
        queue.extend(make_epilogues(results))
    for run in queue:
        run()


def _in_proj(x, lw, rope_tab, seq):
    t = x.shape[0]
    tps = seq // IN_TM
    halo_blocks = IN_TM // IN_HALO
    const = lambda i: (0, 0)
    tile = lambda i: (i, 0)
    cols = lambda i: (0, i)
    resident = dict(pipeline_mode=pl.Buffered(1))
    return pl.pallas_call(
        functools.partial(_in_proj_kernel, tiles_per_seq=tps),
        grid=(t // IN_TM,),
        in_specs=[
            pl.BlockSpec((IN_TM, D_MODEL), tile),
            pl.BlockSpec((IN_HALO, D_MODEL), lambda i: (jnp.maximum(i * halo_blocks - 1, 0), 0)),
            pl.BlockSpec((IN_HALO, D_MODEL), lambda i: (jnp.minimum((i + 1) * halo_blocks, t // IN_HALO - 1), 0)),
            pl.BlockSpec((1, D_MODEL), const),
            pl.BlockSpec((D_MODEL, W_WIDTH), const, **resident),
            pl.BlockSpec((1, W_WIDTH), const),
            pl.BlockSpec((VT_ROWS + N_IFG, D_MODEL), const, **resident),
            pl.BlockSpec((VT_ROWS + N_IFG, 1), const),
            pl.BlockSpec((D_MODEL, LANES), const),
            pl.BlockSpec((1, LANES), const),
            pl.BlockSpec((CONV_K, 2 * M_WIDTH), const),
            pl.BlockSpec((3, IN_TM, LANES), lambda i: (0, i % tps, 0)),
            pl.BlockSpec((1, LANES), const),
            pl.BlockSpec((1, LANES), const),
        ],
        out_specs=[
            pl.BlockSpec((IN_TM, P_WIDTH), tile),
            pl.BlockSpec((IN_TM // ATT_BLOCK, ATT_PAIRS, ATT_BLOCK, LANES), lambda i: (i, 0, 0, 0)),
            pl.BlockSpec((IN_TM, ATT_KVAR_WIDTH), tile),
            pl.BlockSpec((ATT_KV_WIDTH, IN_TM), cols),
            pl.BlockSpec((M_WIDTH, IN_TM), cols),
            pl.BlockSpec((IN_TM, M_WIDTH), tile),
            pl.BlockSpec((M_WIDTH, IN_TM), cols),
            pl.BlockSpec((N_IFG, IN_TM), cols),
            pl.BlockSpec((3, IN_TM, LANES), lambda i: (0, i, 0)),
        ],
        out_shape=[
            jax.ShapeDtypeStruct((t, P_WIDTH), PROJ_DTYPE),
            jax.ShapeDtypeStruct((t // ATT_BLOCK, ATT_PAIRS, ATT_BLOCK, LANES), BF16),
            jax.ShapeDtypeStruct((t, ATT_KVAR_WIDTH), BF16),
            jax.ShapeDtypeStruct((ATT_KV_WIDTH, t), BF16),
            jax.ShapeDtypeStruct((M_WIDTH, t), BF16),
            jax.ShapeDtypeStruct((t, M_WIDTH), BF16),
            jax.ShapeDtypeStruct((M_WIDTH, t), BF16),
            jax.ShapeDtypeStruct((N_IFG, t), F32),
            jax.ShapeDtypeStruct((3, t, LANES), BF16),
        ],
        scratch_shapes=[pltpu.VMEM((IN_TM + 2 * IN_HALO, D_MODEL), BF16),
                        pltpu.VMEM((2, IN_TM + 2 * IN_HALO, LANES), F32)],
        compiler_params=pltpu.CompilerParams(
            dimension_semantics=("arbitrary",), vmem_limit_bytes=VMEM_LIMIT_BYTES),
        name="in_proj",
    )(x, x, x, lw["norm_g"], lw["w_main"], lw["b_main"], lw["w_vt"], lw["b_vt"], lw["w_if"],
      lw["b_if"], lw["conv_w"], rope_tab, lw["gq"], lw["gk"])


def _attention_kernel(q4_ref, z_ref, kp_ref, kc_ref, kn_ref, vp_ref, vc_ref, vn_ref, sink_ref, out_ref,
                      ks, vts, out_t, *, n_blocks):
    ti = pl.program_id(1)
    bq = ATT_BLOCK

    r0 = 0
    for k_ref, v_ref in ((kp_ref, vp_ref), (kc_ref, vc_ref), (kn_ref, vn_ref)):
        nr = k_ref.shape[0]
        ks[r0:r0 + nr, :] = k_ref[...]
        vts[:, r0:r0 + nr] = v_ref[...]
        r0 += nr

    def group_heads(h):
        return (ATT_GROUP * h, ATT_GROUP * h + 2, ATT_GROUP * h + 1, ATT_GROUP * h + 3)

    sink_rows = [jnp.concatenate([jnp.broadcast_to(sink_ref[:, c:c + 1] * LOG2E, (1, bq)) for c in group_heads(h)],
                                 axis=1) for h in range(ATT_KV_HEADS)]

    kk = lax.broadcasted_iota(jnp.int32, (bq, bq), 0)
    qq = lax.broadcasted_iota(jnp.int32, (bq, bq), 1)
    prev_in_band = kk >= qq
    next_in_band = kk <= qq

    JG = ATT_SUB_GROUP

    def sub_blocks(jg, carry):
        js = [jg * JG + u for u in range(JG)]
        row0s = [pl.multiple_of(j * bq, bq) for j in js]
        biases = []
        for j in js:
            blk = ti * ATT_NSB + j
            bias_prev = jnp.where(prev_in_band, jnp.where(blk > 0, 0.0, NEG), NEG).astype(F32)
            bias_next = jnp.where(next_in_band, jnp.where(blk < n_blocks - 1, 0.0, NEG), NEG).astype(F32)
            biases.append((jnp.concatenate([bias_prev] * ATT_GROUP, axis=1),
                           jnp.concatenate([bias_next] * ATT_GROUP, axis=1)))
        q_slabs = [q4_ref[j] for j in js]
        pairs = [(u, h) for u in range(JG) for h in range(ATT_KV_HEADS)]
        q_pairs = [q_slabs[u][2 * h:2 * h + 2].reshape(2 * bq, LANES) for u, h in pairs]
        ss = [jnp.concatenate(
            [lax.dot_general(ks[pl.ds(row0s[u], 3 * bq), (2 * h + v) * LANES:(2 * h + v + 1) * LANES], q_pair, _NT,
                             preferred_element_type=F32)
             for v in range(2)], axis=1) for (u, h), q_pair in zip(pairs, q_pairs)]
        ss = [jnp.concatenate([s[:bq] + biases[u][0], s[bq:2 * bq], s[2 * bq:] + biases[u][1]], axis=0)
              for (u, h), s in zip(pairs, ss)]
        ms = [jnp.maximum(jnp.max(s, axis=0, keepdims=True), sink_rows[h]) for (u, h), s in zip(pairs, ss)]
        ps = [jnp.exp2(s - m) for s, m in zip(ss, ms)]
        denoms = [jnp.sum(p, axis=0, keepdims=True) + jnp.exp2(sink_rows[h] - m)
                  for (u, h), p, m in zip(pairs, ps, ms)]
        os_ = [jnp.dot(vts[h * ATT_HEAD_DIM:(h + 1) * ATT_HEAD_DIM, pl.ds(row0s[u], 3 * bq)], p.astype(BF16),
                       preferred_element_type=F32) * (1.0 / denom)
               for (u, h), p, denom in zip(pairs, ps, denoms)]
        for (u, h), o in zip(pairs, os_):
            for i, c in enumerate(group_heads(h)):
                out_t[u, c * ATT_HEAD_DIM:(c + 1) * ATT_HEAD_DIM, :] = o[:, i * bq:(i + 1) * bq]
        for u in range(JG):
            att = out_t[u].T
            z = z_ref[pl.ds(row0s[u], bq), :].astype(F32)
            out_ref[pl.ds(row0s[u], bq), :] = (att * _silu(z)).astype(out_ref.dtype)
        return carry

    lax.fori_loop(0, ATT_NSB // JG, sub_blocks, 0)


def _attention(proj, q4, kv, vta, sink, batch, seq):
    t = proj.shape[0]
    nb = seq // ATT_BLOCK
    nt = seq // ATT_TQ
    bq = ATT_BLOCK

    prev_blk = lambda b, i: b * nb + jnp.maximum(i * ATT_NSB - 1, 0)
    next_blk = lambda b, i: b * nb + jnp.minimum((i + 1) * ATT_NSB, nb - 1)
    return pl.pallas_call(
        functools.partial(_attention_kernel, n_blocks=nb),
        grid=(batch, nt),
        in_specs=[
            pl.BlockSpec((ATT_NSB, ATT_PAIRS, bq, LANES), lambda b, i: (b * nt + i, 0, 0, 0)),
            pl.BlockSpec((ATT_TQ, ATT_WIDTH), lambda b, i: (b * nt + i, P_AZ // ATT_WIDTH)),
            pl.BlockSpec((bq, ATT_KVAR_WIDTH), lambda b, i: (prev_blk(b, i), 0)),
            pl.BlockSpec((ATT_TQ, ATT_KVAR_WIDTH), lambda b, i: (b * nt + i, 0)),
            pl.BlockSpec((bq, ATT_KVAR_WIDTH), lambda b, i: (next_blk(b, i), 0)),
            pl.BlockSpec((ATT_KV_WIDTH, bq), lambda b, i: (0, prev_blk(b, i))),
            pl.BlockSpec((ATT_KV_WIDTH, ATT_TQ), lambda b, i: (0, b * nt + i)),
            pl.BlockSpec((ATT_KV_WIDTH, bq), lambda b, i: (0, next_blk(b, i))),
            pl.BlockSpec((1, ATT_HEADS), lambda b, i: (0, 0)),
        ],
        out_specs=pl.BlockSpec((ATT_TQ, ATT_WIDTH), lambda b, i: (b * nt + i, 0)),
        out_shape=jax.ShapeDtypeStruct((t, ATT_WIDTH), BF16),
        scratch_shapes=[
            pltpu.VMEM((ATT_TQ + 2 * bq, ATT_KVAR_WIDTH), BF16),
            pltpu.VMEM((ATT_KV_WIDTH, ATT_TQ + 2 * bq), BF16),
            pltpu.VMEM((ATT_SUB_GROUP, ATT_WIDTH, bq), F32),
        ],
        compiler_params=pltpu.CompilerParams(
            dimension_semantics=("arbitrary", "arbitrary"), vmem_limit_bytes=VMEM_LIMIT_BYTES),
        name="attention",
    )(q4, proj, kv, kv, kv, vta, vta, vta, sink)


def _mlstm_kernel(qct_ref, kc_ref, vt_ref, o_ref, z_ref, grow_ref, ccol_ref, ng_ref,
                  out_ref, cct_s, nf_s, nb_s, mf_s, mb_s, *, seq):
    hd = pl.program_id(1)
    L = M_CHUNK
    dv = M_HEAD_DIM
    nc = seq // L

    head_row = lax.broadcasted_iota(jnp.int32, (M_HEADS, L), 0) == hd

    def gate_row(base, t0):
        rows = grow_ref[base:base + M_HEADS, pl.ds(t0, L)]
        return jnp.sum(jnp.where(head_row, rows, 0.0), axis=0, keepdims=True)

    U = min(MLSTM_SCAN_GROUP, nc)
    dirs = ((ROW_LI_F, ROW_B_F, L - 1, nf_s, mf_s, 0), (ROW_LI_B, ROW_B_B, 0, nb_s, mb_s, dv))

    def scan_body(gj, carry):
        cis = [[gj * U + u for u in range(U)], [nc - 1 - (gj * U + u) for u in range(U)]]
        t0s = [[pl.multiple_of(ci * L, L) for ci in row] for row in cis]
        lis = [[gate_row(dirs[d][0], t0) for t0 in t0s[d]] for d in range(2)]
        bs = [[gate_row(dirs[d][1], t0) for t0 in t0s[d]] for d in range(2)]
        b_lasts = [[b[:, dirs[d][2]:dirs[d][2] + 1] for b in bs[d]] for d in range(2)]
        gs = [[b_last - b + li for b_last, b, li in zip(b_lasts[d], bs[d], lis[d])] for d in range(2)]
        g_maxs = [[jnp.max(g, axis=-1, keepdims=True) for g in gs[d]] for d in range(2)]
        ks = [[kc_ref[pl.ds(t0, L), :] for t0 in t0s[d]] for d in range(2)]
        vts = [[vt_ref[:, pl.ds(t0, L)].astype(F32) for t0 in t0s[d]] for d in range(2)]

        ms = [[carry[d][2]] for d in range(2)]
        for d in range(2):
            for u in range(U):
                ms[d].append(jnp.maximum(b_lasts[d][u] + ms[d][u], g_maxs[d][u]))
        w_cs = [[jnp.exp(b_lasts[d][u] + ms[d][u] - ms[d][u + 1]) for u in range(U)] for d in range(2)]
        w_ks = [[jnp.exp(gs[d][u] - ms[d][u + 1]) for u in range(U)] for d in range(2)]
        c_upds = [[jnp.dot((vts[d][u] * w_ks[d][u]).astype(BF16), ks[d][u], preferred_element_type=F32)
                   for u in range(U)] for d in range(2)]
        n_upds = [[jnp.dot(jnp.broadcast_to(w_ks[d][u], (SUBLANES, L)).astype(BF16), ks[d][u],
                           preferred_element_type=F32)[0:1, :] for u in range(U)] for d in range(2)]

        out = []
        for d in range(2):
            _, _, _, n_s, m_s, row0 = dirs[d]
            ct_st, n_st, _ = carry[d]
            for u in range(U):
                ci = cis[d][u]
                cct_s[ci, row0:row0 + dv, :] = ct_st.astype(BF16)
                n_s[ci] = jnp.broadcast_to(n_st, (SUBLANES, LANES))
                m_s[ci] = jnp.broadcast_to(ms[d][u], (SUBLANES, LANES))
                ct_st = w_cs[d][u] * ct_st + c_upds[d][u]
                n_st = w_cs[d][u] * n_st + n_upds[d][u]
            out.append((ct_st, n_st, ms[d][U]))
        return tuple(out)

    rr = lax.broadcasted_iota(jnp.int32, (L, L), 0)
    cc = lax.broadcasted_iota(jnp.int32, (L, L), 1)
    row8 = lax.broadcasted_iota(jnp.int32, (SUBLANES, LANES), 0)
    sel_k = lax.broadcasted_iota(jnp.int32, (LANES, 2 * L), 0)
    sel_n = lax.broadcasted_iota(jnp.int32, (LANES, 2 * L), 1)
    sel_lane = jnp.where(sel_n < L, LANE_C_F + hd, LANE_C_B + hd)
    sel = jnp.where(sel_k == sel_lane, 1.0, 0.0).astype(BF16)
    keep_f = rr <= cc
    keep_b = rr >= cc
    G = min(MLSTM_OUT_GROUP, nc)

    def out_pre(gi):
        cis = [gi * G + g for g in range(G)]
        t0s = [ci * L for ci in cis]
        qts = [qct_ref[:, t0:t0 + L] for t0 in t0s]
        qk_ts = [jnp.dot(kc_ref[t0:t0 + L, :], qt, preferred_element_type=F32)
                 for t0, qt in zip(t0s, qts)]
        c_bcs = [sum(jnp.dot(ccol_ref[ti, t0:t0 + L, :], sel, preferred_element_type=F32) for ti in range(3))
                 for t0 in t0s]
        items = []
        for g in range(G):
            items.append((g, c_bcs[g][:, :L], ROW_B_F, keep_f, mf_s))
            items.append((g, c_bcs[g][:, L:], ROW_B_B, keep_b, mb_s))
        b_rs = [gate_row(b_base, t0s[g]) for g, _, b_base, _, _ in items]
        ds = [jnp.where(keep, b_r + c_s, NEG)
              for (_, c_s, _, keep, _), b_r in zip(items, b_rs)]
        d_maxs = [jnp.max(d, axis=0, keepdims=True) for d in ds]
        return cis, t0s, qts, qk_ts, items, b_rs, ds, d_maxs

    def out_post(pre):
        cis, t0s, qts, qk_ts, items, b_rs, ds, d_maxs = pre
        cqs = [jnp.dot(cct_s[ci], qt, preferred_element_type=F32) for ci, qt in zip(cis, qts)]
        n8s = [jnp.where(row8 == 0, nf_s[ci], jnp.where(row8 == 1, nb_s[ci], 0.0)).astype(BF16) for ci in cis]
        qns = [jnp.dot(n8, qt, preferred_element_type=F32) for n8, qt in zip(n8s, qts)]
        qn_rows = [qns[it[0]][i % 2:i % 2 + 1, :] for i, it in enumerate(items)]
        m_prevs = [it[4][cis[it[0]]][0:1, 0:1] for it in items]
        m_inters = [b_r + m_prev for b_r, m_prev in zip(b_rs, m_prevs)]
        m_ts = [jnp.maximum(m_inter, d_max) for m_inter, d_max in zip(m_inters, d_maxs)]
        w_inters = [jnp.exp(m_inter - m_t) for m_inter, m_t in zip(m_inters, m_ts)]
        a_s = [jnp.exp(d - m_t) * qk_ts[it[0]] for d, m_t, it in zip(ds, m_ts, items)]
        dens = [w_inter * qn + jnp.sum(a, axis=0, keepdims=True) for w_inter, qn, a in zip(w_inters, qn_rows, a_s)]
        invs = [1.0 / jnp.maximum(jnp.abs(den), jnp.exp(-m_t)) for den, m_t in zip(dens, m_ts)]
        ps = [a * inv for a, inv in zip(a_s, invs)]
        ws = [w_inter * inv for w_inter, inv in zip(w_inters, invs)]

        h_ts = [jnp.dot(vt_ref[:, t0s[g]:t0s[g] + L], (ps[2 * g] + ps[2 * g + 1]).astype(BF16),
                        preferred_element_type=F32)
                + cqs[g][:dv, :] * ws[2 * g] + cqs[g][dv:, :] * ws[2 * g + 1] for g in range(G)]
        hs = [_sigmoid(o_ref[t0:t0 + L, :].astype(F32)) * h_t.T for t0, h_t in zip(t0s, h_ts)]
        hs = [h * lax.rsqrt(jnp.mean(h * h, axis=-1, keepdims=True) + NORM_EPS) * ng_ref[...] for h in hs]
        for t0, h in zip(t0s, hs):
            out_ref[t0:t0 + L, :] = (h * _silu(z_ref[t0:t0 + L, :].astype(F32))).astype(out_ref.dtype)

    pre = out_pre(0)
    init = (jnp.zeros((dv, M_HEAD_DIM), F32), jnp.zeros((1, M_HEAD_DIM), F32),
            jnp.full((1, 1), NEG, F32))
    lax.fori_loop(0, nc // U, scan_body, (init, init))
    for gi in range(nc // G):
        nxt = out_pre(gi + 1) if gi + 1 < nc // G else None
        out_post(pre)
        pre = nxt


def _mlstm(proj, qct, kc, vtm, grow, ccol, ng, batch, seq):
    t = proj.shape[0]
    nc = seq // M_CHUNK
    head_rows = lambda b, h: (h, b)
    head_cols = lambda b, h: (b, h)

    def proj_head(col0):
        return pl.BlockSpec((seq, M_HEAD_DIM), lambda b, h: (b, col0 // M_HEAD_DIM + h))

    return pl.pallas_call(
        functools.partial(_mlstm_kernel, seq=seq),
        grid=(batch, M_HEADS),
        in_specs=[
            pl.BlockSpec((M_HEAD_DIM, seq), head_rows),
            pl.BlockSpec((seq, M_HEAD_DIM), head_cols),
            pl.BlockSpec((M_HEAD_DIM, seq), head_rows),
            proj_head(P_MO), proj_head(P_MZ),
            pl.BlockSpec((N_IFG, seq), lambda b, h: (0, b)),
            pl.BlockSpec((3, seq, LANES), lambda b, h: (0, b, 0)),
            pl.BlockSpec((1, M_HEAD_DIM), lambda b, h: (0, h)),
        ],
        out_specs=pl.BlockSpec((seq, M_HEAD_DIM), head_cols),
        out_shape=jax.ShapeDtypeStruct((t, M_WIDTH), BF16),
        scratch_shapes=[
            pltpu.VMEM((nc, 2 * M_HEAD_DIM, M_HEAD_DIM), BF16),
            pltpu.VMEM((nc, SUBLANES, M_HEAD_DIM), F32),
            pltpu.VMEM((nc, SUBLANES, M_HEAD_DIM), F32),
            pltpu.VMEM((nc, SUBLANES, LANES), F32),
            pltpu.VMEM((nc, SUBLANES, LANES), F32),
        ],
        compiler_params=pltpu.CompilerParams(
            dimension_semantics=("arbitrary", "arbitrary"), vmem_limit_bytes=VMEM_LIMIT_BYTES),
        name="mlstm",
    )(qct, kc, vtm, proj, proj, grow, ccol, ng)


def _out_proj_kernel(x_ref, a_ref, m_ref, gates_ref, wa_ref, wm_ref, wo_ref, out_ref):
    branch_a = jnp.dot(a_ref[...], wa_ref[...], preferred_element_type=F32)
    branch_m = jnp.dot(m_ref[...], wm_ref[...], preferred_element_type=F32)
    gates = _sigmoid(gates_ref[...].astype(F32))
    merged = gates[:, :D_MODEL] * branch_a + gates[:, D_MODEL:] * branch_m
    out_ref[...] = x_ref[...] + jnp.dot(merged.astype(BF16), wo_ref[...], preferred_element_type=F32)


def _out_proj(x, a, m, proj, wa, wm, wo):
    t = x.shape[0]
    tile = lambda i: (i, 0)
    const = lambda i: (0, 0)
    return pl.pallas_call(
        _out_proj_kernel,
        grid=(t // OUT_TM,),
        in_specs=[
            pl.BlockSpec((OUT_TM, D_MODEL), tile),
            pl.BlockSpec((OUT_TM, ATT_WIDTH), tile),
            pl.BlockSpec((OUT_TM, M_WIDTH), tile),
            pl.BlockSpec((OUT_TM, 2 * D_MODEL), lambda i: (i, P_GATES // (2 * D_MODEL))),
            pl.BlockSpec((ATT_WIDTH, D_MODEL), const),
            pl.BlockSpec((M_WIDTH, D_MODEL), const),
            pl.BlockSpec((D_MODEL, D_MODEL), const),
        ],
        out_specs=pl.BlockSpec((OUT_TM, D_MODEL), tile),
        out_shape=jax.ShapeDtypeStruct((t, D_MODEL), F32),
        compiler_params=pltpu.CompilerParams(
            dimension_semantics=("arbitrary",), vmem_limit_bytes=VMEM_LIMIT_BYTES),
        name="out_proj",
    )(x, a, m, proj, wa, wm, wo)


def _rope_tables(seq):
    inv = jnp.power(jnp.float32(ROPE_THETA), -jnp.arange(ROPE_HALF, dtype=F32) * 2.0 / ROPE_DIM)
    ang = jnp.arange(seq, dtype=F32)[:, None] * inv[None, :]
    cos, sin = jnp.cos(ang), jnp.sin(ang)
    pad = ATT_HEAD_DIM - ROPE_DIM
    ones = jnp.ones((seq, pad), F32)
    zeros = jnp.zeros((seq, pad), F32)
    zh = jnp.zeros((seq, ROPE_HALF), F32)
    c = jnp.concatenate([cos, cos, ones], axis=-1)
    s1 = jnp.concatenate([-sin, zh, zeros], axis=-1)
    s2 = jnp.concatenate([zh, sin, zeros], axis=-1)
    tab = jnp.stack([c, s1, s2])
    return jnp.concatenate([tab, tab], axis=-1)


def _prep_weights(w_in, b_in):
    order = [(_R_GATES, 2 * D_MODEL), (_R_AZ, ATT_WIDTH), (_R_MO, M_WIDTH), (_R_MZ, M_WIDTH),
             (_R_MQ, M_WIDTH), (_R_MK, M_WIDTH), (_R_AQ, ATT_WIDTH), (_R_AK, ATT_KV_WIDTH)]
    w_main = jnp.concatenate([w_in[:, :, o:o + n] for o, n in order], axis=-1).astype(BF16)
    b_main = jnp.concatenate([b_in[:, o:o + n] for o, n in order], axis=-1)[:, None, :]
    vt_order = [(_R_MV, M_WIDTH), (_R_AV, ATT_KV_WIDTH), (_R_IF, N_IFG)]
    w_vt = jnp.swapaxes(jnp.concatenate([w_in[:, :, o:o + n] for o, n in vt_order], axis=-1), 1, 2).astype(BF16)
    b_vt = jnp.concatenate([b_in[:, o:o + n] for o, n in vt_order], axis=-1)[:, :, None]
    w_if = w_in[:, :, _R_IF:_R_IF + N_IFG]
    b_if = b_in[:, _R_IF:_R_IF + N_IFG]
    w_if_pad = jnp.pad(w_if, ((0, 0), (0, 0), (0, LANES - N_IFG))).astype(BF16)
    b_if_pad = jnp.pad(b_if, ((0, 0), (0, LANES - N_IFG)))[:, None, :]
    return w_main, b_main, w_vt, b_vt, w_if_pad, b_if_pad


def _trunk(x, rope_tab, layers):
    batch, seq, _ = x.shape
    xf = x.reshape(batch * seq, D_MODEL)
    for lw in layers:
        proj, q4, kv, vta, qct, kc, vtm, grow, ccol = _in_proj(xf, lw, rope_tab, seq)
        a = _attention(proj, q4, kv, vta, lw["sink"], batch, seq)
        m = _mlstm(proj, qct, kc, vtm, grow, ccol, lw["m_norm_g"], batch, seq)
        xf = _out_proj(xf, a, m, proj, lw["w_att_out"], lw["w_m_out"], lw["w_out"])
    return xf.reshape(batch, seq, D_MODEL)


def kernel(x_prompt, x_sample, norm_g, w_in, b_in, q_norm_g, k_norm_g, sink, conv_w, m_norm_g,
           w_att_out, w_m_out, w_out):
    w_main, b_main, w_vt, b_vt, w_if, b_if = _prep_weights(w_in, b_in)
    wa, wm, wo = w_att_out.astype(BF16), w_m_out.astype(BF16), w_out.astype(BF16)
    layers = []
    for l in range(DEPTH):
        layers.append(dict(
            norm_g=norm_g[l][None, :], w_main=w_main[l], b_main=b_main[l], w_vt=w_vt[l], b_vt=b_vt[l],
            w_if=w_if[l], b_if=b_if[l],
            gq=jnp.tile(q_norm_g[l] * (ATT_HEAD_DIM ** -0.5 * LOG2E), 2)[None, :],
            gk=jnp.tile(k_norm_g[l], 2)[None, :],
            sink=sink[l][None, :], conv_w=conv_w[l], m_norm_g=m_norm_g[l][None, :],
            w_att_out=wa[l], w_m_out=wm[l], w_out=wo[l]))
    outs = []
    for x in (x_prompt, x_sample):
        outs.append(_trunk(x, _rope_tables(x.shape[1]), layers))
    return tuple(outs)
```

```python
import functools
import math

import jax
import jax.numpy as jnp
from jax import lax
from jax.experimental import pallas as pl
from jax.experimental.pallas import tpu as pltpu

F32 = jnp.float32
BF16 = jnp.bfloat16

D_MODEL = 1024
DEPTH = 4
ATT_HEADS = 16
ATT_KV_HEADS = 4
ATT_GROUP = ATT_HEADS // ATT_KV_HEADS
ATT_HEAD_DIM = 64
ATT_WIDTH = ATT_HEADS * ATT_HEAD_DIM
ATT_KV_WIDTH = ATT_KV_HEADS * ATT_HEAD_DIM
WINDOW = 128
ATT_BLOCK = 128
ROPE_THETA = 500000.0
ROPE_DIM = ATT_HEAD_DIM // 4
ROPE_HALF = ROPE_DIM // 2
M_HEADS = 8
M_HEAD_DIM = 128
M_WIDTH = M_HEADS * M_HEAD_DIM
M_CHUNK = 128
CONV_K = 3
NORM_EPS = 1e-6
NEG = -1e30
LOG2E = math.log2(math.e)

LANES = 128
SUBLANES = 8
VMEM_LIMIT_BYTES = 56 * 1024 * 1024
_REF_SPLITS = (ATT_WIDTH, ATT_KV_WIDTH, ATT_KV_WIDTH, ATT_WIDTH,
               M_WIDTH, M_WIDTH, M_WIDTH, M_WIDTH, M_WIDTH,
               M_HEADS, M_HEADS, M_HEADS, M_HEADS, 2 * D_MODEL)
_REF_OFF = [0]
for _w in _REF_SPLITS:
    _REF_OFF.append(_REF_OFF[-1] + _w)
(_R_AQ, _R_AK, _R_AV, _R_AZ, _R_MQ, _R_MK, _R_MV, _R_MO, _R_MZ,
 _R_IF, _R_FF, _R_IB, _R_FB, _R_GATES) = _REF_OFF[:-1]

P_GATES = 0
P_AZ = P_GATES + 2 * D_MODEL
P_MO = P_AZ + ATT_WIDTH
P_MZ = P_MO + M_WIDTH
P_WIDTH = P_MZ + M_WIDTH
W_MQ = P_WIDTH
W_MK = W_MQ + M_WIDTH
W_AQ = W_MK + M_WIDTH
W_AK = W_AQ + ATT_WIDTH
W_WIDTH = W_AK + ATT_KV_WIDTH
VT_ROWS = M_WIDTH + ATT_KV_WIDTH
N_IFG = 4 * M_HEADS

ROW_LI_F, ROW_B_F, ROW_LI_B, ROW_B_B = 0, M_HEADS, 2 * M_HEADS, 3 * M_HEADS
LANE_C_F, LANE_C_B = 0, 2 * M_HEADS

PROJ_DTYPE = BF16
IN_TM = 512
IN_TN = 512
IN_EPILOGUES_PER_MATMUL = 2
IN_HALO = 16
IN_PIECE = 256
VT_TN = 256
OUT_TM = 1024
ATT_TQ = 512
ATT_NSB = ATT_TQ // ATT_BLOCK
ATT_SUB_GROUP = 4
ATT_PAIRS = ATT_WIDTH // LANES
ATT_KVAR_WIDTH = ATT_KV_HEADS * 2 * LANES
MLSTM_SCAN_GROUP = 32
MLSTM_OUT_GROUP = 16

_NT = (((1,), (1,)), ((), ()))


def _sigmoid(x):
    return 1.0 / (1.0 + jnp.exp(-x))


def _silu(x):
    return x * _sigmoid(x)


def _log_sigmoid(x):
    return jnp.minimum(x, 0.0) - jnp.log1p(jnp.exp(-jnp.abs(x)))


def _bf16_terms(x):
    hi = x.astype(BF16)
    r1 = x - hi.astype(F32)
    mid = r1.astype(BF16)
    lo = (r1 - mid.astype(F32)).astype(BF16)
    return hi, mid, lo


def _split_dot_left(sel, x):
    return sum(jnp.dot(sel, t, preferred_element_type=F32) for t in _bf16_terms(x))


def _split_dot_right(x, sel):
    return sum(jnp.dot(t, sel, preferred_element_type=F32) for t in _bf16_terms(x))


def _rope(y, tab):
    return (y * tab[0]
            + pltpu.roll(y, LANES - ROPE_HALF, 1) * tab[1]
            + pltpu.roll(y, ROPE_HALF, 1) * tab[2])


def _norm_rope_tile(x, gain, tab):
    lane = lax.broadcasted_iota(jnp.int32, x.shape, 1)
    left = lane < ATT_HEAD_DIM
    sq = x * x
    ss_l = jnp.sum(jnp.where(left, sq, 0.0), axis=-1, keepdims=True)
    ss_r = jnp.sum(jnp.where(left, 0.0, sq), axis=-1, keepdims=True)
    inv = jnp.where(left, lax.rsqrt(ss_l / ATT_HEAD_DIM + NORM_EPS),
                    lax.rsqrt(ss_r / ATT_HEAD_DIM + NORM_EPS))
    return _rope(x * inv * gain, tab)


def _rmsnorm_bf16(x, g):
    return (x * lax.rsqrt(jnp.mean(x * x, axis=-1, keepdims=True) + NORM_EPS) * g).astype(BF16)


def _in_proj_kernel(x_ref, xprev_ref, xnext_ref, g_ref, w_ref, b_ref, wvt_ref, bvt_ref,
                    cw_ref, tab_ref, gq_ref, gk_ref,
                    proj_ref, q4_ref, kv_ref, vta_ref, qct_ref, kc_ref, vtm_ref, grow_ref, ccol_ref,
                    xn_s, conv_s, *, tiles_per_seq):
    tm = x_ref.shape[0]
    tile_in_seq = pl.program_id(0) % tiles_per_seq

    xn_s[0:IN_HALO, :] = _rmsnorm_bf16(xprev_ref[...], g_ref[...])
    xn_s[IN_HALO:IN_HALO + tm, :] = _rmsnorm_bf16(x_ref[...], g_ref[...])
    xn_s[IN_HALO + tm:, :] = _rmsnorm_bf16(xnext_ref[...], g_ref[...])
    xn = xn_s[IN_HALO:IN_HALO + tm, :]
    xn_halo = xn_s[...]

    def chunk(c0, width, lhs=None):
        lhs = xn if lhs is None else lhs

        def piece(p0):
            cols = slice(p0, p0 + IN_PIECE)
            return lambda: jnp.dot(lhs, w_ref[:, cols], preferred_element_type=F32) + b_ref[:, cols]
        return [piece(p0) for p0 in range(c0, c0 + width, IN_PIECE)]

    stages = []

    def store_plain(c0):
        def store(acc, p0):
            def run():
                proj_ref[:, p0:p0 + IN_PIECE] = acc.astype(proj_ref.dtype)
            return run
        return lambda accs: [store(acc, c0 + i * IN_PIECE) for i, acc in enumerate(accs)]

    plain = [(chunk(c0, IN_TN), store_plain(c0)) for c0 in range(0, P_WIDTH, IN_TN)]

    has_before = tile_in_seq > 0
    has_after = tile_in_seq < tiles_per_seq - 1
    row_id = lax.broadcasted_iota(jnp.int32, (tm, LANES), 0)
    kscale = M_HEAD_DIM ** -0.5
    lane_tiles = range(0, IN_PIECE, LANES)

    def conv_silu(c0):
        def conv(acc, p0, l0):
            cols = slice(p0 + l0, p0 + l0 + LANES)

            def run():
                buf = conv_s.at[((p0 + l0) // LANES) % 2]
                buf[...] = acc[:, l0:l0 + LANES]
                lo, hi = IN_HALO - 1, IN_HALO + tm
                buf[lo:lo + 1, :] = jnp.where(has_before, acc[lo:lo + 1, l0:l0 + LANES], 0.0)
                buf[hi:hi + 1, :] = jnp.where(has_after, acc[hi:hi + 1, l0:l0 + LANES], 0.0)
                a = acc[IN_HALO:IN_HALO + tm, l0:l0 + LANES]
                xm = buf[IN_HALO - 1:IN_HALO - 1 + tm, :]
                xp = buf[IN_HALO + 1:IN_HALO + 1 + tm, :]
                y = _silu(xm * cw_ref[0:1, cols] + a * cw_ref[1:2, cols] + xp * cw_ref[2:3, cols])
                if p0 < M_WIDTH:
                    qct_ref[cols, :] = y.T.astype(BF16)
                else:
                    kc_ref[:, p0 + l0 - M_WIDTH:p0 + l0 - M_WIDTH + LANES] = (y * kscale).astype(BF16)
            return run
        return lambda accs: [conv(acc, c0 + i * IN_PIECE, l0) for i, acc in enumerate(accs) for l0 in lane_tiles]

    conv = [(chunk(W_MQ + c0, IN_TN, lhs=xn_halo), conv_silu(c0)) for c0 in range(0, 2 * M_WIDTH, IN_TN)]

    tab = tab_ref[...]

    def q_rope(c0):
        def rope(acc, p0, l0):
            def run():
                y = _norm_rope_tile(acc[:, l0:l0 + LANES], gq_ref[...], tab).astype(BF16)
                for j in range(tm // ATT_BLOCK):
                    q4_ref[j, (p0 + l0) // LANES] = y[j * ATT_BLOCK:(j + 1) * ATT_BLOCK, :]
            return run
        return lambda accs: [rope(acc, c0 + i * IN_PIECE, l0) for i, acc in enumerate(accs) for l0 in lane_tiles]

    att_q = [(chunk(W_AQ + c0, IN_TN), q_rope(c0)) for c0 in range(0, ATT_WIDTH, IN_TN)]

    left_half = lax.broadcasted_iota(jnp.int32, (tm, LANES), 1) < ATT_HEAD_DIM

    def k_norm_rope(accs):
        def norm_rope(acc, p0):
            def run():
                for t in range(IN_PIECE // LANES):
                    kt = _norm_rope_tile(acc[:, t * LANES:(t + 1) * LANES], gk_ref[...], tab)
                    sw = pltpu.roll(kt, ATT_HEAD_DIM, 1)
                    head = (p0 // LANES + t) * 2
                    variants = ((head, 0, jnp.where(left_half, kt, 0.0)), (head, 1, jnp.where(left_half, 0.0, sw)),
                                (head + 1, 0, jnp.where(left_half, sw, 0.0)), (head + 1, 1, jnp.where(left_half, 0.0, kt)))
                    for h, v, val in variants:
                        col = (2 * h + v) * LANES
                        kv_ref[:, col:col + LANES] = val.astype(BF16)
            return run
        return [norm_rope(acc, i * IN_PIECE) for i, acc in enumerate(accs)]

    att_k = (chunk(W_AK, ATT_KV_WIDTH), k_norm_rope)

    def vt_matmul():
        return lax.dot_general(wvt_ref[...], xn, _NT, preferred_element_type=F32) + bvt_ref[...]

    def store_vt(vt, r0):
        def run():
            if r0 < M_WIDTH:
                vtm_ref[r0:r0 + VT_TN, :] = vt[r0:r0 + VT_TN, :].astype(BF16)
            else:
                vta_ref[r0 - M_WIDTH:r0 - M_WIDTH + VT_TN, :] = vt[r0:r0 + VT_TN, :].astype(BF16)
        return run

    def vt_gate_epilogues(results):
        vt = results[0]
        gr = vt[VT_ROWS:VT_ROWS + N_IFG, :]
        return [store_vt(vt, r0) for r0 in range(0, VT_ROWS, VT_TN)] + [lambda: gate_epilogue(gr)]

    def gate_epilogue(gr):
        gc = jnp.concatenate([gr, jnp.zeros((LANES - N_IFG, tm), F32)], axis=0).T
        ls_c = _log_sigmoid(gc)
        ls_r = _log_sigmoid(gr)
        r = lax.broadcasted_iota(jnp.int32, (M_CHUNK, M_CHUNK), 0)
        c = lax.broadcasted_iota(jnp.int32, (M_CHUNK, M_CHUNK), 1)
        tril = jnp.where(c <= r, 1.0, 0.0).astype(BF16)
        triu = jnp.where(c >= r, 1.0, 0.0).astype(BF16)
        lane = lax.broadcasted_iota(jnp.int32, (M_CHUNK, LANES), 1)
        fwd_lanes = lane < 2 * M_HEADS
        grow_ref[ROW_LI_F:ROW_LI_F + M_HEADS, :] = gr[0:M_HEADS, :]
        grow_ref[ROW_LI_B:ROW_LI_B + M_HEADS, :] = gr[2 * M_HEADS:3 * M_HEADS, :]
        for ci in range(tm // M_CHUNK):
            rows = slice(ci * M_CHUNK, (ci + 1) * M_CHUNK)
            lsc = ls_c[rows, :]
            b_f = _split_dot_left(tril, lsc)
            b_b = _split_dot_left(triu, lsc)
            cc = gc[rows, :] - pltpu.roll(jnp.where(fwd_lanes, b_f, b_b), LANES - M_HEADS, 1)
            for ti, term in enumerate(_bf16_terms(cc)):
                ccol_ref[ti, rows, :] = term
            grow_ref[ROW_B_F:ROW_B_F + M_HEADS, rows] = _split_dot_right(ls_r[M_HEADS:2 * M_HEADS, rows], triu)
            grow_ref[ROW_B_B:ROW_B_B + M_HEADS, rows] = _split_dot_right(ls_r[3 * M_HEADS:4 * M_HEADS, rows], tril)

    light = plain
    heavy = [([vt_matmul], vt_gate_epilogues),
             conv[0], att_q[0], conv[1], att_q[1], conv[2], att_k, conv[3]]
    stages.append(heavy[0])
    for i, stage in enumerate(heavy[1:]):
        stages.extend([stage, light[i]])
    stages.extend(light[len(heavy) - 1:])

    n_matmuls = sum(len(matmuls) for matmuls, _ in stages)
    queue = []
    emitted = 0
    for matmuls, make_epilogues in stages:
        results = []
        for matmul in matmuls:
            results.append(matmul())
            emitted += 1
            take = max(IN_EPILOGUES_PER_MATMUL, -(-len(queue) // max(n_matmuls - emitted, 1))) if queue else 0
            for run in queue[:take]:
                run()
            queue = queue[take:]
        queue.extend(make_epilogues(results))
    for run in queue:
        run()


def _in_proj(x, lw, rope_tab, seq):
    t = x.shape[0]
    tps = seq // IN_TM
    halo_blocks = IN_TM // IN_HALO
    const = lambda i: (0, 0)
    tile = lambda i: (i, 0)
    cols = lambda i: (0, i)
    resident = dict(pipeline_mode=pl.Buffered(1))
    return pl.pallas_call(
        functools.partial(_in_proj_kernel, tiles_per_seq=tps),
        grid=(t // IN_TM,),
        in_specs=[
            pl.BlockSpec((IN_TM, D_MODEL), tile),
            pl.BlockSpec((IN_HALO, D_MODEL), lambda i: (jnp.maximum(i * halo_blocks - 1, 0), 0)),
            pl.BlockSpec((IN_HALO, D_MODEL), lambda i: (jnp.minimum((i + 1) * halo_blocks, t // IN_HALO - 1), 0)),
            pl.BlockSpec((1, D_MODEL), const),
            pl.BlockSpec((D_MODEL, W_WIDTH), const, **resident),
            pl.BlockSpec((1, W_WIDTH), const),
            pl.BlockSpec((VT_ROWS + N_IFG, D_MODEL), const, **resident),
            pl.BlockSpec((VT_ROWS + N_IFG, 1), const),
            pl.BlockSpec((CONV_K, 2 * M_WIDTH), const),
            pl.BlockSpec((3, IN_TM, LANES), lambda i: (0, i % tps, 0)),
            pl.BlockSpec((1, LANES), const),
            pl.BlockSpec((1, LANES), const),
        ],
        out_specs=[
            pl.BlockSpec((IN_TM, P_WIDTH), tile),
            pl.BlockSpec((IN_TM // ATT_BLOCK, ATT_PAIRS, ATT_BLOCK, LANES), lambda i: (i, 0, 0, 0)),
            pl.BlockSpec((IN_TM, ATT_KVAR_WIDTH), tile),
            pl.BlockSpec((ATT_KV_WIDTH, IN_TM), cols),
            pl.BlockSpec((M_WIDTH, IN_TM), cols),
            pl.BlockSpec((IN_TM, M_WIDTH), tile),
            pl.BlockSpec((M_WIDTH, IN_TM), cols),
            pl.BlockSpec((N_IFG, IN_TM), cols),
            pl.BlockSpec((3, IN_TM, LANES), lambda i: (0, i, 0)),
        ],
        out_shape=[
            jax.ShapeDtypeStruct((t, P_WIDTH), PROJ_DTYPE),
            jax.ShapeDtypeStruct((t // ATT_BLOCK, ATT_PAIRS, ATT_BLOCK, LANES), BF16),
            jax.ShapeDtypeStruct((t, ATT_KVAR_WIDTH), BF16),
            jax.ShapeDtypeStruct((ATT_KV_WIDTH, t), BF16),
            jax.ShapeDtypeStruct((M_WIDTH, t), BF16),
            jax.ShapeDtypeStruct((t, M_WIDTH), BF16),
            jax.ShapeDtypeStruct((M_WIDTH, t), BF16),
            jax.ShapeDtypeStruct((N_IFG, t), F32),
            jax.ShapeDtypeStruct((3, t, LANES), BF16),
        ],
        scratch_shapes=[pltpu.VMEM((IN_TM + 2 * IN_HALO, D_MODEL), BF16),
                        pltpu.VMEM((2, IN_TM + 2 * IN_HALO, LANES), F32)],
        compiler_params=pltpu.CompilerParams(
            dimension_semantics=("arbitrary",), vmem_limit_bytes=VMEM_LIMIT_BYTES),
        name="in_proj",
    )(x, x, x, lw["norm_g"], lw["w_main"], lw["b_main"], lw["w_vt"], lw["b_vt"],
      lw["conv_w"], rope_tab, lw["gq"], lw["gk"])


def _attention_kernel(q4_ref, z_ref, kp_ref, kc_ref, kn_ref, vp_ref, vc_ref, vn_ref, sink_ref, out_ref,
                      ks, vts, out_t, *, n_blocks):
    ti = pl.program_id(1)
    bq = ATT_BLOCK

    r0 = 0
    for k_ref, v_ref in ((kp_ref, vp_ref), (kc_ref, vc_ref), (kn_ref, vn_ref)):
        nr = k_ref.shape[0]
        ks[r0:r0 + nr, :] = k_ref[...]
        vts[:, r0:r0 + nr] = v_ref[...]
        r0 += nr

    def group_heads(h):
        return (ATT_GROUP * h, ATT_GROUP * h + 2, ATT_GROUP * h + 1, ATT_GROUP * h + 3)

    sink_rows = [jnp.concatenate([jnp.broadcast_to(sink_ref[:, c:c + 1] * LOG2E, (1, bq)) for c in group_heads(h)],
                                 axis=1) for h in range(ATT_KV_HEADS)]

    kk = lax.broadcasted_iota(jnp.int32, (bq, bq), 0)
    qq = lax.broadcasted_iota(jnp.int32, (bq, bq), 1)
    prev_in_band = kk >= qq
    next_in_band = kk <= qq

    JG = ATT_SUB_GROUP

    def sub_blocks(jg, carry):
        js = [jg * JG + u for u in range(JG)]
        row0s = [pl.multiple_of(j * bq, bq) for j in js]
        biases = []
        for j in js:
            blk = ti * ATT_NSB + j
            bias_prev = jnp.where(prev_in_band, jnp.where(blk > 0, 0.0, NEG), NEG).astype(F32)
            bias_next = jnp.where(next_in_band, jnp.where(blk < n_blocks - 1, 0.0, NEG), NEG).astype(F32)
            biases.append((jnp.concatenate([bias_prev] * ATT_GROUP, axis=1),
                           jnp.concatenate([bias_next] * ATT_GROUP, axis=1)))
        q_slabs = [q4_ref[j] for j in js]
        pairs = [(u, h) for u in range(JG) for h in range(ATT_KV_HEADS)]
        q_pairs = [q_slabs[u][2 * h:2 * h + 2].reshape(2 * bq, LANES) for u, h in pairs]
        ss = [jnp.concatenate(
            [lax.dot_general(ks[pl.ds(row0s[u], 3 * bq), (2 * h + v) * LANES:(2 * h + v + 1) * LANES], q_pair, _NT,
                             preferred_element_type=F32)
             for v in range(2)], axis=1) for (u, h), q_pair in zip(pairs, q_pairs)]
        ss = [jnp.concatenate([s[:bq] + biases[u][0], s[bq:2 * bq], s[2 * bq:] + biases[u][1]], axis=0)
              for (u, h), s in zip(pairs, ss)]
        ms = [jnp.maximum(jnp.max(s, axis=0, keepdims=True), sink_rows[h]) for (u, h), s in zip(pairs, ss)]
        ps = [jnp.exp2(s - m) for s, m in zip(ss, ms)]
        denoms = [jnp.sum(p, axis=0, keepdims=True) + jnp.exp2(sink_rows[h] - m)
                  for (u, h), p, m in zip(pairs, ps, ms)]
        os_ = [jnp.dot(vts[h * ATT_HEAD_DIM:(h + 1) * ATT_HEAD_DIM, pl.ds(row0s[u], 3 * bq)], p.astype(BF16),
                       preferred_element_type=F32) * (1.0 / denom)
               for (u, h), p, denom in zip(pairs, ps, denoms)]
        for (u, h), o in zip(pairs, os_):
            for i, c in enumerate(group_heads(h)):
                out_t[u, c * ATT_HEAD_DIM:(c + 1) * ATT_HEAD_DIM, :] = o[:, i * bq:(i + 1) * bq]
        for u in range(JG):
            att = out_t[u].T
            z = z_ref[pl.ds(row0s[u], bq), :].astype(F32)
            out_ref[pl.ds(row0s[u], bq), :] = (att * _silu(z)).astype(out_ref.dtype)
        return carry

    lax.fori_loop(0, ATT_NSB // JG, sub_blocks, 0)


def _attention(proj, q4, kv, vta, sink, batch, seq):
    t = proj.shape[0]
    nb = seq // ATT_BLOCK
    nt = seq // ATT_TQ
    bq = ATT_BLOCK

    prev_blk = lambda b, i: b * nb + jnp.maximum(i * ATT_NSB - 1, 0)
    next_blk = lambda b, i: b * nb + jnp.minimum((i + 1) * ATT_NSB, nb - 1)
    return pl.pallas_call(
        functools.partial(_attention_kernel, n_blocks=nb),
        grid=(batch, nt),
        in_specs=[
            pl.BlockSpec((ATT_NSB, ATT_PAIRS, bq, LANES), lambda b, i: (b * nt + i, 0, 0, 0)),
            pl.BlockSpec((ATT_TQ, ATT_WIDTH), lambda b, i: (b * nt + i, P_AZ // ATT_WIDTH)),
            pl.BlockSpec((bq, ATT_KVAR_WIDTH), lambda b, i: (prev_blk(b, i), 0)),
            pl.BlockSpec((ATT_TQ, ATT_KVAR_WIDTH), lambda b, i: (b * nt + i, 0)),
            pl.BlockSpec((bq, ATT_KVAR_WIDTH), lambda b, i: (next_blk(b, i), 0)),
            pl.BlockSpec((ATT_KV_WIDTH, bq), lambda b, i: (0, prev_blk(b, i))),
            pl.BlockSpec((ATT_KV_WIDTH, ATT_TQ), lambda b, i: (0, b * nt + i)),
            pl.BlockSpec((ATT_KV_WIDTH, bq), lambda b, i: (0, next_blk(b, i))),
            pl.BlockSpec((1, ATT_HEADS), lambda b, i: (0, 0)),
        ],
        out_specs=pl.BlockSpec((ATT_TQ, ATT_WIDTH), lambda b, i: (b * nt + i, 0)),
        out_shape=jax.ShapeDtypeStruct((t, ATT_WIDTH), BF16),
        scratch_shapes=[
            pltpu.VMEM((ATT_TQ + 2 * bq, ATT_KVAR_WIDTH), BF16),
            pltpu.VMEM((ATT_KV_WIDTH, ATT_TQ + 2 * bq), BF16),
            pltpu.VMEM((ATT_SUB_GROUP, ATT_WIDTH, bq), F32),
        ],
        compiler_params=pltpu.CompilerParams(
            dimension_semantics=("arbitrary", "arbitrary"), vmem_limit_bytes=VMEM_LIMIT_BYTES),
        name="attention",
    )(q4, proj, kv, kv, kv, vta, vta, vta, sink)


def _mlstm_kernel(qct_ref, kc_ref, vt_ref, o_ref, z_ref, grow_ref, ccol_ref, ng_ref,
                  out_ref, cct_s, nf_s, nb_s, mf_s, mb_s, *, seq):
    hd = pl.program_id(1)
    L = M_CHUNK
    dv = M_HEAD_DIM
    nc = seq // L

    head_row = lax.broadcasted_iota(jnp.int32, (M_HEADS, L), 0) == hd

    def gate_row(base, t0):
        rows = grow_ref[base:base + M_HEADS, pl.ds(t0, L)]
        return jnp.sum(jnp.where(head_row, rows, 0.0), axis=0, keepdims=True)

    U = min(MLSTM_SCAN_GROUP, nc)
    dirs = ((ROW_LI_F, ROW_B_F, L - 1, nf_s, mf_s, 0), (ROW_LI_B, ROW_B_B, 0, nb_s, mb_s, dv))

    def scan_body(gj, carry):
        cis = [[gj * U + u for u in range(U)], [nc - 1 - (gj * U + u) for u in range(U)]]
        t0s = [[pl.multiple_of(ci * L, L) for ci in row] for row in cis]
        lis = [[gate_row(dirs[d][0], t0) for t0 in t0s[d]] for d in range(2)]
        bs = [[gate_row(dirs[d][1], t0) for t0 in t0s[d]] for d in range(2)]
        b_lasts = [[b[:, dirs[d][2]:dirs[d][2] + 1] for b in bs[d]] for d in range(2)]
        gs = [[b_last - b + li for b_last, b, li in zip(b_lasts[d], bs[d], lis[d])] for d in range(2)]
        g_maxs = [[jnp.max(g, axis=-1, keepdims=True) for g in gs[d]] for d in range(2)]
        ks = [[kc_ref[pl.ds(t0, L), :] for t0 in t0s[d]] for d in range(2)]
        vts = [[vt_ref[:, pl.ds(t0, L)].astype(F32) for t0 in t0s[d]] for d in range(2)]

        ms = [[carry[d][2]] for d in range(2)]
        for d in range(2):
            for u in range(U):
                ms[d].append(jnp.maximum(b_lasts[d][u] + ms[d][u], g_maxs[d][u]))
        w_cs = [[jnp.exp(b_lasts[d][u] + ms[d][u] - ms[d][u + 1]) for u in range(U)] for d in range(2)]
        w_ks = [[jnp.exp(gs[d][u] - ms[d][u + 1]) for u in range(U)] for d in range(2)]
        c_upds = [[jnp.dot((vts[d][u] * w_ks[d][u]).astype(BF16), ks[d][u], preferred_element_type=F32)
                   for u in range(U)] for d in range(2)]
        n_upds = [[jnp.dot(jnp.broadcast_to(w_ks[d][u], (SUBLANES, L)).astype(BF16), ks[d][u],
                           preferred_element_type=F32)[0:1, :] for u in range(U)] for d in range(2)]

        out = []
        for d in range(2):
            _, _, _, n_s, m_s, row0 = dirs[d]
            ct_st, n_st, _ = carry[d]
            for u in range(U):
                ci = cis[d][u]
                cct_s[ci, row0:row0 + dv, :] = ct_st.astype(BF16)
                n_s[ci] = jnp.broadcast_to(n_st, (SUBLANES, LANES))
                m_s[ci] = jnp.broadcast_to(ms[d][u], (SUBLANES, LANES))
                ct_st = w_cs[d][u] * ct_st + c_upds[d][u]
                n_st = w_cs[d][u] * n_st + n_upds[d][u]
            out.append((ct_st, n_st, ms[d][U]))
        return tuple(out)

    rr = lax.broadcasted_iota(jnp.int32, (L, L), 0)
    cc = lax.broadcasted_iota(jnp.int32, (L, L), 1)
    row8 = lax.broadcasted_iota(jnp.int32, (SUBLANES, LANES), 0)
    sel_k = lax.broadcasted_iota(jnp.int32, (LANES, 2 * L), 0)
    sel_n = lax.broadcasted_iota(jnp.int32, (LANES, 2 * L), 1)
    sel_lane = jnp.where(sel_n < L, LANE_C_F + hd, LANE_C_B + hd)
    sel = jnp.where(sel_k == sel_lane, 1.0, 0.0).astype(BF16)
    keep_f = rr <= cc
    keep_b = rr >= cc
    G = min(MLSTM_OUT_GROUP, nc)

    def out_pre(gi):
        cis = [gi * G + g for g in range(G)]
        t0s = [ci * L for ci in cis]
        qts = [qct_ref[:, t0:t0 + L] for t0 in t0s]
        qk_ts = [jnp.dot(kc_ref[t0:t0 + L, :], qt, preferred_element_type=F32)
                 for t0, qt in zip(t0s, qts)]
        c_bcs = [sum(jnp.dot(ccol_ref[ti, t0:t0 + L, :], sel, preferred_element_type=F32) for ti in range(3))
                 for t0 in t0s]
        items = []
        for g in range(G):
            items.append((g, c_bcs[g][:, :L], ROW_B_F, keep_f, mf_s))
            items.append((g, c_bcs[g][:, L:], ROW_B_B, keep_b, mb_s))
        b_rs = [gate_row(b_base, t0s[g]) for g, _, b_base, _, _ in items]
        ds = [jnp.where(keep, b_r + c_s, NEG)
              for (_, c_s, _, keep, _), b_r in zip(items, b_rs)]
        d_maxs = [jnp.max(d, axis=0, keepdims=True) for d in ds]
        return cis, t0s, qts, qk_ts, items, b_rs, ds, d_maxs

    def out_post(pre):
        cis, t0s, qts, qk_ts, items, b_rs, ds, d_maxs = pre
        cqs = [jnp.dot(cct_s[ci], qt, preferred_element_type=F32) for ci, qt in zip(cis, qts)]
        n8s = [jnp.where(row8 == 0, nf_s[ci], jnp.where(row8 == 1, nb_s[ci], 0.0)).astype(BF16) for ci in cis]
        qns = [jnp.dot(n8, qt, preferred_element_type=F32) for n8, qt in zip(n8s, qts)]
        qn_rows = [qns[it[0]][i % 2:i % 2 + 1, :] for i, it in enumerate(items)]
        m_prevs = [it[4][cis[it[0]]][0:1, 0:1] for it in items]
        m_inters = [b_r + m_prev for b_r, m_prev in zip(b_rs, m_prevs)]
        m_ts = [jnp.maximum(m_inter, d_max) for m_inter, d_max in zip(m_inters, d_maxs)]
        w_inters = [jnp.exp(m_inter - m_t) for m_inter, m_t in zip(m_inters, m_ts)]
        a_s = [jnp.exp(d - m_t) * qk_ts[it[0]] for d, m_t, it in zip(ds, m_ts, items)]
        dens = [w_inter * qn + jnp.sum(a, axis=0, keepdims=True) for w_inter, qn, a in zip(w_inters, qn_rows, a_s)]
        invs = [1.0 / jnp.maximum(jnp.abs(den), jnp.exp(-m_t)) for den, m_t in zip(dens, m_ts)]
        ps = [a * inv for a, inv in zip(a_s, invs)]
        ws = [w_inter * inv for w_inter, inv in zip(w_inters, invs)]

        h_ts = [jnp.dot(vt_ref[:, t0s[g]:t0s[g] + L], (ps[2 * g] + ps[2 * g + 1]).astype(BF16),
                        preferred_element_type=F32)
                + cqs[g][:dv, :] * ws[2 * g] + cqs[g][dv:, :] * ws[2 * g + 1] for g in range(G)]
        hs = [_sigmoid(o_ref[t0:t0 + L, :].astype(F32)) * h_t.T for t0, h_t in zip(t0s, h_ts)]
        hs = [h * lax.rsqrt(jnp.mean(h * h, axis=-1, keepdims=True) + NORM_EPS) * ng_ref[...] for h in hs]
        for t0, h in zip(t0s, hs):
            out_ref[t0:t0 + L, :] = (h * _silu(z_ref[t0:t0 + L, :].astype(F32))).astype(out_ref.dtype)

    pre = out_pre(0)
    init = (jnp.zeros((dv, M_HEAD_DIM), F32), jnp.zeros((1, M_HEAD_DIM), F32),
            jnp.full((1, 1), NEG, F32))
    lax.fori_loop(0, nc // U, scan_body, (init, init))
    for gi in range(nc // G):
        nxt = out_pre(gi + 1) if gi + 1 < nc // G else None
        out_post(pre)
        pre = nxt


def _mlstm(proj, qct, kc, vtm, grow, ccol, ng, batch, seq):
    t = proj.shape[0]
    nc = seq // M_CHUNK
    head_rows = lambda b, h: (h, b)
    head_cols = lambda b, h: (b, h)

    def proj_head(col0):
        return pl.BlockSpec((seq, M_HEAD_DIM), lambda b, h: (b, col0 // M_HEAD_DIM + h))

    return pl.pallas_call(
        functools.partial(_mlstm_kernel, seq=seq),
        grid=(batch, M_HEADS),
        in_specs=[
            pl.BlockSpec((M_HEAD_DIM, seq), head_rows),
            pl.BlockSpec((seq, M_HEAD_DIM), head_cols),
            pl.BlockSpec((M_HEAD_DIM, seq), head_rows),
            proj_head(P_MO), proj_head(P_MZ),
            pl.BlockSpec((N_IFG, seq), lambda b, h: (0, b)),
            pl.BlockSpec((3, seq, LANES), lambda b, h: (0, b, 0)),
            pl.BlockSpec((1, M_HEAD_DIM), lambda b, h: (0, h)),
        ],
        out_specs=pl.BlockSpec((seq, M_HEAD_DIM), head_cols),
        out_shape=jax.ShapeDtypeStruct((t, M_WIDTH), BF16),
        scratch_shapes=[
            pltpu.VMEM((nc, 2 * M_HEAD_DIM, M_HEAD_DIM), BF16),
            pltpu.VMEM((nc, SUBLANES, M_HEAD_DIM), F32),
            pltpu.VMEM((nc, SUBLANES, M_HEAD_DIM), F32),
            pltpu.VMEM((nc, SUBLANES, LANES), F32),
            pltpu.VMEM((nc, SUBLANES, LANES), F32),
        ],
        compiler_params=pltpu.CompilerParams(
            dimension_semantics=("arbitrary", "arbitrary"), vmem_limit_bytes=VMEM_LIMIT_BYTES),
        name="mlstm",
    )(qct, kc, vtm, proj, proj, grow, ccol, ng)


def _out_proj_kernel(x_ref, a_ref, m_ref, gates_ref, wa_ref, wm_ref, wo_ref, out_ref):
    branch_a = jnp.dot(a_ref[...], wa_ref[...], preferred_element_type=F32)
    branch_m = jnp.dot(m_ref[...], wm_ref[...], preferred_element_type=F32)
    gates = _sigmoid(gates_ref[...].astype(F32))
    merged = gates[:, :D_MODEL] * branch_a + gates[:, D_MODEL:] * branch_m
    out_ref[...] = x_ref[...] + jnp.dot(merged.astype(BF16), wo_ref[...], preferred_element_type=F32)


def _out_proj(x, a, m, proj, wa, wm, wo):
    t = x.shape[0]
    tile = lambda i: (i, 0)
    const = lambda i: (0, 0)
    return pl.pallas_call(
        _out_proj_kernel,
        grid=(t // OUT_TM,),
        in_specs=[
            pl.BlockSpec((OUT_TM, D_MODEL), tile),
            pl.BlockSpec((OUT_TM, ATT_WIDTH), tile),
            pl.BlockSpec((OUT_TM, M_WIDTH), tile),
            pl.BlockSpec((OUT_TM, 2 * D_MODEL), lambda i: (i, P_GATES // (2 * D_MODEL))),
            pl.BlockSpec((ATT_WIDTH, D_MODEL), const),
            pl.BlockSpec((M_WIDTH, D_MODEL), const),
            pl.BlockSpec((D_MODEL, D_MODEL), const),
        ],
        out_specs=pl.BlockSpec((OUT_TM, D_MODEL), tile),
        out_shape=jax.ShapeDtypeStruct((t, D_MODEL), F32),
        compiler_params=pltpu.CompilerParams(
            dimension_semantics=("arbitrary",), vmem_limit_bytes=VMEM_LIMIT_BYTES),
        name="out_proj",
    )(x, a, m, proj, wa, wm, wo)


def _rope_tables(seq):
    inv = jnp.power(jnp.float32(ROPE_THETA), -jnp.arange(ROPE_HALF, dtype=F32) * 2.0 / ROPE_DIM)
    ang = jnp.arange(seq, dtype=F32)[:, None] * inv[None, :]
    cos, sin = jnp.cos(ang), jnp.sin(ang)
    pad = ATT_HEAD_DIM - ROPE_DIM
    ones = jnp.ones((seq, pad), F32)
    zeros = jnp.zeros((seq, pad), F32)
    zh = jnp.zeros((seq, ROPE_HALF), F32)
    c = jnp.concatenate([cos, cos, ones], axis=-1)
    s1 = jnp.concatenate([-sin, zh, zeros], axis=-1)
    s2 = jnp.concatenate([zh, sin, zeros], axis=-1)
    tab = jnp.stack([c, s1, s2])
    return jnp.concatenate([tab, tab], axis=-1)


def _prep_weights(w_in, b_in):
    order = [(_R_GATES, 2 * D_MODEL), (_R_AZ, ATT_WIDTH), (_R_MO, M_WIDTH), (_R_MZ, M_WIDTH),
             (_R_MQ, M_WIDTH), (_R_MK, M_WIDTH), (_R_AQ, ATT_WIDTH), (_R_AK, ATT_KV_WIDTH)]
    w_main = jnp.concatenate([w_in[:, :, o:o + n] for o, n in order], axis=-1).astype(BF16)
    b_main = jnp.concatenate([b_in[:, o:o + n] for o, n in order], axis=-1)[:, None, :]
    vt_order = [(_R_MV, M_WIDTH), (_R_AV, ATT_KV_WIDTH), (_R_IF, N_IFG)]
    w_vt = jnp.swapaxes(jnp.concatenate([w_in[:, :, o:o + n] for o, n in vt_order], axis=-1), 1, 2).astype(BF16)
    b_vt = jnp.concatenate([b_in[:, o:o + n] for o, n in vt_order], axis=-1)[:, :, None]
    return w_main, b_main, w_vt, b_vt


def _trunk(x, rope_tab, layers):
    batch, seq, _ = x.shape
    xf = x.reshape(batch * seq, D_MODEL)
    for lw in layers:
        proj, q4, kv, vta, qct, kc, vtm, grow, ccol = _in_proj(xf, lw, rope_tab, seq)
        a = _attention(proj, q4, kv, vta, lw["sink"], batch, seq)
        m = _mlstm(proj, qct, kc, vtm, grow, ccol, lw["m_norm_g"], batch, seq)
        xf = _out_proj(xf, a, m, proj, lw["w_att_out"], lw["w_m_out"], lw["w_out"])
    return xf.reshape(batch, seq, D_MODEL)


def kernel(x_prompt, x_sample, norm_g, w_in, b_in, q_norm_g, k_norm_g, sink, conv_w, m_norm_g,
           w_att_out, w_m_out, w_out):
    w_main, b_main, w_vt, b_vt = _prep_weights(w_in, b_in)
    wa, wm, wo = w_att_out.astype(BF16), w_m_out.astype(BF16), w_out.astype(BF16)
    layers = []
    for l in range(DEPTH):
        layers.append(dict(
            norm_g=norm_g[l][None, :], w_main=w_main[l], b_main=b_main[l], w_vt=w_vt[l], b_vt=b_vt[l],
            gq=jnp.tile(q_norm_g[l] * (ATT_HEAD_DIM ** -0.5 * LOG2E), 2)[None, :],
            gk=jnp.tile(k_norm_g[l], 2)[None, :],
            sink=sink[l][None, :], conv_w=conv_w[l], m_norm_g=m_norm_g[l][None, :],
            w_att_out=wa[l], w_m_out=wm[l], w_out=wo[l]))
    outs = []
    for x in (x_prompt, x_sample):
        outs.append(_trunk(x, _rope_tables(x.shape[1]), layers))
    return tuple(outs)
```

```python
import functools
import math

import jax
import jax.numpy as jnp
from jax import lax
from jax.experimental import pallas as pl
from jax.experimental.pallas import tpu as pltpu

F32 = jnp.float32
BF16 = jnp.bfloat16

D_MODEL = 1024
DEPTH = 4
ATT_HEADS = 16
ATT_KV_HEADS = 4
ATT_GROUP = ATT_HEADS // ATT_KV_HEADS
ATT_HEAD_DIM = 64
ATT_WIDTH = ATT_HEADS * ATT_HEAD_DIM
ATT_KV_WIDTH = ATT_KV_HEADS * ATT_HEAD_DIM
WINDOW = 128
ATT_BLOCK = 128
ROPE_THETA = 500000.0
ROPE_DIM = ATT_HEAD_DIM // 4
ROPE_HALF = ROPE_DIM // 2
M_HEADS = 8
M_HEAD_DIM = 128
M_WIDTH = M_HEADS * M_HEAD_DIM
M_CHUNK = 128
CONV_K = 3
NORM_EPS = 1e-6
NEG = -1e30
LOG2E = math.log2(math.e)

LANES = 128
SUBLANES = 8
VMEM_LIMIT_BYTES = 56 * 1024 * 1024
_REF_SPLITS = (ATT_WIDTH, ATT_KV_WIDTH, ATT_KV_WIDTH, ATT_WIDTH,
               M_WIDTH, M_WIDTH, M_WIDTH, M_WIDTH, M_WIDTH,
               M_HEADS, M_HEADS, M_HEADS, M_HEADS, 2 * D_MODEL)
_REF_OFF = [0]
for _w in _REF_SPLITS:
    _REF_OFF.append(_REF_OFF[-1] + _w)
(_R_AQ, _R_AK, _R_AV, _R_AZ, _R_MQ, _R_MK, _R_MV, _R_MO, _R_MZ,
 _R_IF, _R_FF, _R_IB, _R_FB, _R_GATES) = _REF_OFF[:-1]

P_GATES = 0
P_AZ = P_GATES + 2 * D_MODEL
P_MO = P_AZ + ATT_WIDTH
P_MZ = P_MO + M_WIDTH
P_WIDTH = P_MZ + M_WIDTH
W_MQ = P_WIDTH
W_MK = W_MQ + M_WIDTH
W_AQ = W_MK + M_WIDTH
W_AK = W_AQ + ATT_WIDTH
W_WIDTH = W_AK + ATT_KV_WIDTH
VT_ROWS = M_WIDTH + ATT_KV_WIDTH
N_IFG = 4 * M_HEADS

ROW_LI_F, ROW_B_F, ROW_LI_B, ROW_B_B = 0, M_HEADS, 2 * M_HEADS, 3 * M_HEADS
LANE_C_F, LANE_C_B = 0, 2 * M_HEADS

PROJ_DTYPE = BF16
IN_TM = 512
IN_TN = 512
IN_EPILOGUES_PER_MATMUL = 2
IN_HALO = 16
IN_PIECE = 256
VT_TN = 256
OUT_TM = 1024
ATT_TQ = 512
ATT_NSB = ATT_TQ // ATT_BLOCK
ATT_SUB_GROUP = 4
ATT_PAIRS = ATT_WIDTH // LANES
ATT_KVAR_WIDTH = ATT_KV_HEADS * 2 * LANES
MLSTM_SCAN_GROUP = 32
MLSTM_OUT_GROUP = 16

_NT = (((1,), (1,)), ((), ()))


def _sigmoid(x):
    return 1.0 / (1.0 + jnp.exp(-x))


def _silu(x):
    return x * _sigmoid(x)


def _log_sigmoid(x):
    return jnp.minimum(x, 0.0) - jnp.log1p(jnp.exp(-jnp.abs(x)))


def _bf16_terms(x):
    hi = x.astype(BF16)
    r1 = x - hi.astype(F32)
    mid = r1.astype(BF16)
    lo = (r1 - mid.astype(F32)).astype(BF16)
    return hi, mid, lo


def _split_dot_right(x, sel):
    return sum(jnp.dot(t, sel, preferred_element_type=F32) for t in _bf16_terms(x))


def _rope(y, tab):
    return (y * tab[0]
            + pltpu.roll(y, LANES - ROPE_HALF, 1) * tab[1]
            + pltpu.roll(y, ROPE_HALF, 1) * tab[2])


def _norm_rope_tile(x, gain, tab):
    lane = lax.broadcasted_iota(jnp.int32, x.shape, 1)
    left = lane < ATT_HEAD_DIM
    sq = x * x
    ss_l = jnp.sum(jnp.where(left, sq, 0.0), axis=-1, keepdims=True)
    ss_r = jnp.sum(jnp.where(left, 0.0, sq), axis=-1, keepdims=True)
    inv = jnp.where(left, lax.rsqrt(ss_l / ATT_HEAD_DIM + NORM_EPS),
                    lax.rsqrt(ss_r / ATT_HEAD_DIM + NORM_EPS))
    return _rope(x * inv * gain, tab)


def _rmsnorm_bf16(x, g):
    return (x * lax.rsqrt(jnp.mean(x * x, axis=-1, keepdims=True) + NORM_EPS) * g).astype(BF16)


def _in_proj_kernel(x_ref, xprev_ref, xnext_ref, g_ref, w_ref, b_ref, wvt_ref, bvt_ref,
                    cw_ref, tab_ref, gq_ref, gk_ref,
                    proj_ref, q4_ref, kv_ref, vta_ref, qct_ref, kc_ref, vtm_ref, grow_ref, ccol_ref,
                    xn_s, conv_s, *, tiles_per_seq):
    tm = x_ref.shape[0]
    tile_in_seq = pl.program_id(0) % tiles_per_seq

    xn_s[0:IN_HALO, :] = _rmsnorm_bf16(xprev_ref[...], g_ref[...])
    xn_s[IN_HALO:IN_HALO + tm, :] = _rmsnorm_bf16(x_ref[...], g_ref[...])
    xn_s[IN_HALO + tm:, :] = _rmsnorm_bf16(xnext_ref[...], g_ref[...])
    xn = xn_s[IN_HALO:IN_HALO + tm, :]
    xn_halo = xn_s[...]

    def chunk(c0, width, lhs=None):
        lhs = xn if lhs is None else lhs

        def piece(p0):
            cols = slice(p0, p0 + IN_PIECE)
            return lambda: jnp.dot(lhs, w_ref[:, cols], preferred_element_type=F32) + b_ref[:, cols]
        return [piece(p0) for p0 in range(c0, c0 + width, IN_PIECE)]

    stages = []

    def store_plain(c0):
        def store(acc, p0):
            def run():
                proj_ref[:, p0:p0 + IN_PIECE] = acc.astype(proj_ref.dtype)
            return run
        return lambda accs: [store(acc, c0 + i * IN_PIECE) for i, acc in enumerate(accs)]

    plain = [(chunk(c0, IN_TN), store_plain(c0)) for c0 in range(0, P_WIDTH, IN_TN)]

    has_before = tile_in_seq > 0
    has_after = tile_in_seq < tiles_per_seq - 1
    row_id = lax.broadcasted_iota(jnp.int32, (tm, LANES), 0)
    kscale = M_HEAD_DIM ** -0.5
    lane_tiles = range(0, IN_PIECE, LANES)

    def conv_silu(c0):
        def conv(acc, p0, l0):
            cols = slice(p0 + l0, p0 + l0 + LANES)

            def run():
                buf = conv_s.at[((p0 + l0) // LANES) % 2]
                buf[...] = acc[:, l0:l0 + LANES]
                lo, hi = IN_HALO - 1, IN_HALO + tm
                buf[lo:lo + 1, :] = jnp.where(has_before, acc[lo:lo + 1, l0:l0 + LANES], 0.0)
                buf[hi:hi + 1, :] = jnp.where(has_after, acc[hi:hi + 1, l0:l0 + LANES], 0.0)
                a = acc[IN_HALO:IN_HALO + tm, l0:l0 + LANES]
                xm = buf[IN_HALO - 1:IN_HALO - 1 + tm, :]
                xp = buf[IN_HALO + 1:IN_HALO + 1 + tm, :]
                y = _silu(xm * cw_ref[0:1, cols] + a * cw_ref[1:2, cols] + xp * cw_ref[2:3, cols])
                if p0 < M_WIDTH:
                    qct_ref[cols, :] = y.T.astype(BF16)
                else:
                    kc_ref[:, p0 + l0 - M_WIDTH:p0 + l0 - M_WIDTH + LANES] = (y * kscale).astype(BF16)
            return run
        return lambda accs: [conv(acc, c0 + i * IN_PIECE, l0) for i, acc in enumerate(accs) for l0 in lane_tiles]

    conv = [(chunk(W_MQ + c0, IN_TN, lhs=xn_halo), conv_silu(c0)) for c0 in range(0, 2 * M_WIDTH, IN_TN)]

    tab = tab_ref[...]

    def q_rope(c0):
        def rope(acc, p0, l0):
            def run():
                y = _norm_rope_tile(acc[:, l0:l0 + LANES], gq_ref[...], tab).astype(BF16)
                for j in range(tm // ATT_BLOCK):
                    q4_ref[j, (p0 + l0) // LANES] = y[j * ATT_BLOCK:(j + 1) * ATT_BLOCK, :]
            return run
        return lambda accs: [rope(acc, c0 + i * IN_PIECE, l0) for i, acc in enumerate(accs) for l0 in lane_tiles]

    att_q = [(chunk(W_AQ + c0, IN_TN), q_rope(c0)) for c0 in range(0, ATT_WIDTH, IN_TN)]

    left_half = lax.broadcasted_iota(jnp.int32, (tm, LANES), 1) < ATT_HEAD_DIM

    def k_norm_rope(accs):
        def norm_rope(acc, p0):
            def run():
                for t in range(IN_PIECE // LANES):
                    kt = _norm_rope_tile(acc[:, t * LANES:(t + 1) * LANES], gk_ref[...], tab)
                    sw = pltpu.roll(kt, ATT_HEAD_DIM, 1)
                    head = (p0 // LANES + t) * 2
                    variants = ((head, 0, jnp.where(left_half, kt, 0.0)), (head, 1, jnp.where(left_half, 0.0, sw)),
                                (head + 1, 0, jnp.where(left_half, sw, 0.0)), (head + 1, 1, jnp.where(left_half, 0.0, kt)))
                    for h, v, val in variants:
                        col = (2 * h + v) * LANES
                        kv_ref[:, col:col + LANES] = val.astype(BF16)
            return run
        return [norm_rope(acc, i * IN_PIECE) for i, acc in enumerate(accs)]

    att_k = (chunk(W_AK, ATT_KV_WIDTH), k_norm_rope)

    def vt_matmul():
        return lax.dot_general(wvt_ref[...], xn, _NT, preferred_element_type=F32) + bvt_ref[...]

    def store_vt(vt, r0):
        def run():
            if r0 < M_WIDTH:
                vtm_ref[r0:r0 + VT_TN, :] = vt[r0:r0 + VT_TN, :].astype(BF16)
            else:
                vta_ref[r0 - M_WIDTH:r0 - M_WIDTH + VT_TN, :] = vt[r0:r0 + VT_TN, :].astype(BF16)
        return run

    def vt_gate_epilogues(results):
        vt = results[0]
        gr = vt[VT_ROWS:VT_ROWS + N_IFG, :]
        return [store_vt(vt, r0) for r0 in range(0, VT_ROWS, VT_TN)] + [lambda: gate_epilogue(gr)]

    def gate_epilogue(gr):
        ls_r = _log_sigmoid(gr)
        r = lax.broadcasted_iota(jnp.int32, (M_CHUNK, M_CHUNK), 0)
        c = lax.broadcasted_iota(jnp.int32, (M_CHUNK, M_CHUNK), 1)
        tril = jnp.where(c <= r, 1.0, 0.0).astype(BF16)
        triu = jnp.where(c >= r, 1.0, 0.0).astype(BF16)
        li_f = gr[0:M_HEADS, :]
        li_b = gr[2 * M_HEADS:3 * M_HEADS, :]
        grow_ref[ROW_LI_F:ROW_LI_F + M_HEADS, :] = li_f
        grow_ref[ROW_LI_B:ROW_LI_B + M_HEADS, :] = li_b
        b_f, b_b = [], []
        for ci in range(tm // M_CHUNK):
            rows = slice(ci * M_CHUNK, (ci + 1) * M_CHUNK)
            b_f.append(_split_dot_right(ls_r[M_HEADS:2 * M_HEADS, rows], triu))
            b_b.append(_split_dot_right(ls_r[3 * M_HEADS:4 * M_HEADS, rows], tril))
            grow_ref[ROW_B_F:ROW_B_F + M_HEADS, rows] = b_f[-1]
            grow_ref[ROW_B_B:ROW_B_B + M_HEADS, rows] = b_b[-1]
        gap = jnp.zeros((LANE_C_B - LANE_C_F - M_HEADS, tm), F32)
        tail = jnp.zeros((LANES - LANE_C_B - M_HEADS, tm), F32)
        c_rows = jnp.concatenate([li_f - jnp.concatenate(b_f, axis=1), gap,
                                  li_b - jnp.concatenate(b_b, axis=1), tail], axis=0)
        for ti, term in enumerate(_bf16_terms(c_rows.T)):
            ccol_ref[ti] = term

    light = plain
    heavy = [([vt_matmul], vt_gate_epilogues),
             conv[0], att_q[0], conv[1], att_q[1], conv[2], att_k, conv[3]]
    stages.append(heavy[0])
    for i, stage in enumerate(heavy[1:]):
        stages.extend([stage, light[i]])
    stages.extend(light[len(heavy) - 1:])

    n_matmuls = sum(len(matmuls) for matmuls, _ in stages)
    queue = []
    emitted = 0
    for matmuls, make_epilogues in stages:
        results = []
        for matmul in matmuls:
            results.append(matmul())
            emitted += 1
            take = max(IN_EPILOGUES_PER_MATMUL, -(-len(queue) // max(n_matmuls - emitted, 1))) if queue else 0
            for run in queue[:take]:
                run()
            queue = queue[take:]
        queue.extend(make_epilogues(results))
    for run in queue:
        run()


def _in_proj(x, lw, rope_tab, seq):
    t = x.shape[0]
    tps = seq // IN_TM
    halo_blocks = IN_TM // IN_HALO
    const = lambda i: (0, 0)
    tile = lambda i: (i, 0)
    cols = lambda i: (0, i)
    resident = dict(pipeline_mode=pl.Buffered(1))
    return pl.pallas_call(
        functools.partial(_in_proj_kernel, tiles_per_seq=tps),
        grid=(t // IN_TM,),
        in_specs=[
            pl.BlockSpec((IN_TM, D_MODEL), tile),
            pl.BlockSpec((IN_HALO, D_MODEL), lambda i: (jnp.maximum(i * halo_blocks - 1, 0), 0)),
            pl.BlockSpec((IN_HALO, D_MODEL), lambda i: (jnp.minimum((i + 1) * halo_blocks, t // IN_HALO - 1), 0)),
            pl.BlockSpec((1, D_MODEL), const),
            pl.BlockSpec((D_MODEL, W_WIDTH), const, **resident),
            pl.BlockSpec((1, W_WIDTH), const),
            pl.BlockSpec((VT_ROWS + N_IFG, D_MODEL), const, **resident),
            pl.BlockSpec((VT_ROWS + N_IFG, 1), const),
            pl.BlockSpec((CONV_K, 2 * M_WIDTH), const),
            pl.BlockSpec((3, IN_TM, LANES), lambda i: (0, i % tps, 0)),
            pl.BlockSpec((1, LANES), const),
            pl.BlockSpec((1, LANES), const),
        ],
        out_specs=[
            pl.BlockSpec((IN_TM, P_WIDTH), tile),
            pl.BlockSpec((IN_TM // ATT_BLOCK, ATT_PAIRS, ATT_BLOCK, LANES), lambda i: (i, 0, 0, 0)),
            pl.BlockSpec((IN_TM, ATT_KVAR_WIDTH), tile),
            pl.BlockSpec((ATT_KV_WIDTH, IN_TM), cols),
            pl.BlockSpec((M_WIDTH, IN_TM), cols),
            pl.BlockSpec((IN_TM, M_WIDTH), tile),
            pl.BlockSpec((M_WIDTH, IN_TM), cols),
            pl.BlockSpec((N_IFG, IN_TM), cols),
            pl.BlockSpec((3, IN_TM, LANES), lambda i: (0, i, 0)),
        ],
        out_shape=[
            jax.ShapeDtypeStruct((t, P_WIDTH), PROJ_DTYPE),
            jax.ShapeDtypeStruct((t // ATT_BLOCK, ATT_PAIRS, ATT_BLOCK, LANES), BF16),
            jax.ShapeDtypeStruct((t, ATT_KVAR_WIDTH), BF16),
            jax.ShapeDtypeStruct((ATT_KV_WIDTH, t), BF16),
            jax.ShapeDtypeStruct((M_WIDTH, t), BF16),
            jax.ShapeDtypeStruct((t, M_WIDTH), BF16),
            jax.ShapeDtypeStruct((M_WIDTH, t), BF16),
            jax.ShapeDtypeStruct((N_IFG, t), F32),
            jax.ShapeDtypeStruct((3, t, LANES), BF16),
        ],
        scratch_shapes=[pltpu.VMEM((IN_TM + 2 * IN_HALO, D_MODEL), BF16),
                        pltpu.VMEM((2, IN_TM + 2 * IN_HALO, LANES), F32)],
        compiler_params=pltpu.CompilerParams(
            dimension_semantics=("arbitrary",), vmem_limit_bytes=VMEM_LIMIT_BYTES),
        name="in_proj",
    )(x, x, x, lw["norm_g"], lw["w_main"], lw["b_main"], lw["w_vt"], lw["b_vt"],
      lw["conv_w"], rope_tab, lw["gq"], lw["gk"])


def _attention_kernel(q4_ref, z_ref, kp_ref, kc_ref, kn_ref, vp_ref, vc_ref, vn_ref, sink_ref, out_ref,
                      ks, vts, out_t, *, n_blocks):
    ti = pl.program_id(1)
    bq = ATT_BLOCK

    r0 = 0
    for k_ref, v_ref in ((kp_ref, vp_ref), (kc_ref, vc_ref), (kn_ref, vn_ref)):
        nr = k_ref.shape[0]
        ks[r0:r0 + nr, :] = k_ref[...]
        vts[:, r0:r0 + nr] = v_ref[...]
        r0 += nr

    def group_heads(h):
        return (ATT_GROUP * h, ATT_GROUP * h + 2, ATT_GROUP * h + 1, ATT_GROUP * h + 3)

    sink_rows = [jnp.concatenate([jnp.broadcast_to(sink_ref[:, c:c + 1] * LOG2E, (1, bq)) for c in group_heads(h)],
                                 axis=1) for h in range(ATT_KV_HEADS)]

    kk = lax.broadcasted_iota(jnp.int32, (bq, bq), 0)
    qq = lax.broadcasted_iota(jnp.int32, (bq, bq), 1)
    prev_in_band = kk >= qq
    next_in_band = kk <= qq

    JG = ATT_SUB_GROUP

    def sub_blocks(jg, carry):
        js = [jg * JG + u for u in range(JG)]
        row0s = [pl.multiple_of(j * bq, bq) for j in js]
        biases = []
        for j in js:
            blk = ti * ATT_NSB + j
            bias_prev = jnp.where(prev_in_band, jnp.where(blk > 0, 0.0, NEG), NEG).astype(F32)
            bias_next = jnp.where(next_in_band, jnp.where(blk < n_blocks - 1, 0.0, NEG), NEG).astype(F32)
            biases.append((jnp.concatenate([bias_prev] * ATT_GROUP, axis=1),
                           jnp.concatenate([bias_next] * ATT_GROUP, axis=1)))
        q_slabs = [q4_ref[j] for j in js]
        pairs = [(u, h) for u in range(JG) for h in range(ATT_KV_HEADS)]
        q_pairs = [q_slabs[u][2 * h:2 * h + 2].reshape(2 * bq, LANES) for u, h in pairs]
        ss = [jnp.concatenate(
            [lax.dot_general(ks[pl.ds(row0s[u], 3 * bq), (2 * h + v) * LANES:(2 * h + v + 1) * LANES], q_pair, _NT,
                             preferred_element_type=F32)
             for v in range(2)], axis=1) for (u, h), q_pair in zip(pairs, q_pairs)]
        ss = [jnp.concatenate([s[:bq] + biases[u][0], s[bq:2 * bq], s[2 * bq:] + biases[u][1]], axis=0)
              for (u, h), s in zip(pairs, ss)]
        ms = [jnp.maximum(jnp.max(s, axis=0, keepdims=True), sink_rows[h]) for (u, h), s in zip(pairs, ss)]
        ps = [jnp.exp2(s - m) for s, m in zip(ss, ms)]
        denoms = [jnp.sum(p, axis=0, keepdims=True) + jnp.exp2(sink_rows[h] - m)
                  for (u, h), p, m in zip(pairs, ps, ms)]
        os_ = [jnp.dot(vts[h * ATT_HEAD_DIM:(h + 1) * ATT_HEAD_DIM, pl.ds(row0s[u], 3 * bq)], p.astype(BF16),
                       preferred_element_type=F32) * (1.0 / denom)
               for (u, h), p, denom in zip(pairs, ps, denoms)]
        for (u, h), o in zip(pairs, os_):
            for i, c in enumerate(group_heads(h)):
                out_t[u, c * ATT_HEAD_DIM:(c + 1) * ATT_HEAD_DIM, :] = o[:, i * bq:(i + 1) * bq]
        for u in range(JG):
            att = out_t[u].T
            z = z_ref[pl.ds(row0s[u], bq), :].astype(F32)
            out_ref[pl.ds(row0s[u], bq), :] = (att * _silu(z)).astype(out_ref.dtype)
        return carry

    lax.fori_loop(0, ATT_NSB // JG, sub_blocks, 0)


def _attention(proj, q4, kv, vta, sink, batch, seq):
    t = proj.shape[0]
    nb = seq // ATT_BLOCK
    nt = seq // ATT_TQ
    bq = ATT_BLOCK

    prev_blk = lambda b, i: b * nb + jnp.maximum(i * ATT_NSB - 1, 0)
    next_blk = lambda b, i: b * nb + jnp.minimum((i + 1) * ATT_NSB, nb - 1)
    return pl.pallas_call(
        functools.partial(_attention_kernel, n_blocks=nb),
        grid=(batch, nt),
        in_specs=[
            pl.BlockSpec((ATT_NSB, ATT_PAIRS, bq, LANES), lambda b, i: (b * nt + i, 0, 0, 0)),
            pl.BlockSpec((ATT_TQ, ATT_WIDTH), lambda b, i: (b * nt + i, P_AZ // ATT_WIDTH)),
            pl.BlockSpec((bq, ATT_KVAR_WIDTH), lambda b, i: (prev_blk(b, i), 0)),
            pl.BlockSpec((ATT_TQ, ATT_KVAR_WIDTH), lambda b, i: (b * nt + i, 0)),
            pl.BlockSpec((bq, ATT_KVAR_WIDTH), lambda b, i: (next_blk(b, i), 0)),
            pl.BlockSpec((ATT_KV_WIDTH, bq), lambda b, i: (0, prev_blk(b, i))),
            pl.BlockSpec((ATT_KV_WIDTH, ATT_TQ), lambda b, i: (0, b * nt + i)),
            pl.BlockSpec((ATT_KV_WIDTH, bq), lambda b, i: (0, next_blk(b, i))),
            pl.BlockSpec((1, ATT_HEADS), lambda b, i: (0, 0)),
        ],
        out_specs=pl.BlockSpec((ATT_TQ, ATT_WIDTH), lambda b, i: (b * nt + i, 0)),
        out_shape=jax.ShapeDtypeStruct((t, ATT_WIDTH), BF16),
        scratch_shapes=[
            pltpu.VMEM((ATT_TQ + 2 * bq, ATT_KVAR_WIDTH), BF16),
            pltpu.VMEM((ATT_KV_WIDTH, ATT_TQ + 2 * bq), BF16),
            pltpu.VMEM((ATT_SUB_GROUP, ATT_WIDTH, bq), F32),
        ],
        compiler_params=pltpu.CompilerParams(
            dimension_semantics=("arbitrary", "arbitrary"), vmem_limit_bytes=VMEM_LIMIT_BYTES),
        name="attention",
    )(q4, proj, kv, kv, kv, vta, vta, vta, sink)


def _mlstm_kernel(qct_ref, kc_ref, vt_ref, o_ref, z_ref, grow_ref, ccol_ref, ng_ref,
                  out_ref, cct_s, nf_s, nb_s, mf_s, mb_s, *, seq):
    hd = pl.program_id(1)
    L = M_CHUNK
    dv = M_HEAD_DIM
    nc = seq // L

    head_row = lax.broadcasted_iota(jnp.int32, (M_HEADS, L), 0) == hd

    def gate_row(base, t0):
        rows = grow_ref[base:base + M_HEADS, pl.ds(t0, L)]
        return jnp.sum(jnp.where(head_row, rows, 0.0), axis=0, keepdims=True)

    U = min(MLSTM_SCAN_GROUP, nc)
    dirs = ((ROW_LI_F, ROW_B_F, L - 1, nf_s, mf_s, 0), (ROW_LI_B, ROW_B_B, 0, nb_s, mb_s, dv))

    def scan_body(gj, carry):
        cis = [[gj * U + u for u in range(U)], [nc - 1 - (gj * U + u) for u in range(U)]]
        t0s = [[pl.multiple_of(ci * L, L) for ci in row] for row in cis]
        lis = [[gate_row(dirs[d][0], t0) for t0 in t0s[d]] for d in range(2)]
        bs = [[gate_row(dirs[d][1], t0) for t0 in t0s[d]] for d in range(2)]
        b_lasts = [[b[:, dirs[d][2]:dirs[d][2] + 1] for b in bs[d]] for d in range(2)]
        gs = [[b_last - b + li for b_last, b, li in zip(b_lasts[d], bs[d], lis[d])] for d in range(2)]
        g_maxs = [[jnp.max(g, axis=-1, keepdims=True) for g in gs[d]] for d in range(2)]
        ks = [[kc_ref[pl.ds(t0, L), :] for t0 in t0s[d]] for d in range(2)]
        vts = [[vt_ref[:, pl.ds(t0, L)].astype(F32) for t0 in t0s[d]] for d in range(2)]

        ms = [[carry[d][2]] for d in range(2)]
        for d in range(2):
            for u in range(U):
                ms[d].append(jnp.maximum(b_lasts[d][u] + ms[d][u], g_maxs[d][u]))
        w_cs = [[jnp.exp(b_lasts[d][u] + ms[d][u] - ms[d][u + 1]) for u in range(U)] for d in range(2)]
        w_ks = [[jnp.exp(gs[d][u] - ms[d][u + 1]) for u in range(U)] for d in range(2)]
        c_upds = [[jnp.dot((vts[d][u] * w_ks[d][u]).astype(BF16), ks[d][u], preferred_element_type=F32)
                   for u in range(U)] for d in range(2)]
        n_upds = [[jnp.dot(jnp.broadcast_to(w_ks[d][u], (SUBLANES, L)).astype(BF16), ks[d][u],
                           preferred_element_type=F32)[0:1, :] for u in range(U)] for d in range(2)]

        out = []
        for d in range(2):
            _, _, _, n_s, m_s, row0 = dirs[d]
            ct_st, n_st, _ = carry[d]
            for u in range(U):
                ci = cis[d][u]
                cct_s[ci, row0:row0 + dv, :] = ct_st.astype(BF16)
                n_s[ci] = jnp.broadcast_to(n_st, (SUBLANES, LANES))
                m_s[ci] = jnp.broadcast_to(ms[d][u], (SUBLANES, LANES))
                ct_st = w_cs[d][u] * ct_st + c_upds[d][u]
                n_st = w_cs[d][u] * n_st + n_upds[d][u]
            out.append((ct_st, n_st, ms[d][U]))
        return tuple(out)

    rr = lax.broadcasted_iota(jnp.int32, (L, L), 0)
    cc = lax.broadcasted_iota(jnp.int32, (L, L), 1)
    row8 = lax.broadcasted_iota(jnp.int32, (SUBLANES, LANES), 0)
    sel_k = lax.broadcasted_iota(jnp.int32, (LANES, 2 * L), 0)
    sel_n = lax.broadcasted_iota(jnp.int32, (LANES, 2 * L), 1)
    sel_lane = jnp.where(sel_n < L, LANE_C_F + hd, LANE_C_B + hd)
    sel = jnp.where(sel_k == sel_lane, 1.0, 0.0).astype(BF16)
    keep_f = rr <= cc
    keep_b = rr >= cc
    G = min(MLSTM_OUT_GROUP, nc)

    def out_pre(gi):
        cis = [gi * G + g for g in range(G)]
        t0s = [ci * L for ci in cis]
        qts = [qct_ref[:, t0:t0 + L] for t0 in t0s]
        qk_ts = [jnp.dot(kc_ref[t0:t0 + L, :], qt, preferred_element_type=F32)
                 for t0, qt in zip(t0s, qts)]
        c_bcs = [sum(jnp.dot(ccol_ref[ti, t0:t0 + L, :], sel, preferred_element_type=F32) for ti in range(3))
                 for t0 in t0s]
        items = []
        for g in range(G):
            items.append((g, c_bcs[g][:, :L], ROW_B_F, keep_f, mf_s))
            items.append((g, c_bcs[g][:, L:], ROW_B_B, keep_b, mb_s))
        b_rs = [gate_row(b_base, t0s[g]) for g, _, b_base, _, _ in items]
        ds = [jnp.where(keep, b_r + c_s, NEG)
              for (_, c_s, _, keep, _), b_r in zip(items, b_rs)]
        d_maxs = [jnp.max(d, axis=0, keepdims=True) for d in ds]
        return cis, t0s, qts, qk_ts, items, b_rs, ds, d_maxs

    def out_post(pre):
        cis, t0s, qts, qk_ts, items, b_rs, ds, d_maxs = pre
        cqs = [jnp.dot(cct_s[ci], qt, preferred_element_type=F32) for ci, qt in zip(cis, qts)]
        n8s = [jnp.where(row8 == 0, nf_s[ci], jnp.where(row8 == 1, nb_s[ci], 0.0)).astype(BF16) for ci in cis]
        qns = [jnp.dot(n8, qt, preferred_element_type=F32) for n8, qt in zip(n8s, qts)]
        qn_rows = [qns[it[0]][i % 2:i % 2 + 1, :] for i, it in enumerate(items)]
        m_prevs = [it[4][cis[it[0]]][0:1, 0:1] for it in items]
        m_inters = [b_r + m_prev for b_r, m_prev in zip(b_rs, m_prevs)]
        m_ts = [jnp.maximum(m_inter, d_max) for m_inter, d_max in zip(m_inters, d_maxs)]
        w_inters = [jnp.exp(m_inter - m_t) for m_inter, m_t in zip(m_inters, m_ts)]
        a_s = [jnp.exp(d - m_t) * qk_ts[it[0]] for d, m_t, it in zip(ds, m_ts, items)]
        dens = [w_inter * qn + jnp.sum(a, axis=0, keepdims=True) for w_inter, qn, a in zip(w_inters, qn_rows, a_s)]
        invs = [1.0 / jnp.maximum(jnp.abs(den), jnp.exp(-m_t)) for den, m_t in zip(dens, m_ts)]
        ps = [a * inv for a, inv in zip(a_s, invs)]
        ws = [w_inter * inv for w_inter, inv in zip(w_inters, invs)]

        h_ts = [jnp.dot(vt_ref[:, t0s[g]:t0s[g] + L], (ps[2 * g] + ps[2 * g + 1]).astype(BF16),
                        preferred_element_type=F32)
                + cqs[g][:dv, :] * ws[2 * g] + cqs[g][dv:, :] * ws[2 * g + 1] for g in range(G)]
        hs = [_sigmoid(o_ref[t0:t0 + L, :].astype(F32)) * h_t.T for t0, h_t in zip(t0s, h_ts)]
        hs = [h * lax.rsqrt(jnp.mean(h * h, axis=-1, keepdims=True) + NORM_EPS) * ng_ref[...] for h in hs]
        for t0, h in zip(t0s, hs):
            out_ref[t0:t0 + L, :] = (h * _silu(z_ref[t0:t0 + L, :].astype(F32))).astype(out_ref.dtype)

    pre = out_pre(0)
    init = (jnp.zeros((dv, M_HEAD_DIM), F32), jnp.zeros((1, M_HEAD_DIM), F32),
            jnp.full((1, 1), NEG, F32))
    lax.fori_loop(0, nc // U, scan_body, (init, init))
    for gi in range(nc // G):
        nxt = out_pre(gi + 1) if gi + 1 < nc // G else None
        out_post(pre)
        pre = nxt


def _mlstm(proj, qct, kc, vtm, grow, ccol, ng, batch, seq):
    t = proj.shape[0]
    nc = seq // M_CHUNK
    head_rows = lambda b, h: (h, b)
    head_cols = lambda b, h: (b, h)

    def proj_head(col0):
        return pl.BlockSpec((seq, M_HEAD_DIM), lambda b, h: (b, col0 // M_HEAD_DIM + h))

    return pl.pallas_call(
        functools.partial(_mlstm_kernel, seq=seq),
        grid=(batch, M_HEADS),
        in_specs=[
            pl.BlockSpec((M_HEAD_DIM, seq), head_rows),
            pl.BlockSpec((seq, M_HEAD_DIM), head_cols),
            pl.BlockSpec((M_HEAD_DIM, seq), head_rows),
            proj_head(P_MO), proj_head(P_MZ),
            pl.BlockSpec((N_IFG, seq), lambda b, h: (0, b)),
            pl.BlockSpec((3, seq, LANES), lambda b, h: (0, b, 0)),
            pl.BlockSpec((1, M_HEAD_DIM), lambda b, h: (0, h)),
        ],
        out_specs=pl.BlockSpec((seq, M_HEAD_DIM), head_cols),
        out_shape=jax.ShapeDtypeStruct((t, M_WIDTH), BF16),
        scratch_shapes=[
            pltpu.VMEM((nc, 2 * M_HEAD_DIM, M_HEAD_DIM), BF16),
            pltpu.VMEM((nc, SUBLANES, M_HEAD_DIM), F32),
            pltpu.VMEM((nc, SUBLANES, M_HEAD_DIM), F32),
            pltpu.VMEM((nc, SUBLANES, LANES), F32),
            pltpu.VMEM((nc, SUBLANES, LANES), F32),
        ],
        compiler_params=pltpu.CompilerParams(
            dimension_semantics=("arbitrary", "arbitrary"), vmem_limit_bytes=VMEM_LIMIT_BYTES),
        name="mlstm",
    )(qct, kc, vtm, proj, proj, grow, ccol, ng)


def _out_proj_kernel(x_ref, a_ref, m_ref, gates_ref, wa_ref, wm_ref, wo_ref, out_ref):
    branch_a = jnp.dot(a_ref[...], wa_ref[...], preferred_element_type=F32)
    branch_m = jnp.dot(m_ref[...], wm_ref[...], preferred_element_type=F32)
    gates = _sigmoid(gates_ref[...].astype(F32))
    merged = gates[:, :D_MODEL] * branch_a + gates[:, D_MODEL:] * branch_m
    out_ref[...] = x_ref[...] + jnp.dot(merged.astype(BF16), wo_ref[...], preferred_element_type=F32)


def _out_proj(x, a, m, proj, wa, wm, wo):
    t = x.shape[0]
    tile = lambda i: (i, 0)
    const = lambda i: (0, 0)
    return pl.pallas_call(
        _out_proj_kernel,
        grid=(t // OUT_TM,),
        in_specs=[
            pl.BlockSpec((OUT_TM, D_MODEL), tile),
            pl.BlockSpec((OUT_TM, ATT_WIDTH), tile),
            pl.BlockSpec((OUT_TM, M_WIDTH), tile),
            pl.BlockSpec((OUT_TM, 2 * D_MODEL), lambda i: (i, P_GATES // (2 * D_MODEL))),
            pl.BlockSpec((ATT_WIDTH, D_MODEL), const),
            pl.BlockSpec((M_WIDTH, D_MODEL), const),
            pl.BlockSpec((D_MODEL, D_MODEL), const),
        ],
        out_specs=pl.BlockSpec((OUT_TM, D_MODEL), tile),
        out_shape=jax.ShapeDtypeStruct((t, D_MODEL), F32),
        compiler_params=pltpu.CompilerParams(
            dimension_semantics=("arbitrary",), vmem_limit_bytes=VMEM_LIMIT_BYTES),
        name="out_proj",
    )(x, a, m, proj, wa, wm, wo)


def _rope_tables(seq):
    inv = jnp.power(jnp.float32(ROPE_THETA), -jnp.arange(ROPE_HALF, dtype=F32) * 2.0 / ROPE_DIM)
    ang = jnp.arange(seq, dtype=F32)[:, None] * inv[None, :]
    cos, sin = jnp.cos(ang), jnp.sin(ang)
    pad = ATT_HEAD_DIM - ROPE_DIM
    ones = jnp.ones((seq, pad), F32)
    zeros = jnp.zeros((seq, pad), F32)
    zh = jnp.zeros((seq, ROPE_HALF), F32)
    c = jnp.concatenate([cos, cos, ones], axis=-1)
    s1 = jnp.concatenate([-sin, zh, zeros], axis=-1)
    s2 = jnp.concatenate([zh, sin, zeros], axis=-1)
    tab = jnp.stack([c, s1, s2])
    return jnp.concatenate([tab, tab], axis=-1)


def _prep_weights(w_in, b_in):
    order = [(_R_GATES, 2 * D_MODEL), (_R_AZ, ATT_WIDTH), (_R_MO, M_WIDTH), (_R_MZ, M_WIDTH),
             (_R_MQ, M_WIDTH), (_R_MK, M_WIDTH), (_R_AQ, ATT_WIDTH), (_R_AK, ATT_KV_WIDTH)]
    w_main = jnp.concatenate([w_in[:, :, o:o + n] for o, n in order], axis=-1).astype(BF16)
    b_main = jnp.concatenate([b_in[:, o:o + n] for o, n in order], axis=-1)[:, None, :]
    vt_order = [(_R_MV, M_WIDTH), (_R_AV, ATT_KV_WIDTH), (_R_IF, N_IFG)]
    w_vt = jnp.swapaxes(jnp.concatenate([w_in[:, :, o:o + n] for o, n in vt_order], axis=-1), 1, 2).astype(BF16)
    b_vt = jnp.concatenate([b_in[:, o:o + n] for o, n in vt_order], axis=-1)[:, :, None]
    return w_main, b_main, w_vt, b_vt


def _trunk(x, rope_tab, layers):
    batch, seq, _ = x.shape
    xf = x.reshape(batch * seq, D_MODEL)
    for lw in layers:
        proj, q4, kv, vta, qct, kc, vtm, grow, ccol = _in_proj(xf, lw, rope_tab, seq)
        a = _attention(proj, q4, kv, vta, lw["sink"], batch, seq)
        m = _mlstm(proj, qct, kc, vtm, grow, ccol, lw["m_norm_g"], batch, seq)
        xf = _out_proj(xf, a, m, proj, lw["w_att_out"], lw["w_m_out"], lw["w_out"])
    return xf.reshape(batch, seq, D_MODEL)


def kernel(x_prompt, x_sample, norm_g, w_in, b_in, q_norm_g, k_norm_g, sink, conv_w, m_norm_g,
           w_att_out, w_m_out, w_out):
    w_main, b_main, w_vt, b_vt = _prep_weights(w_in, b_in)
    wa, wm, wo = w_att_out.astype(BF16), w_m_out.astype(BF16), w_out.astype(BF16)
    layers = []
    for l in range(DEPTH):
        layers.append(dict(
            norm_g=norm_g[l][None, :], w_main=w_main[l], b_main=b_main[l], w_vt=w_vt[l], b_vt=b_vt[l],
            gq=jnp.tile(q_norm_g[l] * (ATT_HEAD_DIM ** -0.5 * LOG2E), 2)[None, :],
            gk=jnp.tile(k_norm_g[l], 2)[None, :],
            sink=sink[l][None, :], conv_w=conv_w[l], m_norm_g=m_norm_g[l][None, :],
            w_att_out=wa[l], w_m_out=wm[l], w_out=wo[l]))
    outs = []
    for x in (x_prompt, x_sample):
        outs.append(_trunk(x, _rope_tables(x.shape[1]), layers))
    return tuple(outs)
```
